```python
import math
import jax
import jax.numpy as jnp
from jax import lax
import numpy as np

D_MODEL = 2048
BATCH = 4
SEQ = 4096
DEPTH = 2

GRID_W = 64
CTX_LEN = 256
HEAD_DIM = 128
AXIS_DIM = HEAD_DIM // 2
ROPE_THETA = 10000.0
Q_BLOCK = 128
A_HEADS = 8
A_KV_HEADS = 2
B_HEADS = 8
B_KV_HEADS = 2
WINDOW = 128
C_QK_HEADS = 4
C_V_HEADS = 8
CONV_K = 5
CHUNK = 64
N_EXPERTS = 64
TOP_K = 8
EXPERT_FF = 512
SHARED_FF = 512
ROUTED_SCALE = 2.5
MOE_BLOCK = 128
MOD_WIDTH = 6 * D_MODEL
DEEPNORM_ALPHA = (2.0 * DEPTH) ** 0.25
DEEPNORM_BETA = (8.0 * DEPTH) ** -0.25
LN_EPS = 1e-5
RMS_EPS = 1e-6

A_Q = A_HEADS * HEAD_DIM
A_KV = A_KV_HEADS * HEAD_DIM
B_Q = B_HEADS * HEAD_DIM
B_KV = B_KV_HEADS * HEAD_DIM
C_QK = C_QK_HEADS * HEAD_DIM
C_V = C_V_HEADS * HEAD_DIM
IN_SPLITS = (A_Q, A_KV, A_KV, B_Q, B_KV, B_KV, C_QK, C_QK, C_V, C_V, 2 * C_V_HEADS, 2 * C_V_HEADS, 3 * D_MODEL)
IN_WIDTH = sum(IN_SPLITS)
IN_OFFSETS = tuple(int(o) for o in np.cumsum(IN_SPLITS)[:-1])

kernel_name = 'hybrid_dit_gqa_swa_gdn_moe'


def _layernorm(x):
    xf = x.astype(jnp.float32)
    mu = jnp.mean(xf, axis=-1, keepdims=True)
    var = jnp.mean(jnp.square(xf - mu), axis=-1, keepdims=True)
    return (xf - mu) * lax.rsqrt(var + LN_EPS)


def _post_norm(x, g, b):
    return (_layernorm(x) * g.astype(jnp.float32) + b.astype(jnp.float32)).astype(x.dtype)


def _rmsnorm(x, g):
    xf = x.astype(jnp.float32)
    y = xf * lax.rsqrt(jnp.mean(jnp.square(xf), axis=-1, keepdims=True) + RMS_EPS)
    return (y * g.astype(jnp.float32)).astype(x.dtype)


def _l2norm(x):
    return x * lax.rsqrt(jnp.sum(jnp.square(x), axis=-1, keepdims=True) + RMS_EPS)


def _axial_tables(n_tokens, dtype):
    rows = n_tokens // GRID_W
    row = jnp.repeat(jnp.arange(rows, dtype=jnp.float32), GRID_W)
    col = jnp.tile(jnp.arange(GRID_W, dtype=jnp.float32), rows)
    inv_freq = ROPE_THETA ** (-jnp.arange(0, AXIS_DIM, 2, dtype=jnp.float32) / AXIS_DIM)
    ang_r = row[:, None] * inv_freq[None, :]
    ang_c = col[:, None] * inv_freq[None, :]
    ang = jnp.concatenate([ang_r, ang_r, ang_c, ang_c], axis=-1)
    return jnp.cos(ang).astype(dtype), jnp.sin(ang).astype(dtype)


def _rotate_half(t):
    a, b = jnp.split(t, 2, axis=-1)
    return jnp.concatenate([-b, a], axis=-1)


def _apply_axial_rope(t, cos, sin):
    t_row, t_col = jnp.split(t, 2, axis=-1)
    rot = jnp.concatenate([_rotate_half(t_row), _rotate_half(t_col)], axis=-1)
    return t * cos[None, :, None, :] + rot * sin[None, :, None, :]


def _heads(t, n_heads):
    return t.reshape(t.shape[:2] + (n_heads, HEAD_DIM))


def _group(q, n_kv):
    b, s, h, d = q.shape
    return q.reshape(b, s, n_kv, h // n_kv, d)


def _global_attn_latent(q, k_all, v_all):
    b, s, hkv, g, d = q.shape
    nb = s // Q_BLOCK
    scale = d ** -0.5
    q_blocks = jnp.moveaxis(q.reshape(b, nb, Q_BLOCK, hkv, g, d), 1, 0)

    def one_block(qi):
        sc = jnp.einsum('bqhgd,bkhd->bhgqk', qi, k_all).astype(jnp.float32) * scale
        p = jax.nn.softmax(sc, axis=-1).astype(v_all.dtype)
        return jnp.einsum('bhgqk,bkhd->bqhgd', p, v_all)

    o = lax.map(one_block, q_blocks)
    return jnp.moveaxis(o, 0, 1).reshape(b, s, hkv * g * d)


def _ctx_attn(q, k, v, sink=None):
    b, s, hkv, g, d = q.shape
    sc = jnp.einsum('bqhgd,bkhd->bhgqk', q, k).astype(jnp.float32) * d ** -0.5
    if sink is not None:
        col = jnp.broadcast_to(sink.reshape(1, hkv, g, 1, 1).astype(jnp.float32), sc.shape[:-1] + (1,))
        p = jax.nn.softmax(jnp.concatenate([sc, col], axis=-1), axis=-1)[..., :-1]
    else:
        p = jax.nn.softmax(sc, axis=-1)
    return jnp.einsum('bhgqk,bkhd->bqhgd', p.astype(v.dtype), v).reshape(b, s, hkv * g * d)


def _window_attn_latent(q, k, v, k_ctx, v_ctx, sink):
    b, s, hkv, g, d = q.shape
    nb = s // Q_BLOCK
    span = 3 * Q_BLOCK
    scale = d ** -0.5

    def band(t):
        tb = jnp.pad(t, ((0, 0), (Q_BLOCK, Q_BLOCK), (0, 0), (0, 0))).reshape(b, nb + 2, Q_BLOCK, hkv, d)
        return jnp.concatenate([tb[:, :-2], tb[:, 1:-1], tb[:, 2:]], axis=2)

    k_band, v_band = band(k), band(v)
    qb = q.reshape(b, nb, Q_BLOCK, hkv, g, d)
    s_win = jnp.einsum('bnqhgd,bnkhd->bhgnqk', qb, k_band).astype(jnp.float32) * scale
    s_ctx = jnp.einsum('bnqhgd,bchd->bhgnqc', qb, k_ctx).astype(jnp.float32) * scale
    q_pos = jnp.arange(s).reshape(nb, Q_BLOCK)
    k_pos = (jnp.arange(nb) * Q_BLOCK - Q_BLOCK)[:, None] + jnp.arange(span)[None, :]
    kp = k_pos[:, None, :]
    valid = (jnp.abs(kp - q_pos[:, :, None]) <= WINDOW) & (kp >= 0) & (kp < s)
    s_win = jnp.where(valid, s_win, -jnp.inf)
    sink_col = jnp.broadcast_to(sink.reshape(1, hkv, g, 1, 1, 1).astype(jnp.float32), s_win.shape[:-1] + (1,))
    p = jax.nn.softmax(jnp.concatenate([s_win, s_ctx, sink_col], axis=-1), axis=-1).astype(v.dtype)
    o = (jnp.einsum('bhgnqk,bnkhd->bnqhgd', p[..., :span], v_band)
         + jnp.einsum('bhgnqc,bchd->bnqhgd', p[..., span:span + k_ctx.shape[1]], v_ctx))
    return o.reshape(b, s, hkv * g * d)


def _short_conv(t, w):
    y = lax.conv_general_dilated(t, w[:, None, :].astype(t.dtype), window_strides=(1,),
                                 padding=[(CONV_K // 2, CONV_K // 2)],
                                 dimension_numbers=('NWC', 'WIO', 'NWC'),
                                 feature_group_count=t.shape[-1])
    return jax.nn.silu(y)


def _delta_inputs(qkv, a, bgate, a_log, dt_bias):
    b, n, _ = qkv.shape
    q, k, v = jnp.split(qkv.astype(jnp.float32), (C_QK, 2 * C_QK), axis=-1)
    rep = C_V_HEADS // C_QK_HEADS
    q = jnp.repeat(_l2norm(_heads(q, C_QK_HEADS)), rep, axis=2) * HEAD_DIM ** -0.5
    k = jnp.repeat(_l2norm(_heads(k, C_QK_HEADS)), rep, axis=2)
    v = _heads(v, C_V_HEADS)
    a = a.astype(jnp.float32).reshape(b, n, 2, C_V_HEADS)
    g = -jnp.exp(a_log.astype(jnp.float32)) * jax.nn.softplus(a + dt_bias.astype(jnp.float32))
    beta = jax.nn.sigmoid(bgate.astype(jnp.float32).reshape(b, n, 2, C_V_HEADS))
    return q, k, v, g, beta


def _delta_scan(q, k, v, g, beta, state0):
    b, n_tok, h, dk = q.shape
    dv = v.shape[-1]
    n = n_tok // CHUNK

    def to_chunks(t):
        t = t.reshape((b, n, CHUNK) + t.shape[2:])
        return jnp.moveaxis(t, (1, 2), (0, 3))

    qc, kc, vc = to_chunks(q), to_chunks(k), to_chunks(v)
    gc = jnp.cumsum(to_chunks(g), axis=-1)
    bc = to_chunks(beta)
    idx = jnp.arange(CHUNK)
    lower = idx[:, None] >= idx[None, :]
    strict = idx[:, None] > idx[None, :]
    decay = jnp.exp(jnp.where(lower, gc[..., :, None] - gc[..., None, :], -jnp.inf))
    kb = kc * bc[..., None]
    a_mat = jnp.where(strict, jnp.einsum('nbhik,nbhjk->nbhij', kb, kc) * decay, 0.0)
    eye = jnp.eye(CHUNK, dtype=jnp.float32)
    t_inv = lax.linalg.triangular_solve(eye + a_mat, jnp.broadcast_to(eye, a_mat.shape), left_side=True, lower=True)
    u = t_inv @ (vc * bc[..., None])
    w = t_inv @ (kb * jnp.exp(gc)[..., None])
    qk = jnp.einsum('nbhik,nbhjk->nbhij', qc, kc) * decay
    q_dec = qc * jnp.exp(gc)[..., None]
    g_last = gc[..., -1]
    k_dec = kc * jnp.exp(g_last[..., None] - gc)[..., None]

    def step(state, xs):
        u_i, w_i, qd_i, qk_i, kd_i, gl_i = xs
        v_new = u_i - w_i @ state
        o_i = qd_i @ state + qk_i @ v_new
        state = state * jnp.exp(gl_i)[..., None, None] + jnp.einsum('bhck,bhcv->bhkv', kd_i, v_new)
        return state, o_i

    state_fin, o = lax.scan(step, state0, (u, w, q_dec, qk, k_dec, g_last))
    o = jnp.moveaxis(o, (0, 3), (1, 2)).reshape(b, n_tok, h, dv)
    return o, state_fin


def _bidir_delta(ctx_in, lat_in):
    qc, kc, vc, gc, bc = ctx_in
    ql, kl, vl, gl, bl = lat_in
    state0 = jnp.zeros((ql.shape[0], C_V_HEADS, HEAD_DIM, HEAD_DIM), jnp.float32)

    def flip(t):
        return jnp.flip(t, axis=1)

    o_cf, s_f = _delta_scan(qc, kc, vc, gc[:, :, 0], bc[:, :, 0], state0)
    o_lf, _ = _delta_scan(ql, kl, vl, gl[:, :, 0], bl[:, :, 0], s_f)
    o_cb, s_b = _delta_scan(flip(qc), flip(kc), flip(vc), flip(gc[:, :, 1]), flip(bc[:, :, 1]), state0)
    o_lb, _ = _delta_scan(flip(ql), flip(kl), flip(vl), flip(gl[:, :, 1]), flip(bl[:, :, 1]), s_b)
    return o_cf + flip(o_cb), o_lf + flip(o_lb)


def _token_mixer(h, hc, w_in, q_norm_a, k_norm_a, sink_b, conv_c, a_log_c, dt_bias_c, norm_c,
                 w_br_a, w_br_b, w_br_c, w_out, cos, sin, need_ctx):
    aq, ak, av, bq, bk, bv, cq, ck, cv, cz, ca, cb, gates = jnp.split(h @ w_in, IN_OFFSETS, axis=-1)
    aq_c, ak_c, av_c, bq_c, bk_c, bv_c, cq_c, ck_c, cv_c, cz_c, ca_c, cb_c, gates_c = jnp.split(hc @ w_in, IN_OFFSETS, axis=-1)

    def rope(t):
        return _apply_axial_rope(t, cos, sin)

    qa = rope(_rmsnorm(_heads(aq, A_HEADS), q_norm_a))
    ka = rope(_rmsnorm(_heads(ak, A_KV_HEADS), k_norm_a))
    ka_c = _rmsnorm(_heads(ak_c, A_KV_HEADS), k_norm_a)
    va, va_c = _heads(av, A_KV_HEADS), _heads(av_c, A_KV_HEADS)
    o_a = _global_attn_latent(_group(qa, A_KV_HEADS), jnp.concatenate([ka_c, ka], axis=1),
                              jnp.concatenate([va_c, va], axis=1))
    qb = rope(_heads(bq, B_HEADS))
    kb = rope(_heads(bk, B_KV_HEADS))
    kb_c, vb, vb_c = _heads(bk_c, B_KV_HEADS), _heads(bv, B_KV_HEADS), _heads(bv_c, B_KV_HEADS)
    o_b = _window_attn_latent(_group(qb, B_KV_HEADS), kb, vb, kb_c, vb_c, sink_b)
    lat_in = _delta_inputs(_short_conv(jnp.concatenate([cq, ck, cv], axis=-1), conv_c), ca, cb, a_log_c, dt_bias_c)
    ctx_in = _delta_inputs(_short_conv(jnp.concatenate([cq_c, ck_c, cv_c], axis=-1), conv_c), ca_c, cb_c, a_log_c, dt_bias_c)
    o_c_ctx, o_c = _bidir_delta(ctx_in, lat_in)

    def gated_out(o, z):
        zf = _heads(z, C_V_HEADS).astype(jnp.float32)
        return (_rmsnorm(o, norm_c) * jax.nn.silu(zf)).reshape(o.shape[:2] + (C_V,)).astype(z.dtype)

    def merge(o_a_, o_b_, o_c_, gates_):
        g_a, g_b, g_c = jnp.split(jax.nn.sigmoid(gates_), 3, axis=-1)
        y_ = g_a * (o_a_ @ w_br_a) + g_b * (o_b_ @ w_br_b) + g_c * (o_c_ @ w_br_c)
        return y_ @ w_out

    y = merge(o_a, o_b, gated_out(o_c, cz), gates)
    if not need_ctx:
        return y, None
    qa_c = _rmsnorm(_heads(aq_c, A_HEADS), q_norm_a)
    o_a_c = _ctx_attn(_group(qa_c, A_KV_HEADS), ka_c, va_c)
    o_b_c = _ctx_attn(_group(_heads(bq_c, B_HEADS), B_KV_HEADS), kb_c, vb_c, sink_b)
    y_c = merge(o_a_c, o_b_c, gated_out(o_c_ctx, cz_c), gates_c)
    return y, y_c


def _swiglu(t, w_gate, w_up, w_down):
    return (jax.nn.silu(t @ w_gate) * (t @ w_up)) @ w_down


def _moe(t, w_router, router_bias, w1, w3, w2, ws1, ws3, ws2):
    n_tok, d = t.shape
    scores = jax.nn.sigmoid((t @ w_router).astype(jnp.float32))
    _, idx = lax.top_k(scores + router_bias.astype(jnp.float32), TOP_K)
    gate = jnp.take_along_axis(scores, idx, axis=-1)
    gate = gate / jnp.sum(gate, axis=-1, keepdims=True) * ROUTED_SCALE
    n_assign = n_tok * TOP_K
    n_blocks = -(-n_assign // MOE_BLOCK) + N_EXPERTS
    e_flat = idx.reshape(-1)
    order = jnp.argsort(e_flat)
    e_sorted = e_flat[order]
    counts = jnp.bincount(e_flat, length=N_EXPERTS)
    padded = (counts + MOE_BLOCK - 1) // MOE_BLOCK * MOE_BLOCK
    padded_end = jnp.cumsum(padded)
    dest = (padded_end - padded)[e_sorted] + jnp.arange(n_assign) - (jnp.cumsum(counts) - counts)[e_sorted]
    slot_tok = jnp.full((n_blocks * MOE_BLOCK,), n_tok, jnp.int32).at[dest].set((order // TOP_K).astype(jnp.int32))
    slot_gate = jnp.zeros((n_blocks * MOE_BLOCK,), jnp.float32).at[dest].set(gate.reshape(-1)[order])
    block_expert = jnp.minimum(jnp.searchsorted(padded_end, jnp.arange(n_blocks) * MOE_BLOCK, side='right'), N_EXPERTS - 1)
    t_pad = jnp.concatenate([t, jnp.zeros((1, d), t.dtype)], axis=0)

    def one_block(args):
        tok, gt, e = args
        return _swiglu(t_pad[tok], w1[e], w3[e], w2[e]) * gt[:, None].astype(t.dtype)

    out = lax.map(one_block, (slot_tok.reshape(n_blocks, MOE_BLOCK), slot_gate.reshape(n_blocks, MOE_BLOCK), block_expert))
    routed = jnp.zeros((n_tok + 1, d), t.dtype).at[slot_tok].add(out.reshape(-1, d))[:n_tok]
    return routed + _swiglu(t, ws1, ws3, ws2)


def setup_inputs(seed: int = 0) -> dict:
    key = jax.random.key(seed)
    ks = jax.random.split(key, 30)
    f32 = jnp.float32
    L, D, E = DEPTH, D_MODEL, N_EXPERTS

    def nrm(k, shape, fan_in, gain=1.0):
        return jax.random.normal(k, shape, f32) * (gain * fan_in ** -0.5)

    def near_one(k, shape):
        return 1.0 + 0.05 * jax.random.normal(k, shape, f32)

    dt = jnp.exp(jax.random.uniform(ks[12], (L, 2, C_V_HEADS), f32) * (math.log(0.1) - math.log(0.001)) + math.log(0.001))
    return {
        'x': jax.random.normal(ks[0], (BATCH, SEQ, D), f32),
        'c': jax.random.normal(ks[1], (BATCH, D), f32),
        'ctx': jax.random.normal(ks[2], (BATCH, CTX_LEN, D), f32),
        'c_ctx': jax.random.normal(ks[3], (D,), f32),
        'w_mod': nrm(ks[4], (L, D, MOD_WIDTH), D, 0.5),
        'b_mod': 0.02 * jax.random.normal(ks[5], (L, MOD_WIDTH), f32),
        'w_in': nrm(ks[6], (L, D, IN_WIDTH), D),
        'q_norm_a': near_one(ks[7], (L, HEAD_DIM)),
        'k_norm_a': near_one(ks[8], (L, HEAD_DIM)),
        'sink_b': 0.5 * jax.random.normal(ks[9], (L, B_HEADS), f32),
        'conv_c': nrm(ks[10], (L, CONV_K, 2 * C_QK + C_V), CONV_K),
        'a_log_c': jnp.log(jax.random.uniform(ks[11], (L, 2, C_V_HEADS), f32, 1.0, 16.0)),
        'dt_bias_c': dt + jnp.log(-jnp.expm1(-dt)),
        'norm_c': near_one(ks[13], (L, HEAD_DIM)),
        'w_br_a': nrm(ks[14], (L, A_Q, D), A_Q),
        'w_br_b': nrm(ks[15], (L, B_Q, D), B_Q),
        'w_br_c': nrm(ks[16], (L, C_V, D), C_V),
        'w_out': nrm(ks[17], (L, D, D), D, DEEPNORM_BETA),
        'ln1_g': near_one(ks[18], (L, D)),
        'ln1_b': 0.02 * jax.random.normal(ks[19], (L, D), f32),
        'w_router': nrm(ks[20], (L, D, E), D),
        'router_bias': 0.01 * jax.random.normal(ks[21], (L, E), f32),
        'w1': nrm(ks[22], (L, E, D, EXPERT_FF), D),
        'w3': nrm(ks[23], (L, E, D, EXPERT_FF), D),
        'w2': nrm(ks[24], (L, E, EXPERT_FF, D), EXPERT_FF, DEEPNORM_BETA),
        'ws1': nrm(ks[25], (L, D, SHARED_FF), D),
        'ws3': nrm(ks[26], (L, D, SHARED_FF), D),
        'ws2': nrm(ks[27], (L, SHARED_FF, D), SHARED_FF, DEEPNORM_BETA),
        'ln2_g': near_one(ks[28], (L, D)),
        'ln2_b': 0.02 * jax.random.normal(ks[29], (L, D), f32),
    }


def reference(x, c, ctx, c_ctx, w_mod, b_mod, w_in, q_norm_a, k_norm_a, sink_b, conv_c, a_log_c, dt_bias_c,
              norm_c, w_br_a, w_br_b, w_br_c, w_out, ln1_g, ln1_b, w_router, router_bias, w1, w3, w2,
              ws1, ws3, ws2, ln2_g, ln2_b):
    b, s, d = x.shape
    n_ctx = ctx.shape[1]
    cos, sin = _axial_tables(s, x.dtype)
    x = _layernorm(x).astype(x.dtype)
    xc = _layernorm(ctx).astype(ctx.dtype)
    silu_c = jax.nn.silu(c)
    silu_cc = jax.nn.silu(c_ctx)
    for l in range(DEPTH):
        need_ctx = l < DEPTH - 1
        sh1, sc1, gt1, sh2, sc2, gt2 = jnp.split(silu_c @ w_mod[l] + b_mod[l], 6, axis=-1)
        sh1c, sc1c, gt1c, sh2c, sc2c, gt2c = jnp.split(silu_cc @ w_mod[l] + b_mod[l], 6, axis=-1)
        h = x * (1 + sc1[:, None]) + sh1[:, None]
        hc = xc * (1 + sc1c) + sh1c
        y, y_c = _token_mixer(h, hc, w_in[l], q_norm_a[l], k_norm_a[l], sink_b[l], conv_c[l], a_log_c[l],
                              dt_bias_c[l], norm_c[l], w_br_a[l], w_br_b[l], w_br_c[l], w_out[l], cos, sin, need_ctx)
        x = _post_norm(DEEPNORM_ALPHA * x + gt1[:, None] * y, ln1_g[l], ln1_b[l])
        h = x * (1 + sc2[:, None]) + sh2[:, None]
        if need_ctx:
            xc = _post_norm(DEEPNORM_ALPHA * xc + gt1c * y_c, ln1_g[l], ln1_b[l])
            hc = xc * (1 + sc2c) + sh2c
            f = _moe(jnp.concatenate([h.reshape(-1, d), hc.reshape(-1, d)], axis=0), w_router[l], router_bias[l],
                     w1[l], w3[l], w2[l], ws1[l], ws3[l], ws2[l])
            f_lat = f[:b * s].reshape(b, s, d)
            f_ctx = f[b * s:].reshape(b, n_ctx, d)
            xc = _post_norm(DEEPNORM_ALPHA * xc + gt2c * f_ctx, ln2_g[l], ln2_b[l])
        else:
            f_lat = _moe(h.reshape(-1, d), w_router[l], router_bias[l], w1[l], w3[l], w2[l],
                         ws1[l], ws3[l], ws2[l]).reshape(b, s, d)
        x = _post_norm(DEEPNORM_ALPHA * x + gt2[:, None] * f_lat, ln2_g[l], ln2_b[l])
    return x
```

```python
import functools
import math

import jax
import jax.numpy as jnp
from jax import lax
from jax.experimental import pallas as pl
from jax.experimental.pallas import tpu as pltpu

F32 = jnp.float32
BF16 = jnp.bfloat16
HIGHEST = lax.Precision.HIGHEST

HEAD_DIM = 128
GRID_W = 64
ROPE_THETA = 10000.0
A_HEADS, A_KV_HEADS = 8, 2
B_HEADS, B_KV_HEADS = 8, 2
WINDOW = 128
Q_BLOCK = 128
C_QK_HEADS, C_V_HEADS = 4, 8
CONV_K = 5
CHUNK = 64
TOP_K = 8
ROUTED_SCALE = 2.5
LN_EPS = 1e-5
RMS_EPS = 1e-6
MOE_ROWS = 256
N_MOD = 6
HALO = 8
VMEM_LIMIT = 56 * 2**20

A_Q, A_KV = A_HEADS * HEAD_DIM, A_KV_HEADS * HEAD_DIM
B_Q, B_KV = B_HEADS * HEAD_DIM, B_KV_HEADS * HEAD_DIM
C_QK, C_V = C_QK_HEADS * HEAD_DIM, C_V_HEADS * HEAD_DIM
C_CONV = 2 * C_QK + C_V
ATT_W = A_Q + 2 * A_KV + B_Q + 2 * B_KV
MAIN_W = C_CONV + C_V + ATT_W
AB_W = 128


def _cparams(sem):
    return pltpu.CompilerParams(dimension_semantics=sem, vmem_limit_bytes=VMEM_LIMIT)


def _pick(n, prefs):
    for p in prefs:
        if n % p == 0:
            return p
    return n


def _dot(a, b, **kw):
    return jnp.dot(a, b, preferred_element_type=F32, **kw)


def _dot_nt(a, b):
    return lax.dot_general(a, b, (((1,), (1,)), ((), ())), preferred_element_type=F32)


def _dot_tn(a, b):
    return lax.dot_general(a, b, (((0,), (0,)), ((), ())), preferred_element_type=F32)


def _silu(x):
    return x * jax.nn.sigmoid(x)


def _mod_kernel(c_ref, w_ref, b_ref, o_ref):
    o_ref[0] = _dot(_silu(c_ref[...]), w_ref[0], precision=HIGHEST) + b_ref[0]


def _modulation(c_all, w_mod, b_mod):
    depth, d, width = w_mod.shape
    tn = _pick(width, (1024, 512, 256, 128))
    return pl.pallas_call(
        _mod_kernel,
        grid=(depth, width // tn),
        in_specs=[pl.BlockSpec((8, d), lambda l, j: (0, 0)),
                  pl.BlockSpec((1, d, tn), lambda l, j: (l, 0, j)),
                  pl.BlockSpec((1, 1, tn), lambda l, j: (l, 0, j))],
        out_specs=pl.BlockSpec((1, 8, tn), lambda l, j: (l, 0, j)),
        out_shape=jax.ShapeDtypeStruct((depth, 8, width), F32),
        compiler_params=_cparams(("parallel", "parallel")),
        name="modulation",
    )(c_all, w_mod, b_mod.reshape(depth, 1, width))


class _Rows:
    def __init__(self, b, s, c):
        self.b, self.s, self.c = b, s, c
        self.tm = 256 if c % 256 == 0 else 128
        self.n_lat = b * s
        self.r = b * s + b * c
        self.lat_blocks = self.n_lat // self.tm
        self.blocks = self.r // self.tm

    def group(self, i):
        return jnp.where(i < self.lat_blocks, i // (self.s // self.tm), self.b)


def _mod_spec(rows, d, which):
    return pl.BlockSpec((1, 1, 1, d), lambda i: (rows.group(i), which, 0, 0))


def _vec_spec(d):
    return pl.BlockSpec((1, d), lambda i: (0, 0))


def _layernorm(x):
    mu = jnp.mean(x, axis=-1, keepdims=True)
    xc = x - mu
    var = jnp.mean(xc * xc, axis=-1, keepdims=True)
    return xc * lax.rsqrt(var + LN_EPS)


def _entry_kernel(x_ref, sc_ref, sh_ref, x0_ref, h_ref):
    xn = _layernorm(x_ref[...])
    x0_ref[...] = xn
    h_ref[...] = (xn * (1.0 + sc_ref[0, 0]) + sh_ref[0, 0]).astype(BF16)


def _entry(rows, x_all, mod_l):
    r, d = x_all.shape
    tm = rows.tm
    row_spec = pl.BlockSpec((tm, d), lambda i: (i, 0))
    return pl.pallas_call(
        _entry_kernel,
        grid=(r // tm,),
        in_specs=[row_spec, _mod_spec(rows, d, 1), _mod_spec(rows, d, 0)],
        out_specs=[row_spec, row_spec],
        out_shape=[jax.ShapeDtypeStruct((r, d), F32), jax.ShapeDtypeStruct((r, d), BF16)],
        compiler_params=_cparams(("parallel",)),
        name="entry_norm",
    )(x_all, mod_l, mod_l)


def _mm_kernel(a_ref, w_ref, o_ref):
    o_ref[...] = _dot(a_ref[...], w_ref[...]).astype(o_ref.dtype)


def _matmul(a, w, out_dtype):
    m, k = a.shape
    n = w.shape[1]
    tm = _pick(m, (1024, 512, 256, 128))
    tn = _pick(n, (1024, 512, 256, 128))
    return pl.pallas_call(
        _mm_kernel,
        grid=(n // tn, m // tm),
        in_specs=[pl.BlockSpec((tm, k), lambda j, i: (i, 0)),
                  pl.BlockSpec((k, tn), lambda j, i: (0, j))],
        out_specs=pl.BlockSpec((tm, tn), lambda j, i: (i, j)),
        out_shape=jax.ShapeDtypeStruct((m, n), out_dtype),
        compiler_params=_cparams(("parallel", "parallel")),
        name="in_proj",
    )(a, w)


def _rope(t, cos, sin_lo, sin_hi):
    return t * cos + pltpu.roll(t, HEAD_DIM - 32, 1) * sin_lo + pltpu.roll(t, 32, 1) * sin_hi


def _rms(t, g):
    return t * lax.rsqrt(jnp.mean(t * t, axis=-1, keepdims=True) + RMS_EPS) * g


def _attn_prep_kernel(x_ref, cos_ref, slo_ref, shi_ref, qn_ref, kn_ref, o_ref):
    cos, slo, shi = cos_ref[...], slo_ref[...], shi_ref[...]
    scale = HEAD_DIM ** -0.5

    def head(j):
        return x_ref[:, j * HEAD_DIM:(j + 1) * HEAD_DIM]

    def put(j, v):
        o_ref[:, j * HEAD_DIM:(j + 1) * HEAD_DIM] = v.astype(BF16)

    j = 0
    for _ in range(A_HEADS):
        put(j, _rope(_rms(head(j), qn_ref[...]), cos, slo, shi) * scale)
        j += 1
    for _ in range(A_KV_HEADS):
        put(j, _rope(_rms(head(j), kn_ref[...]), cos, slo, shi))
        j += 1
    for _ in range(A_KV_HEADS):
        put(j, head(j))
        j += 1
    for _ in range(B_HEADS):
        put(j, _rope(head(j), cos, slo, shi) * scale)
        j += 1
    for _ in range(B_KV_HEADS):
        put(j, _rope(head(j), cos, slo, shi))
        j += 1
    for _ in range(B_KV_HEADS):
        put(j, head(j))
        j += 1


def _attn_prep(rows, proj_main, tables, q_norm, k_norm):
    r = proj_main.shape[0]
    tm = rows.tm
    pos_blocks = rows.s // tm
    col_blk = (C_CONV + C_V) // ATT_W
    assert col_blk * ATT_W == C_CONV + C_V

    def tab_map(i):
        return (jnp.where(i < rows.lat_blocks, i % pos_blocks, pos_blocks), 0)

    tab_spec = pl.BlockSpec((tm, HEAD_DIM), tab_map)
    return pl.pallas_call(
        _attn_prep_kernel,
        grid=(r // tm,),
        in_specs=[pl.BlockSpec((tm, ATT_W), lambda i: (i, col_blk)), tab_spec, tab_spec, tab_spec,
                  _vec_spec(HEAD_DIM), _vec_spec(HEAD_DIM)],
        out_specs=pl.BlockSpec((tm, ATT_W), lambda i: (i, 0)),
        out_shape=jax.ShapeDtypeStruct((r, ATT_W), BF16),
        compiler_params=_cparams(("parallel",)),
        name="attn_prep",
    )(proj_main, *tables, q_norm, k_norm)


def _rope_tables(s, tm):
    row = jnp.repeat(jnp.arange(s // GRID_W, dtype=F32), GRID_W)
    col = jnp.tile(jnp.arange(GRID_W, dtype=F32), s // GRID_W)
    axis_dim = HEAD_DIM // 2
    inv_freq = ROPE_THETA ** (-jnp.arange(0, axis_dim, 2, dtype=F32) / axis_dim)
    ang_r = row[:, None] * inv_freq[None, :]
    ang_c = col[:, None] * inv_freq[None, :]
    ang = jnp.concatenate([ang_r, ang_r, ang_c, ang_c], axis=-1)
    cos, sin = jnp.cos(ang), jnp.sin(ang)
    lo = (jnp.arange(HEAD_DIM) % axis_dim) < (axis_dim // 2)
    sin_lo = jnp.where(lo, -sin, 0.0)
    sin_hi = jnp.where(lo, 0.0, sin)
    one, zero = jnp.ones((tm, HEAD_DIM), F32), jnp.zeros((tm, HEAD_DIM), F32)
    return (jnp.concatenate([cos, one]), jnp.concatenate([sin_lo, zero]), jnp.concatenate([sin_hi, zero]))


def _attn_full_kernel(*refs, n_src, has_sink, groups):
    if has_sink:
        sink_ref, refs = refs[0], refs[1:]
    q_ref, o_ref = refs[0], refs[1 + 2 * n_src]
    kv = refs[1:1 + 2 * n_src]
    kvh = pl.program_id(1)
    for g in range(groups):
        q = q_ref[:, g * HEAD_DIM:(g + 1) * HEAD_DIM]
        scores = [_dot_nt(q, kv[2 * j][...]) for j in range(n_src)]
        m = functools.reduce(jnp.maximum, [jnp.max(s, axis=-1, keepdims=True) for s in scores])
        if has_sink:
            sink = sink_ref[kvh * groups + g]
            m = jnp.maximum(m, sink)
        es = [jnp.exp(s - m) for s in scores]
        den = functools.reduce(jnp.add, [jnp.sum(e, axis=-1, keepdims=True) for e in es])
        if has_sink:
            den = den + jnp.exp(sink - m)
        acc = functools.reduce(jnp.add, [_dot(es[j].astype(BF16), kv[2 * j + 1][...]) for j in range(n_src)])
        o_ref[:, g * HEAD_DIM:(g + 1) * HEAD_DIM] = (acc / den).astype(o_ref.dtype)


def _attn_full(qkv, *, batch, n_q, q_base, sources, q_col, k_col, v_col, heads, kv_heads, sink=None):
    groups = heads // kv_heads
    tq = _pick(n_q, (256, 128))
    qw = groups * HEAD_DIM
    nq = n_q // tq
    in_specs = []
    args = []
    if sink is not None:
        in_specs.append(pl.BlockSpec(memory_space=pltpu.SMEM))
        args.append(sink)
    in_specs.append(pl.BlockSpec((tq, qw), lambda b, h, i: (q_base // tq + b * nq + i, q_col // qw + h)))
    args.append(qkv)
    for base, n in sources:
        for col in (k_col, v_col):
            in_specs.append(pl.BlockSpec(
                (n, HEAD_DIM), lambda b, h, i, base=base, n=n, col=col: (base // n + b, col // HEAD_DIM + h)))
            args.append(qkv)
    kern = functools.partial(_attn_full_kernel, n_src=len(sources), has_sink=sink is not None, groups=groups)
    return pl.pallas_call(
        kern,
        grid=(batch, kv_heads, nq),
        in_specs=in_specs,
        out_specs=pl.BlockSpec((tq, qw), lambda b, h, i: (b * nq + i, h)),
        out_shape=jax.ShapeDtypeStruct((batch * n_q, heads * HEAD_DIM), BF16),
        compiler_params=_cparams(("parallel", "parallel", "arbitrary")),
        name="attn_full",
    )(*args)


def _attn_win_kernel(sink_ref, q_ref, k_ref, v_ref, kc_ref, vc_ref, o_ref, *, groups, seq):
    kvh = pl.program_id(1)
    n = pl.program_id(2)
    span = 3 * Q_BLOCK
    start = jnp.clip((n - 1) * Q_BLOCK, 0, seq - span)
    start = pl.multiple_of(start, Q_BLOCK)
    kb = k_ref[pl.ds(start, span), :]
    vb = v_ref[pl.ds(start, span), :]
    q_pos = n * Q_BLOCK + lax.broadcasted_iota(jnp.int32, (Q_BLOCK, span), 0)
    k_pos = start + lax.broadcasted_iota(jnp.int32, (Q_BLOCK, span), 1)
    valid = jnp.abs(k_pos - q_pos) <= WINDOW
    for g in range(groups):
        q = q_ref[:, g * HEAD_DIM:(g + 1) * HEAD_DIM]
        s_win = jnp.where(valid, _dot_nt(q, kb), -jnp.inf)
        s_ctx = _dot_nt(q, kc_ref[...])
        sink = sink_ref[kvh * groups + g]
        m = jnp.maximum(jnp.maximum(jnp.max(s_win, axis=-1, keepdims=True),
                                    jnp.max(s_ctx, axis=-1, keepdims=True)), sink)
        e_win = jnp.exp(s_win - m)
        e_ctx = jnp.exp(s_ctx - m)
        den = (jnp.sum(e_win, axis=-1, keepdims=True) + jnp.sum(e_ctx, axis=-1, keepdims=True)
               + jnp.exp(sink - m))
        acc = _dot(e_win.astype(BF16), vb) + _dot(e_ctx.astype(BF16), vc_ref[...])
        o_ref[:, g * HEAD_DIM:(g + 1) * HEAD_DIM] = (acc / den).astype(o_ref.dtype)


def _attn_window(qkv, sink, rows):
    b, s, c = rows.b, rows.s, rows.c
    groups = B_HEADS // B_KV_HEADS
    qw = groups * HEAD_DIM
    nq = s // Q_BLOCK
    q_col = A_Q + 2 * A_KV
    k_col = q_col + B_Q
    v_col = k_col + B_KV
    ctx_base = rows.n_lat // c

    def kv_spec(n, base, col):
        return pl.BlockSpec((n, HEAD_DIM), lambda bb, h, i: (base + bb, col // HEAD_DIM + h))

    return pl.pallas_call(
        functools.partial(_attn_win_kernel, groups=groups, seq=s),
        grid=(b, B_KV_HEADS, nq),
        in_specs=[pl.BlockSpec(memory_space=pltpu.SMEM),
                  pl.BlockSpec((Q_BLOCK, qw), lambda bb, h, i: (bb * nq + i, q_col // qw + h)),
                  kv_spec(s, 0, k_col), kv_spec(s, 0, v_col),
                  kv_spec(c, ctx_base, k_col), kv_spec(c, ctx_base, v_col)],
        out_specs=pl.BlockSpec((Q_BLOCK, qw), lambda bb, h, i: (bb * nq + i, h)),
        out_shape=jax.ShapeDtypeStruct((b * s, B_Q), BF16),
        compiler_params=_cparams(("parallel", "parallel", "arbitrary")),
        name="attn_window",
    )(sink, qkv, qkv, qkv, qkv, qkv)


def _delta_prep_kernel(prev_ref, cur_ref, next_ref, ab_ref, w_ref, alog_ref, dtb_ref, qkv_ref, gb_ref,
                       *, lat_blocks, seq_blocks, ctx_blocks):
    i = pl.program_id(0)
    tm = cur_ref.shape[0]
    is_lat = i < lat_blocks
    pos = jnp.where(is_lat, i % seq_blocks, (i - lat_blocks) % ctx_blocks)
    n_pos = jnp.where(is_lat, seq_blocks, ctx_blocks)
    first = pos == 0
    last = pos == n_pos - 1
    keep_prev = jnp.where(first, 0.0, 1.0)
    keep_next = jnp.where(last, 0.0, 1.0)
    ext = tm + 2 * HALO
    n_qk = 2 * C_QK_HEADS
    for j in range(C_CONV // HEAD_DIM):
        sl = slice(j * HEAD_DIM, (j + 1) * HEAD_DIM)
        xt = jnp.concatenate([prev_ref[:, sl] * keep_prev, cur_ref[:, sl], next_ref[:, sl] * keep_next], axis=0)
        acc = None
        for tap in range(CONV_K):
            shift = (CONV_K // 2 - tap) % ext
            xs = xt if shift == 0 else pltpu.roll(xt, shift, 0)
            term = xs[HALO:HALO + tm] * w_ref[tap:tap + 1, sl]
            acc = term if acc is None else acc + term
        y = _silu(acc)
        if j < n_qk:
            y = y * lax.rsqrt(jnp.sum(y * y, axis=-1, keepdims=True) + RMS_EPS)
            if j < C_QK_HEADS:
                y = y * (HEAD_DIM ** -0.5)
        qkv_ref[:, sl] = y
    ab = ab_ref[...]
    lane = lax.broadcasted_iota(jnp.int32, ab.shape, 1)
    n_g = 2 * C_V_HEADS
    z = ab + dtb_ref[...]
    softplus = jnp.maximum(z, 0.0) + jnp.log(1.0 + jnp.exp(-jnp.abs(z)))
    g = -jnp.exp(alog_ref[...]) * softplus
    gb_ref[...] = jnp.where(lane < n_g, g, jax.nn.sigmoid(ab))


def _delta_prep(rows, proj_main, proj_ab, conv_w, a_log, dt_bias):
    r = proj_main.shape[0]
    tm = rows.tm
    hb = tm // HALO
    last_halo = r // HALO - 1
    pad = AB_W - a_log.size
    alog = jnp.pad(a_log.reshape(1, -1), ((0, 0), (0, pad)))
    dtb = jnp.pad(dt_bias.reshape(1, -1), ((0, 0), (0, pad)))
    kern = functools.partial(_delta_prep_kernel, lat_blocks=rows.lat_blocks, seq_blocks=rows.s // tm,
                             ctx_blocks=rows.c // tm)
    return pl.pallas_call(
        kern,
        grid=(r // tm,),
        in_specs=[pl.BlockSpec((HALO, C_CONV), lambda i: (jnp.maximum(i * hb - 1, 0), 0)),
                  pl.BlockSpec((tm, C_CONV), lambda i: (i, 0)),
                  pl.BlockSpec((HALO, C_CONV), lambda i: (jnp.minimum((i + 1) * hb, last_halo), 0)),
                  pl.BlockSpec((tm, AB_W), lambda i: (i, 0)),
                  pl.BlockSpec((CONV_K, C_CONV), lambda i: (0, 0)),
                  _vec_spec(AB_W), _vec_spec(AB_W)],
        out_specs=[pl.BlockSpec((tm, C_CONV), lambda i: (i, 0)), pl.BlockSpec((tm, AB_W), lambda i: (i, 0))],
        out_shape=[jax.ShapeDtypeStruct((r, C_CONV), F32), jax.ShapeDtypeStruct((r, AB_W), F32)],
        compiler_params=_cparams(("parallel",)),
        name="delta_prep",
    )(proj_main, proj_main, proj_main, proj_ab, conv_w, alog, dtb)


def _delta_kernel(qkv_ref, gb_ref, o_ref, state_ref, *, direction):
    @pl.when(pl.program_id(1) == 0)
    def _():
        state_ref[...] = jnp.zeros_like(state_ref)

    row = lax.broadcasted_iota(jnp.int32, (CHUNK, CHUNK), 0)
    col = lax.broadcasted_iota(jnp.int32, (CHUNK, CHUNK), 1)
    if direction == 0:
        incl, strict, last = row >= col, row > col, CHUNK - 1
    else:
        incl, strict, last = row <= col, row < col, 0
    eye = jnp.where(row == col, 1.0, 0.0)
    gb = gb_ref[...]
    gcum = _dot(jnp.where(incl, 1.0, 0.0), gb, precision=HIGHEST)
    gcum_t = gcum.T
    rep = C_V_HEADS // C_QK_HEADS
    for h in range(C_V_HEADS):
        lane = direction * C_V_HEADS + h
        g_col = gcum[:, lane:lane + 1]
        g_row = gcum_t[lane:lane + 1, :]
        g_last = gcum[last:last + 1, lane:lane + 1]
        beta = gb[:, 2 * C_V_HEADS + lane:2 * C_V_HEADS + lane + 1]
        decay = jnp.where(incl, jnp.exp(jnp.minimum(g_col - g_row, 0.0)), 0.0)
        qh = h // rep
        q = qkv_ref[:, qh * HEAD_DIM:(qh + 1) * HEAD_DIM]
        k = qkv_ref[:, C_QK + qh * HEAD_DIM:C_QK + (qh + 1) * HEAD_DIM]
        v = qkv_ref[:, 2 * C_QK + h * HEAD_DIM:2 * C_QK + (h + 1) * HEAD_DIM]
        kb = k * beta
        e_g = jnp.exp(g_col)
        kq = _dot_nt(jnp.concatenate([kb, q], axis=0).astype(BF16), k.astype(BF16))
        a_mat = jnp.where(strict, kq[:CHUNK] * decay, 0.0)
        qk = kq[CHUNK:] * decay
        p = -a_mat
        t_inv = eye + p
        for _ in range(int(math.log2(CHUNK)) - 1):
            pb = p.astype(BF16)
            p = _dot(pb, pb)
            t_inv = t_inv + _dot(t_inv.astype(BF16), p.astype(BF16))
        uw = _dot(t_inv.astype(BF16), jnp.concatenate([v * beta, kb * e_g], axis=1).astype(BF16))
        u, w = uw[:, :HEAD_DIM], uw[:, HEAD_DIM:]
        state = state_ref[h]
        ws_qs = _dot(jnp.concatenate([w, q * e_g], axis=0).astype(BF16), state.astype(BF16))
        v_new = u - ws_qs[:CHUNK]
        o_ref[:, h * HEAD_DIM:(h + 1) * HEAD_DIM] = ws_qs[CHUNK:] + _dot(qk.astype(BF16), v_new.astype(BF16))
        k_dec = k * jnp.exp(g_last - g_col)
        state_ref[h] = state * jnp.exp(g_last) + _dot_tn(k_dec.astype(BF16), v_new.astype(BF16))


def _delta_scan(rows, dqkv, gb, direction):
    b, s, c = rows.b, rows.s, rows.c
    ctx_chunks, lat_chunks = c // CHUNK, s // CHUNK
    ctx_base = rows.n_lat // CHUNK

    def chunk_map(bb, t):
        if direction == 0:
            ctx_i, lat_i = t, t - ctx_chunks
        else:
            ctx_i, lat_i = ctx_chunks - 1 - t, lat_chunks - 1 - (t - ctx_chunks)
        return (jnp.where(t < ctx_chunks, ctx_base + bb * ctx_chunks + ctx_i, bb * lat_chunks + lat_i), 0)

    return pl.pallas_call(
        functools.partial(_delta_kernel, direction=direction),
        grid=(b, ctx_chunks + lat_chunks),
        in_specs=[pl.BlockSpec((CHUNK, C_CONV), chunk_map), pl.BlockSpec((CHUNK, AB_W), chunk_map)],
        out_specs=pl.BlockSpec((CHUNK, C_V), chunk_map),
        out_shape=jax.ShapeDtypeStruct((rows.r, C_V), F32),
        scratch_shapes=[pltpu.VMEM((C_V_HEADS, HEAD_DIM, HEAD_DIM), F32)],
        compiler_params=_cparams(("parallel", "arbitrary")),
        name=f"delta_scan_{direction}",
    )(dqkv, gb)


def _merge_kernel(oa_ref, ob_ref, of_ref, or_ref, z_ref, nc_ref, ga_ref, gb_ref, gc_ref,
                  wa_ref, wb_ref, wc_ref, y_ref, oc_ref):
    for h in range(C_V_HEADS):
        sl = slice(h * HEAD_DIM, (h + 1) * HEAD_DIM)
        o = of_ref[:, sl] + or_ref[:, sl]
        oc_ref[:, sl] = (_rms(o, nc_ref[...]) * _silu(z_ref[:, sl])).astype(BF16)
    y = jax.nn.sigmoid(ga_ref[...]) * _dot(oa_ref[...], wa_ref[...])
    y = y + jax.nn.sigmoid(gb_ref[...]) * _dot(ob_ref[...], wb_ref[...])
    y = y + jax.nn.sigmoid(gc_ref[...]) * _dot(oc_ref[...], wc_ref[...])
    y_ref[...] = y.astype(BF16)


def _merge(n_rows, tm, o_a, o_b, o_f, o_r, proj_main, gates, norm_c, w_a, w_b, w_c):
    d = w_a.shape[1]
    z_blk = C_CONV // C_V

    def rspec(w, col=0):
        return pl.BlockSpec((tm, w), lambda i: (i, col))

    def wspec(kdim):
        return pl.BlockSpec((kdim, d), lambda i: (0, 0))

    return pl.pallas_call(
        _merge_kernel,
        grid=(n_rows // tm,),
        in_specs=[rspec(A_Q), rspec(B_Q), rspec(C_V), rspec(C_V), rspec(C_V, z_blk), _vec_spec(HEAD_DIM),
                  rspec(d, 0), rspec(d, 1), rspec(d, 2), wspec(A_Q), wspec(B_Q), wspec(C_V)],
        out_specs=rspec(d),
        out_shape=jax.ShapeDtypeStruct((n_rows, d), BF16),
        scratch_shapes=[pltpu.VMEM((tm, C_V), BF16)],
        compiler_params=_cparams(("parallel",)),
        name="branch_merge",
    )(o_a, o_b, o_f, o_r, proj_main, norm_c, gates, gates, gates, w_a, w_b, w_c)


def _outproj_kernel(y_ref, w_ref, x_ref, gt_ref, g_ref, b_ref, sc_ref, sh_ref, x1_ref, h_ref, *, alpha):
    y = _dot(y_ref[...], w_ref[...])
    x1 = _layernorm(alpha * x_ref[...] + gt_ref[0, 0] * y) * g_ref[...] + b_ref[...]
    x1_ref[...] = x1
    h_ref[...] = (x1 * (1.0 + sc_ref[0, 0]) + sh_ref[0, 0]).astype(BF16)


def _outproj(rows, n_rows, y, w_out, x, mod_l, ln_g, ln_b, alpha):
    d = x.shape[1]
    tm = rows.tm
    rspec = pl.BlockSpec((tm, d), lambda i: (i, 0))
    return pl.pallas_call(
        functools.partial(_outproj_kernel, alpha=alpha),
        grid=(n_rows // tm,),
        in_specs=[rspec, pl.BlockSpec((d, d), lambda i: (0, 0)), rspec, _mod_spec(rows, d, 2),
                  _vec_spec(d), _vec_spec(d), _mod_spec(rows, d, 4), _mod_spec(rows, d, 3)],
        out_specs=[rspec, rspec],
        out_shape=[jax.ShapeDtypeStruct((n_rows, d), F32), jax.ShapeDtypeStruct((n_rows, d), BF16)],
        compiler_params=_cparams(("parallel",)),
        name="out_proj_norm",
    )(y, w_out, x, mod_l, ln_g, ln_b, mod_l, mod_l)


def _router_kernel(x_ref, sc_ref, sh_ref, wr_ref, rb_ref, idx_ref, gate_ref, rank_ref, cnt_ref, carry_ref):
    @pl.when(pl.program_id(0) == 0)
    def _():
        carry_ref[...] = jnp.zeros_like(carry_ref)

    h = x_ref[...] * (1.0 + sc_ref[0, 0]) + sh_ref[0, 0]
    scores = jax.nn.sigmoid(_dot(h, wr_ref[...], precision=HIGHEST))
    tm, n_exp = scores.shape
    sel = scores + rb_ref[...]
    lane = lax.broadcasted_iota(jnp.int32, (tm, n_exp), 1)
    slot = lax.broadcasted_iota(jnp.int32, (tm, TOP_K), 1)
    idx = jnp.zeros((tm, TOP_K), jnp.int32)
    gate = jnp.zeros((tm, TOP_K), F32)
    mask = jnp.zeros((tm, n_exp), F32)
    hits = []
    for k in range(TOP_K):
        best = jnp.max(sel, axis=-1, keepdims=True)
        choice = jnp.min(jnp.where(sel == best, lane, n_exp), axis=-1, keepdims=True)
        hit = lane == choice
        hits.append(hit)
        idx = jnp.where(slot == k, choice, idx)
        gate = jnp.where(slot == k, jnp.sum(jnp.where(hit, scores, 0.0), axis=-1, keepdims=True), gate)
        sel = jnp.where(hit, -jnp.inf, sel)
        mask = jnp.where(hit, 1.0, mask)
    idx_ref[...] = idx
    gate_ref[...] = gate / jnp.sum(gate, axis=-1, keepdims=True) * ROUTED_SCALE
    r_i = lax.broadcasted_iota(jnp.int32, (tm, tm), 0)
    c_i = lax.broadcasted_iota(jnp.int32, (tm, tm), 1)
    before = jnp.where(c_i < r_i, 1.0, 0.0).astype(BF16)
    prefix = _dot(before, mask.astype(BF16)) + carry_ref[...]
    rank = jnp.zeros((tm, TOP_K), F32)
    for k in range(TOP_K):
        rank = jnp.where(slot == k, jnp.sum(jnp.where(hits[k], prefix, 0.0), axis=-1, keepdims=True), rank)
    rank_ref[...] = rank.astype(jnp.int32)
    carry_ref[...] = carry_ref[...] + jnp.sum(mask, axis=0, keepdims=True)
    cnt_ref[...] = carry_ref[...]


def _router(rows, n_rows, x1, mod_l, w_router, router_bias):
    d, n_exp = w_router.shape
    tm = rows.tm
    kspec = pl.BlockSpec((tm, TOP_K), lambda i: (i, 0))
    return pl.pallas_call(
        _router_kernel,
        grid=(n_rows // tm,),
        in_specs=[pl.BlockSpec((tm, d), lambda i: (i, 0)), _mod_spec(rows, d, 4), _mod_spec(rows, d, 3),
                  pl.BlockSpec((d, n_exp), lambda i: (0, 0)), _vec_spec(n_exp)],
        out_specs=[kspec, kspec, kspec, _vec_spec(n_exp)],
        out_shape=[jax.ShapeDtypeStruct((n_rows, TOP_K), jnp.int32), jax.ShapeDtypeStruct((n_rows, TOP_K), F32),
                   jax.ShapeDtypeStruct((n_rows, TOP_K), jnp.int32), jax.ShapeDtypeStruct((1, n_exp), F32)],
        scratch_shapes=[pltpu.VMEM((1, n_exp), F32)],
        compiler_params=_cparams(("arbitrary",)),
        name="router",
    )(x1, mod_l, mod_l, w_router, router_bias.reshape(1, n_exp))


def _expert_kernel(be_ref, nu_ref, x_ref, w1_ref, w3_ref, w2_ref, o_ref, w1b, w3b, w2b, prev_ref):
    blk = pl.program_id(0)
    e = be_ref[blk]

    @pl.when(blk == 0)
    def _():
        prev_ref[0] = -1

    @pl.when(blk < nu_ref[0])
    def _():
        @pl.when(e != prev_ref[0])
        def _():
            w1b[...] = w1_ref[0, 0].astype(BF16)
            w3b[...] = w3_ref[0, 0].astype(BF16)
            w2b[...] = w2_ref[0, 0].astype(BF16)
            prev_ref[0] = e

        x = x_ref[...]
        hidden = _silu(_dot(x, w1b[...])) * _dot(x, w3b[...])
        o_ref[...] = _dot(hidden.astype(BF16), w2b[...]).astype(o_ref.dtype)

    @pl.when(blk >= nu_ref[0])
    def _():
        o_ref[...] = jnp.zeros_like(o_ref)


def _experts(xs, block_expert, n_used, w1, w3, w2, layer):
    n_slots, d = xs.shape
    ff = w1.shape[-1]
    n_blocks = n_slots // MOE_ROWS
    grid_spec = pltpu.PrefetchScalarGridSpec(
        num_scalar_prefetch=2,
        grid=(n_blocks,),
        in_specs=[pl.BlockSpec((MOE_ROWS, d), lambda i, be, nu: (i, 0)),
                  pl.BlockSpec((1, 1, d, ff), lambda i, be, nu: (layer, be[i], 0, 0)),
                  pl.BlockSpec((1, 1, d, ff), lambda i, be, nu: (layer, be[i], 0, 0)),
                  pl.BlockSpec((1, 1, ff, d), lambda i, be, nu: (layer, be[i], 0, 0))],
        out_specs=pl.BlockSpec((MOE_ROWS, d), lambda i, be, nu: (i, 0)),
        scratch_shapes=[pltpu.VMEM((d, ff), BF16), pltpu.VMEM((d, ff), BF16), pltpu.VMEM((ff, d), BF16),
                        pltpu.SMEM((1,), jnp.int32)],
    )
    return pl.pallas_call(
        _expert_kernel,
        grid_spec=grid_spec,
        out_shape=jax.ShapeDtypeStruct((n_slots, d), BF16),
        compiler_params=_cparams(("arbitrary",)),
        name="experts",
    )(block_expert, n_used, xs, w1, w3, w2)


def _combine_kernel(*refs, alpha, d, emit_h):
    (ys_ref, gate_ref, h_ref, ws1_ref, ws3_ref, ws2_ref, x_ref, gt_ref, g_ref, b_ref) = refs[:10]
    hidden = _silu(_dot(h_ref[...], ws1_ref[...])) * _dot(h_ref[...], ws3_ref[...])
    f = _dot(hidden.astype(BF16), ws2_ref[...])
    gate = gate_ref[...]
    for k in range(TOP_K):
        f = f + ys_ref[:, k * d:(k + 1) * d].astype(F32) * gate[:, k:k + 1]
    x2 = _layernorm(alpha * x_ref[...] + gt_ref[0, 0] * f) * g_ref[...] + b_ref[...]
    if emit_h:
        sc_ref, sh_ref, x2_ref, hn_ref = refs[10:]
        x2_ref[...] = x2
        hn_ref[...] = (x2 * (1.0 + sc_ref[0, 0]) + sh_ref[0, 0]).astype(BF16)
    else:
        refs[10][...] = x2


def _combine(rows, n_rows, ys_g, gate, h2, ws1, ws3, ws2, x1, mod_l, ln_g, ln_b, alpha, mod_next):
    d = x1.shape[1]
    sf = ws1.shape[1]
    tm = rows.tm
    rspec = pl.BlockSpec((tm, d), lambda i: (i, 0))
    in_specs = [pl.BlockSpec((tm, TOP_K * d), lambda i: (i, 0)), pl.BlockSpec((tm, TOP_K), lambda i: (i, 0)), rspec,
                pl.BlockSpec((d, sf), lambda i: (0, 0)), pl.BlockSpec((d, sf), lambda i: (0, 0)),
                pl.BlockSpec((sf, d), lambda i: (0, 0)), rspec, _mod_spec(rows, d, 5), _vec_spec(d), _vec_spec(d)]
    args = [ys_g, gate, h2, ws1, ws3, ws2, x1, mod_l, ln_g, ln_b]
    emit_h = mod_next is not None
    if emit_h:
        in_specs += [_mod_spec(rows, d, 1), _mod_spec(rows, d, 0)]
        args += [mod_next, mod_next]
        out_specs = [rspec, rspec]
        out_shape = [jax.ShapeDtypeStruct((n_rows, d), F32), jax.ShapeDtypeStruct((n_rows, d), BF16)]
    else:
        out_specs = rspec
        out_shape = jax.ShapeDtypeStruct((n_rows, d), F32)
    return pl.pallas_call(
        functools.partial(_combine_kernel, alpha=alpha, d=d, emit_h=emit_h),
        grid=(n_rows // tm,),
        in_specs=in_specs,
        out_specs=out_specs,
        out_shape=out_shape,
        compiler_params=_cparams(("parallel",)),
        name="moe_combine_norm",
    )(*args)


def _permute_w_in(w):
    a_end = A_Q + 2 * A_KV
    b_end = a_end + B_Q + 2 * B_KV
    c_end = b_end + C_CONV + C_V
    ab_end = c_end + 4 * C_V_HEADS
    main = jnp.concatenate([w[:, b_end:c_end], w[:, :b_end]], axis=1).astype(BF16)
    ab = jnp.pad(w[:, c_end:ab_end], ((0, 0), (0, AB_W - 4 * C_V_HEADS))).astype(BF16)
    gates = w[:, ab_end:].astype(BF16)
    return main, gates, ab


def kernel(x, c, ctx, c_ctx, w_mod, b_mod, w_in, q_norm_a, k_norm_a, sink_b, conv_c, a_log_c, dt_bias_c, norm_c,
           w_br_a, w_br_b, w_br_c, w_out, ln1_g, ln1_b, w_router, router_bias, w1, w3, w2, ws1, ws3, ws2,
           ln2_g, ln2_b):
    b, s, d = x.shape
    n_ctx = ctx.shape[1]
    depth = w_mod.shape[0]
    n_exp = w_router.shape[-1]
    alpha = (2.0 * depth) ** 0.25
    rows = _Rows(b, s, n_ctx)
    tm = rows.tm

    c_all = jnp.concatenate([c, c_ctx[None, :], jnp.zeros((8 - b - 1, d), F32)], axis=0)
    mod = _modulation(c_all, w_mod, b_mod).reshape(depth, 8, N_MOD, 1, d)
    tables = _rope_tables(s, tm)

    x_all = jnp.concatenate([x.reshape(b * s, d), ctx.reshape(b * n_ctx, d)], axis=0)
    x_cur, h = _entry(rows, x_all, mod[0])

    for l in range(depth):
        need_ctx = l < depth - 1
        n_rows = rows.r if need_ctx else rows.n_lat
        mod_l = mod[l]
        w_main, w_gates, w_ab = _permute_w_in(w_in[l])
        proj_main = _matmul(h, w_main, F32)
        gates = _matmul(h, w_gates, F32)
        proj_ab = _matmul(h, w_ab, F32)

        qkv = _attn_prep(rows, proj_main, tables, q_norm_a[l].reshape(1, -1), k_norm_a[l].reshape(1, -1))
        a_k, a_v = A_Q, A_Q + A_KV
        o_a = _attn_full(qkv, batch=b, n_q=s, q_base=0, sources=[(rows.n_lat, n_ctx), (0, s)],
                         q_col=0, k_col=a_k, v_col=a_v, heads=A_HEADS, kv_heads=A_KV_HEADS)
        o_b = _attn_window(qkv, sink_b[l], rows)
        if need_ctx:
            b_q = A_Q + 2 * A_KV
            o_a_c = _attn_full(qkv, batch=b, n_q=n_ctx, q_base=rows.n_lat, sources=[(rows.n_lat, n_ctx)],
                               q_col=0, k_col=a_k, v_col=a_v, heads=A_HEADS, kv_heads=A_KV_HEADS)
            o_b_c = _attn_full(qkv, batch=b, n_q=n_ctx, q_base=rows.n_lat, sources=[(rows.n_lat, n_ctx)],
                               q_col=b_q, k_col=b_q + B_Q, v_col=b_q + B_Q + B_KV, heads=B_HEADS,
                               kv_heads=B_KV_HEADS, sink=sink_b[l])
            o_a = jnp.concatenate([o_a, o_a_c], axis=0)
            o_b = jnp.concatenate([o_b, o_b_c], axis=0)

        dqkv, gbeta = _delta_prep(rows, proj_main, proj_ab, conv_c[l], a_log_c[l], dt_bias_c[l])
        o_f = _delta_scan(rows, dqkv, gbeta, 0)
        o_r = _delta_scan(rows, dqkv, gbeta, 1)

        y = _merge(n_rows, tm, o_a, o_b, o_f, o_r, proj_main, gates, norm_c[l].reshape(1, -1),
                   w_br_a[l].astype(BF16), w_br_b[l].astype(BF16), w_br_c[l].astype(BF16))
        x1, h2 = _outproj(rows, n_rows, y, w_out[l].astype(BF16), x_cur, mod_l,
                          ln1_g[l].reshape(1, -1), ln1_b[l].reshape(1, -1), alpha)

        idx, gate, rank, counts = _router(rows, n_rows, x1, mod_l, w_router[l], router_bias[l])
        counts = counts[0].astype(jnp.int32)
        padded = (counts + MOE_ROWS - 1) // MOE_ROWS * MOE_ROWS
        padded_end = jnp.cumsum(padded)
        dest = (padded_end - padded)[idx] + rank
        n_blocks = -(-(n_rows * TOP_K) // MOE_ROWS) + n_exp
        block_expert = jnp.minimum(
            jnp.searchsorted(padded_end, jnp.arange(n_blocks, dtype=jnp.int32) * MOE_ROWS, side="right"),
            n_exp - 1).astype(jnp.int32)
        n_used = (padded_end[-1:] // MOE_ROWS).astype(jnp.int32)
        tok = jnp.broadcast_to(jnp.arange(n_rows, dtype=jnp.int32)[:, None], dest.shape)
        slot_tok = jnp.zeros((n_blocks * MOE_ROWS,), jnp.int32).at[dest.reshape(-1)].set(tok.reshape(-1))
        xs = jnp.take(h2, slot_tok, axis=0)
        ys = _experts(xs, block_expert, n_used, w1, w3, w2, l)
        ys_g = jnp.take(ys, dest.reshape(-1), axis=0).reshape(n_rows, TOP_K * d)

        mod_next = mod[l + 1] if need_ctx else None
        out = _combine(rows, n_rows, ys_g, gate, h2, ws1[l].astype(BF16), ws3[l].astype(BF16), ws2[l].astype(BF16),
                       x1, mod_l, ln2_g[l].reshape(1, -1), ln2_b[l].reshape(1, -1), alpha, mod_next)
        if need_ctx:
            x_cur, h = out
        else:
            x_cur = out
    return x_cur[:rows.n_lat].reshape(b, s, d)
```

```python
import functools
import math

import jax
import jax.numpy as jnp
from jax import lax
from jax.experimental import pallas as pl
from jax.experimental.pallas import tpu as pltpu

F32 = jnp.float32
BF16 = jnp.bfloat16
HIGHEST = lax.Precision.HIGHEST

HEAD_DIM = 128
GRID_W = 64
ROPE_THETA = 10000.0
A_HEADS, A_KV_HEADS = 8, 2
B_HEADS, B_KV_HEADS = 8, 2
WINDOW = 128
Q_BLOCK = 128
C_QK_HEADS, C_V_HEADS = 4, 8
CONV_K = 5
CHUNK = 64
TOP_K = 8
ROUTED_SCALE = 2.5
LN_EPS = 1e-5
RMS_EPS = 1e-6
MOE_ROWS = 256
N_MOD = 6
HALO = 8
VMEM_LIMIT = 56 * 2**20

A_Q, A_KV = A_HEADS * HEAD_DIM, A_KV_HEADS * HEAD_DIM
B_Q, B_KV = B_HEADS * HEAD_DIM, B_KV_HEADS * HEAD_DIM
C_QK, C_V = C_QK_HEADS * HEAD_DIM, C_V_HEADS * HEAD_DIM
C_CONV = 2 * C_QK + C_V
ATT_W = A_Q + 2 * A_KV + B_Q + 2 * B_KV
MAIN_W = C_CONV + C_V + ATT_W
AB_W = 128


def _cparams(sem):
    return pltpu.CompilerParams(dimension_semantics=sem, vmem_limit_bytes=VMEM_LIMIT)


def _pick(n, prefs):
    for p in prefs:
        if n % p == 0:
            return p
    return n


def _dot(a, b, **kw):
    return jnp.dot(a, b, preferred_element_type=F32, **kw)


def _dot_nt(a, b):
    return lax.dot_general(a, b, (((1,), (1,)), ((), ())), preferred_element_type=F32)


def _dot_tn(a, b):
    return lax.dot_general(a, b, (((0,), (0,)), ((), ())), preferred_element_type=F32)


def _silu(x):
    return x * jax.nn.sigmoid(x)


def _mod_kernel(c_ref, w_ref, b_ref, o_ref):
    o_ref[0] = _dot(_silu(c_ref[...]), w_ref[0], precision=HIGHEST) + b_ref[0]


def _modulation(c_all, w_mod, b_mod):
    depth, d, width = w_mod.shape
    tn = _pick(width, (1024, 512, 256, 128))
    return pl.pallas_call(
        _mod_kernel,
        grid=(depth, width // tn),
        in_specs=[pl.BlockSpec((8, d), lambda l, j: (0, 0)),
                  pl.BlockSpec((1, d, tn), lambda l, j: (l, 0, j)),
                  pl.BlockSpec((1, 1, tn), lambda l, j: (l, 0, j))],
        out_specs=pl.BlockSpec((1, 8, tn), lambda l, j: (l, 0, j)),
        out_shape=jax.ShapeDtypeStruct((depth, 8, width), F32),
        compiler_params=_cparams(("parallel", "parallel")),
        name="modulation",
    )(c_all, w_mod, b_mod.reshape(depth, 1, width))


class _Rows:
    def __init__(self, b, s, c):
        self.b, self.s, self.c = b, s, c
        self.tm = 256 if c % 256 == 0 else 128
        self.n_lat = b * s
        self.r = b * s + b * c
        self.lat_blocks = self.n_lat // self.tm
        self.blocks = self.r // self.tm

    def group(self, i):
        return jnp.where(i < self.lat_blocks, i // (self.s // self.tm), self.b)


def _mod_spec(rows, d, which):
    return pl.BlockSpec((1, 1, 1, d), lambda i: (rows.group(i), which, 0, 0))


def _vec_spec(d):
    return pl.BlockSpec((1, d), lambda i: (0, 0))


def _layernorm(x):
    mu = jnp.mean(x, axis=-1, keepdims=True)
    xc = x - mu
    var = jnp.mean(xc * xc, axis=-1, keepdims=True)
    return xc * lax.rsqrt(var + LN_EPS)


def _entry_kernel(x_ref, sc_ref, sh_ref, x0_ref, h_ref):
    xn = _layernorm(x_ref[...])
    x0_ref[...] = xn
    h_ref[...] = (xn * (1.0 + sc_ref[0, 0]) + sh_ref[0, 0]).astype(BF16)


def _entry(rows, x_all, mod_l):
    r, d = x_all.shape
    tm = rows.tm
    row_spec = pl.BlockSpec((tm, d), lambda i: (i, 0))
    return pl.pallas_call(
        _entry_kernel,
        grid=(r // tm,),
        in_specs=[row_spec, _mod_spec(rows, d, 1), _mod_spec(rows, d, 0)],
        out_specs=[row_spec, row_spec],
        out_shape=[jax.ShapeDtypeStruct((r, d), F32), jax.ShapeDtypeStruct((r, d), BF16)],
        compiler_params=_cparams(("parallel",)),
        name="entry_norm",
    )(x_all, mod_l, mod_l)


def _mm_kernel(a_ref, w_ref, o_ref):
    o_ref[...] = _dot(a_ref[...], w_ref[...]).astype(o_ref.dtype)


def _matmul(a, w, out_dtype):
    m, k = a.shape
    n = w.shape[1]
    tm = _pick(m, (1024, 512, 256, 128))
    tn = _pick(n, (1024, 512, 256, 128))
    return pl.pallas_call(
        _mm_kernel,
        grid=(n // tn, m // tm),
        in_specs=[pl.BlockSpec((tm, k), lambda j, i: (i, 0)),
                  pl.BlockSpec((k, tn), lambda j, i: (0, j))],
        out_specs=pl.BlockSpec((tm, tn), lambda j, i: (i, j)),
        out_shape=jax.ShapeDtypeStruct((m, n), out_dtype),
        compiler_params=_cparams(("parallel", "parallel")),
        name="in_proj",
    )(a, w)


def _rope(t, cos, sin_lo, sin_hi):
    return t * cos + pltpu.roll(t, HEAD_DIM - 32, 1) * sin_lo + pltpu.roll(t, 32, 1) * sin_hi


def _rms(t, g):
    return t * lax.rsqrt(jnp.mean(t * t, axis=-1, keepdims=True) + RMS_EPS) * g


def _attn_prep_kernel(x_ref, cos_ref, slo_ref, shi_ref, qn_ref, kn_ref, o_ref):
    cos, slo, shi = cos_ref[...], slo_ref[...], shi_ref[...]
    scale = HEAD_DIM ** -0.5

    def head(j):
        return x_ref[:, j * HEAD_DIM:(j + 1) * HEAD_DIM]

    def put(j, v):
        o_ref[:, j * HEAD_DIM:(j + 1) * HEAD_DIM] = v.astype(BF16)

    j = 0
    for _ in range(A_HEADS):
        put(j, _rope(_rms(head(j), qn_ref[...]), cos, slo, shi) * scale)
        j += 1
    for _ in range(A_KV_HEADS):
        put(j, _rope(_rms(head(j), kn_ref[...]), cos, slo, shi))
        j += 1
    for _ in range(A_KV_HEADS):
        put(j, head(j))
        j += 1
    for _ in range(B_HEADS):
        put(j, _rope(head(j), cos, slo, shi) * scale)
        j += 1
    for _ in range(B_KV_HEADS):
        put(j, _rope(head(j), cos, slo, shi))
        j += 1
    for _ in range(B_KV_HEADS):
        put(j, head(j))
        j += 1


def _attn_prep(rows, proj_main, tables, q_norm, k_norm):
    r = proj_main.shape[0]
    tm = rows.tm
    pos_blocks = rows.s // tm
    col_blk = (C_CONV + C_V) // ATT_W
    assert col_blk * ATT_W == C_CONV + C_V

    def tab_map(i):
        return (jnp.where(i < rows.lat_blocks, i % pos_blocks, pos_blocks), 0)

    tab_spec = pl.BlockSpec((tm, HEAD_DIM), tab_map)
    return pl.pallas_call(
        _attn_prep_kernel,
        grid=(r // tm,),
        in_specs=[pl.BlockSpec((tm, ATT_W), lambda i: (i, col_blk)), tab_spec, tab_spec, tab_spec,
                  _vec_spec(HEAD_DIM), _vec_spec(HEAD_DIM)],
        out_specs=pl.BlockSpec((tm, ATT_W), lambda i: (i, 0)),
        out_shape=jax.ShapeDtypeStruct((r, ATT_W), BF16),
        compiler_params=_cparams(("parallel",)),
        name="attn_prep",
    )(proj_main, *tables, q_norm, k_norm)


def _rope_tables(s, tm):
    row = jnp.repeat(jnp.arange(s // GRID_W, dtype=F32), GRID_W)
    col = jnp.tile(jnp.arange(GRID_W, dtype=F32), s // GRID_W)
    axis_dim = HEAD_DIM // 2
    inv_freq = ROPE_THETA ** (-jnp.arange(0, axis_dim, 2, dtype=F32) / axis_dim)
    ang_r = row[:, None] * inv_freq[None, :]
    ang_c = col[:, None] * inv_freq[None, :]
    ang = jnp.concatenate([ang_r, ang_r, ang_c, ang_c], axis=-1)
    cos, sin = jnp.cos(ang), jnp.sin(ang)
    lo = (jnp.arange(HEAD_DIM) % axis_dim) < (axis_dim // 2)
    sin_lo = jnp.where(lo, -sin, 0.0)
    sin_hi = jnp.where(lo, 0.0, sin)
    one, zero = jnp.ones((tm, HEAD_DIM), F32), jnp.zeros((tm, HEAD_DIM), F32)
    return (jnp.concatenate([cos, one]), jnp.concatenate([sin_lo, zero]), jnp.concatenate([sin_hi, zero]))


def _attn_full_kernel(*refs, n_src, has_sink, groups):
    if has_sink:
        sink_ref, refs = refs[0], refs[1:]
    q_ref, o_ref = refs[0], refs[1 + 2 * n_src]
    kv = refs[1:1 + 2 * n_src]
    kvh = pl.program_id(1)
    for g in range(groups):
        q = q_ref[:, g * HEAD_DIM:(g + 1) * HEAD_DIM]
        scores = [_dot_nt(q, kv[2 * j][...]) for j in range(n_src)]
        m = functools.reduce(jnp.maximum, [jnp.max(s, axis=-1, keepdims=True) for s in scores])
        if has_sink:
            sink = sink_ref[kvh * groups + g]
            m = jnp.maximum(m, sink)
        es = [jnp.exp(s - m) for s in scores]
        den = functools.reduce(jnp.add, [jnp.sum(e, axis=-1, keepdims=True) for e in es])
        if has_sink:
            den = den + jnp.exp(sink - m)
        acc = functools.reduce(jnp.add, [_dot(es[j].astype(BF16), kv[2 * j + 1][...]) for j in range(n_src)])
        o_ref[:, g * HEAD_DIM:(g + 1) * HEAD_DIM] = (acc / den).astype(o_ref.dtype)


def _attn_full(qkv, *, batch, n_q, q_base, sources, q_col, k_col, v_col, heads, kv_heads, sink=None):
    groups = heads // kv_heads
    tq = _pick(n_q, (256, 128))
    qw = groups * HEAD_DIM
    nq = n_q // tq
    in_specs = []
    args = []
    if sink is not None:
        in_specs.append(pl.BlockSpec(memory_space=pltpu.SMEM))
        args.append(sink)
    in_specs.append(pl.BlockSpec((tq, qw), lambda b, h, i: (q_base // tq + b * nq + i, q_col // qw + h)))
    args.append(qkv)
    for base, n in sources:
        for col in (k_col, v_col):
            in_specs.append(pl.BlockSpec(
                (n, HEAD_DIM), lambda b, h, i, base=base, n=n, col=col: (base // n + b, col // HEAD_DIM + h)))
            args.append(qkv)
    kern = functools.partial(_attn_full_kernel, n_src=len(sources), has_sink=sink is not None, groups=groups)
    return pl.pallas_call(
        kern,
        grid=(batch, kv_heads, nq),
        in_specs=in_specs,
        out_specs=pl.BlockSpec((tq, qw), lambda b, h, i: (b * nq + i, h)),
        out_shape=jax.ShapeDtypeStruct((batch * n_q, heads * HEAD_DIM), BF16),
        compiler_params=_cparams(("parallel", "parallel", "arbitrary")),
        name="attn_full",
    )(*args)


def _attn_win_kernel(sink_ref, q_ref, k_ref, v_ref, kc_ref, vc_ref, o_ref, *, groups, seq):
    kvh = pl.program_id(1)
    n = pl.program_id(2)
    span = 3 * Q_BLOCK
    start = jnp.clip((n - 1) * Q_BLOCK, 0, seq - span)
    start = pl.multiple_of(start, Q_BLOCK)
    kb = k_ref[pl.ds(start, span), :]
    vb = v_ref[pl.ds(start, span), :]
    q_pos = n * Q_BLOCK + lax.broadcasted_iota(jnp.int32, (Q_BLOCK, span), 0)
    k_pos = start + lax.broadcasted_iota(jnp.int32, (Q_BLOCK, span), 1)
    valid = jnp.abs(k_pos - q_pos) <= WINDOW
    for g in range(groups):
        q = q_ref[:, g * HEAD_DIM:(g + 1) * HEAD_DIM]
        s_win = jnp.where(valid, _dot_nt(q, kb), -jnp.inf)
        s_ctx = _dot_nt(q, kc_ref[...])
        sink = sink_ref[kvh * groups + g]
        m = jnp.maximum(jnp.maximum(jnp.max(s_win, axis=-1, keepdims=True),
                                    jnp.max(s_ctx, axis=-1, keepdims=True)), sink)
        e_win = jnp.exp(s_win - m)
        e_ctx = jnp.exp(s_ctx - m)
        den = (jnp.sum(e_win, axis=-1, keepdims=True) + jnp.sum(e_ctx, axis=-1, keepdims=True)
               + jnp.exp(sink - m))
        acc = _dot(e_win.astype(BF16), vb) + _dot(e_ctx.astype(BF16), vc_ref[...])
        o_ref[:, g * HEAD_DIM:(g + 1) * HEAD_DIM] = (acc / den).astype(o_ref.dtype)


def _attn_window(qkv, sink, rows):
    b, s, c = rows.b, rows.s, rows.c
    groups = B_HEADS // B_KV_HEADS
    qw = groups * HEAD_DIM
    nq = s // Q_BLOCK
    q_col = A_Q + 2 * A_KV
    k_col = q_col + B_Q
    v_col = k_col + B_KV
    ctx_base = rows.n_lat // c

    def kv_spec(n, base, col):
        return pl.BlockSpec((n, HEAD_DIM), lambda bb, h, i: (base + bb, col // HEAD_DIM + h))

    return pl.pallas_call(
        functools.partial(_attn_win_kernel, groups=groups, seq=s),
        grid=(b, B_KV_HEADS, nq),
        in_specs=[pl.BlockSpec(memory_space=pltpu.SMEM),
                  pl.BlockSpec((Q_BLOCK, qw), lambda bb, h, i: (bb * nq + i, q_col // qw + h)),
                  kv_spec(s, 0, k_col), kv_spec(s, 0, v_col),
                  kv_spec(c, ctx_base, k_col), kv_spec(c, ctx_base, v_col)],
        out_specs=pl.BlockSpec((Q_BLOCK, qw), lambda bb, h, i: (bb * nq + i, h)),
        out_shape=jax.ShapeDtypeStruct((b * s, B_Q), BF16),
        compiler_params=_cparams(("parallel", "parallel", "arbitrary")),
        name="attn_window",
    )(sink, qkv, qkv, qkv, qkv, qkv)


def _delta_prep_kernel(prev_ref, cur_ref, next_ref, ab_ref, w_ref, alog_ref, dtb_ref, qkv_ref, gb_ref,
                       *, lat_blocks, seq_blocks, ctx_blocks):
    i = pl.program_id(0)
    tm = cur_ref.shape[0]
    is_lat = i < lat_blocks
    pos = jnp.where(is_lat, i % seq_blocks, (i - lat_blocks) % ctx_blocks)
    n_pos = jnp.where(is_lat, seq_blocks, ctx_blocks)
    first = pos == 0
    last = pos == n_pos - 1
    keep_prev = jnp.where(first, 0.0, 1.0)
    keep_next = jnp.where(last, 0.0, 1.0)
    ext = tm + 2 * HALO
    n_qk = 2 * C_QK_HEADS
    for j in range(C_CONV // HEAD_DIM):
        sl = slice(j * HEAD_DIM, (j + 1) * HEAD_DIM)
        xt = jnp.concatenate([prev_ref[:, sl] * keep_prev, cur_ref[:, sl], next_ref[:, sl] * keep_next], axis=0)
        acc = None
        for tap in range(CONV_K):
            shift = (CONV_K // 2 - tap) % ext
            xs = xt if shift == 0 else pltpu.roll(xt, shift, 0)
            term = xs[HALO:HALO + tm] * w_ref[tap:tap + 1, sl]
            acc = term if acc is None else acc + term
        y = _silu(acc)
        if j < n_qk:
            y = y * lax.rsqrt(jnp.sum(y * y, axis=-1, keepdims=True) + RMS_EPS)
            if j < C_QK_HEADS:
                y = y * (HEAD_DIM ** -0.5)
        qkv_ref[:, sl] = y
    ab = ab_ref[...]
    lane = lax.broadcasted_iota(jnp.int32, ab.shape, 1)
    n_g = 2 * C_V_HEADS
    z = ab + dtb_ref[...]
    softplus = jnp.maximum(z, 0.0) + jnp.log(1.0 + jnp.exp(-jnp.abs(z)))
    g = -jnp.exp(alog_ref[...]) * softplus
    gb_ref[...] = jnp.where(lane < n_g, g, jax.nn.sigmoid(ab))


def _delta_prep(rows, proj_main, proj_ab, conv_w, a_log, dt_bias):
    r = proj_main.shape[0]
    tm = rows.tm
    hb = tm // HALO
    last_halo = r // HALO - 1
    pad = AB_W - a_log.size
    alog = jnp.pad(a_log.reshape(1, -1), ((0, 0), (0, pad)))
    dtb = jnp.pad(dt_bias.reshape(1, -1), ((0, 0), (0, pad)))
    kern = functools.partial(_delta_prep_kernel, lat_blocks=rows.lat_blocks, seq_blocks=rows.s // tm,
                             ctx_blocks=rows.c // tm)
    return pl.pallas_call(
        kern,
        grid=(r // tm,),
        in_specs=[pl.BlockSpec((HALO, C_CONV), lambda i: (jnp.maximum(i * hb - 1, 0), 0)),
                  pl.BlockSpec((tm, C_CONV), lambda i: (i, 0)),
                  pl.BlockSpec((HALO, C_CONV), lambda i: (jnp.minimum((i + 1) * hb, last_halo), 0)),
                  pl.BlockSpec((tm, AB_W), lambda i: (i, 0)),
                  pl.BlockSpec((CONV_K, C_CONV), lambda i: (0, 0)),
                  _vec_spec(AB_W), _vec_spec(AB_W)],
        out_specs=[pl.BlockSpec((tm, C_CONV), lambda i: (i, 0)), pl.BlockSpec((tm, AB_W), lambda i: (i, 0))],
        out_shape=[jax.ShapeDtypeStruct((r, C_CONV), F32), jax.ShapeDtypeStruct((r, AB_W), F32)],
        compiler_params=_cparams(("parallel",)),
        name="delta_prep",
    )(proj_main, proj_main, proj_main, proj_ab, conv_w, alog, dtb)


def _delta_kernel(qkv_f_ref, gb_f_ref, qkv_r_ref, gb_r_ref, o_f_ref, o_r_ref, state_ref):
    @pl.when(pl.program_id(1) == 0)
    def _():
        state_ref[...] = jnp.zeros_like(state_ref)

    row = lax.broadcasted_iota(jnp.int32, (CHUNK, CHUNK), 0)
    col = lax.broadcasted_iota(jnp.int32, (CHUNK, CHUNK), 1)
    eye = jnp.where(row == col, 1.0, 0.0)
    rep = C_V_HEADS // C_QK_HEADS
    n_sq = int(math.log2(CHUNK)) - 1
    chains = []
    for direction, (qkv_ref, gb_ref, o_ref) in enumerate(((qkv_f_ref, gb_f_ref, o_f_ref),
                                                          (qkv_r_ref, gb_r_ref, o_r_ref))):
        if direction == 0:
            incl, strict, last = row >= col, row > col, CHUNK - 1
        else:
            incl, strict, last = row <= col, row < col, 0
        gb = gb_ref[...]
        gcum = _dot(jnp.where(incl, 1.0, 0.0), gb, precision=HIGHEST)
        gcum_t = gcum.T
        kq = []
        for qh in range(C_QK_HEADS):
            q = qkv_ref[:, qh * HEAD_DIM:(qh + 1) * HEAD_DIM]
            k = qkv_ref[:, C_QK + qh * HEAD_DIM:C_QK + (qh + 1) * HEAD_DIM]
            prod = _dot_nt(jnp.concatenate([k, q], axis=0).astype(BF16), k.astype(BF16))
            kq.append((q, k, prod[:CHUNK], prod[CHUNK:]))
        for h in range(C_V_HEADS):
            lane = direction * C_V_HEADS + h
            q, k, kk, qk = kq[h // rep]
            g_col = gcum[:, lane:lane + 1]
            beta = gb[:, 2 * C_V_HEADS + lane:2 * C_V_HEADS + lane + 1]
            decay = jnp.where(incl, jnp.exp(jnp.minimum(g_col - gcum_t[lane:lane + 1, :], 0.0)), 0.0)
            v = qkv_ref[:, 2 * C_QK + h * HEAD_DIM:2 * C_QK + (h + 1) * HEAD_DIM]
            p = jnp.where(strict, -(kk * beta) * decay, 0.0)
            chains.append(dict(
                h=h, slot=lane, o_ref=o_ref, q=q, k=k, g_col=g_col, e_g=jnp.exp(g_col),
                g_last=gcum[last:last + 1, lane:lane + 1], beta=beta, v=v, qk=qk * decay, p=p, t=eye + p))
    for _ in range(n_sq):
        for ch in chains:
            pb = ch["p"].astype(BF16)
            ch["p"] = _dot(pb, pb)
        for ch in chains:
            ch["t"] = ch["t"] + _dot(ch["t"].astype(BF16), ch["p"].astype(BF16))
    for ch in chains:
        rhs = jnp.concatenate([ch["v"] * ch["beta"], ch["k"] * (ch["beta"] * ch["e_g"])], axis=1)
        uw = _dot(ch["t"].astype(BF16), rhs.astype(BF16))
        ch["u"], ch["w"] = uw[:, :HEAD_DIM], uw[:, HEAD_DIM:]
    for ch in chains:
        ch["state"] = state_ref[ch["slot"]]
        lhs = jnp.concatenate([ch["w"], ch["q"] * ch["e_g"]], axis=0)
        ch["ws_qs"] = _dot(lhs.astype(BF16), ch["state"].astype(BF16))
    for ch in chains:
        ch["v_new"] = (ch["u"] - ch["ws_qs"][:CHUNK]).astype(BF16)
    for ch in chains:
        h = ch["h"]
        ch["o_ref"][:, h * HEAD_DIM:(h + 1) * HEAD_DIM] = ch["ws_qs"][CHUNK:] + _dot(ch["qk"].astype(BF16), ch["v_new"])
    for ch in chains:
        k_dec = ch["k"] * jnp.exp(ch["g_last"] - ch["g_col"])
        state_ref[ch["slot"]] = ch["state"] * jnp.exp(ch["g_last"]) + _dot_tn(k_dec.astype(BF16), ch["v_new"])


def _delta_scan(rows, dqkv, gb):
    b, s, c = rows.b, rows.s, rows.c
    ctx_chunks, lat_chunks = c // CHUNK, s // CHUNK
    ctx_base = rows.n_lat // CHUNK

    def fwd_map(bb, t):
        return (jnp.where(t < ctx_chunks, ctx_base + bb * ctx_chunks + t, bb * lat_chunks + t - ctx_chunks), 0)

    def rev_map(bb, t):
        ctx_i, lat_i = ctx_chunks - 1 - t, lat_chunks - 1 - (t - ctx_chunks)
        return (jnp.where(t < ctx_chunks, ctx_base + bb * ctx_chunks + ctx_i, bb * lat_chunks + lat_i), 0)

    out = jax.ShapeDtypeStruct((rows.r, C_V), F32)
    return pl.pallas_call(
        _delta_kernel,
        grid=(b, ctx_chunks + lat_chunks),
        in_specs=[pl.BlockSpec((CHUNK, C_CONV), fwd_map), pl.BlockSpec((CHUNK, AB_W), fwd_map),
                  pl.BlockSpec((CHUNK, C_CONV), rev_map), pl.BlockSpec((CHUNK, AB_W), rev_map)],
        out_specs=[pl.BlockSpec((CHUNK, C_V), fwd_map), pl.BlockSpec((CHUNK, C_V), rev_map)],
        out_shape=[out, out],
        scratch_shapes=[pltpu.VMEM((2 * C_V_HEADS, HEAD_DIM, HEAD_DIM), F32)],
        compiler_params=_cparams(("parallel", "arbitrary")),
        name="delta_scan",
    )(dqkv, gb, dqkv, gb)


def _merge_kernel(oa_ref, ob_ref, of_ref, or_ref, z_ref, nc_ref, ga_ref, gb_ref, gc_ref,
                  wa_ref, wb_ref, wc_ref, y_ref, oc_ref):
    for h in range(C_V_HEADS):
        sl = slice(h * HEAD_DIM, (h + 1) * HEAD_DIM)
        o = of_ref[:, sl] + or_ref[:, sl]
        oc_ref[:, sl] = (_rms(o, nc_ref[...]) * _silu(z_ref[:, sl])).astype(BF16)
    y = jax.nn.sigmoid(ga_ref[...]) * _dot(oa_ref[...], wa_ref[...])
    y = y + jax.nn.sigmoid(gb_ref[...]) * _dot(ob_ref[...], wb_ref[...])
    y = y + jax.nn.sigmoid(gc_ref[...]) * _dot(oc_ref[...], wc_ref[...])
    y_ref[...] = y.astype(BF16)


def _merge(n_rows, tm, o_a, o_b, o_f, o_r, proj_main, gates, norm_c, w_a, w_b, w_c):
    d = w_a.shape[1]
    z_blk = C_CONV // C_V

    def rspec(w, col=0):
        return pl.BlockSpec((tm, w), lambda i: (i, col))

    def wspec(kdim):
        return pl.BlockSpec((kdim, d), lambda i: (0, 0))

    return pl.pallas_call(
        _merge_kernel,
        grid=(n_rows // tm,),
        in_specs=[rspec(A_Q), rspec(B_Q), rspec(C_V), rspec(C_V), rspec(C_V, z_blk), _vec_spec(HEAD_DIM),
                  rspec(d, 0), rspec(d, 1), rspec(d, 2), wspec(A_Q), wspec(B_Q), wspec(C_V)],
        out_specs=rspec(d),
        out_shape=jax.ShapeDtypeStruct((n_rows, d), BF16),
        scratch_shapes=[pltpu.VMEM((tm, C_V), BF16)],
        compiler_params=_cparams(("parallel",)),
        name="branch_merge",
    )(o_a, o_b, o_f, o_r, proj_main, norm_c, gates, gates, gates, w_a, w_b, w_c)


def _outproj_kernel(y_ref, w_ref, x_ref, gt_ref, g_ref, b_ref, sc_ref, sh_ref, x1_ref, h_ref, *, alpha):
    y = _dot(y_ref[...], w_ref[...])
    x1 = _layernorm(alpha * x_ref[...] + gt_ref[0, 0] * y) * g_ref[...] + b_ref[...]
    x1_ref[...] = x1
    h_ref[...] = (x1 * (1.0 + sc_ref[0, 0]) + sh_ref[0, 0]).astype(BF16)


def _outproj(rows, n_rows, y, w_out, x, mod_l, ln_g, ln_b, alpha):
    d = x.shape[1]
    tm = rows.tm
    rspec = pl.BlockSpec((tm, d), lambda i: (i, 0))
    return pl.pallas_call(
        functools.partial(_outproj_kernel, alpha=alpha),
        grid=(n_rows // tm,),
        in_specs=[rspec, pl.BlockSpec((d, d), lambda i: (0, 0)), rspec, _mod_spec(rows, d, 2),
                  _vec_spec(d), _vec_spec(d), _mod_spec(rows, d, 4), _mod_spec(rows, d, 3)],
        out_specs=[rspec, rspec],
        out_shape=[jax.ShapeDtypeStruct((n_rows, d), F32), jax.ShapeDtypeStruct((n_rows, d), BF16)],
        compiler_params=_cparams(("parallel",)),
        name="out_proj_norm",
    )(y, w_out, x, mod_l, ln_g, ln_b, mod_l, mod_l)


def _router_kernel(x_ref, sc_ref, sh_ref, wr_ref, rb_ref, idx_ref, gate_ref, rank_ref, cnt_ref, carry_ref):
    @pl.when(pl.program_id(0) == 0)
    def _():
        carry_ref[...] = jnp.zeros_like(carry_ref)

    h = x_ref[...] * (1.0 + sc_ref[0, 0]) + sh_ref[0, 0]
    scores = jax.nn.sigmoid(_dot(h, wr_ref[...], precision=HIGHEST))
    tm, n_exp = scores.shape
    sel = scores + rb_ref[...]
    lane = lax.broadcasted_iota(jnp.int32, (tm, n_exp), 1)
    slot = lax.broadcasted_iota(jnp.int32, (tm, TOP_K), 1)
    idx = jnp.zeros((tm, TOP_K), jnp.int32)
    gate = jnp.zeros((tm, TOP_K), F32)
    mask = jnp.zeros((tm, n_exp), F32)
    hits = []
    for k in range(TOP_K):
        best = jnp.max(sel, axis=-1, keepdims=True)
        choice = jnp.min(jnp.where(sel == best, lane, n_exp), axis=-1, keepdims=True)
        hit = lane == choice
        hits.append(hit)
        idx = jnp.where(slot == k, choice, idx)
        gate = jnp.where(slot == k, jnp.sum(jnp.where(hit, scores, 0.0), axis=-1, keepdims=True), gate)
        sel = jnp.where(hit, -jnp.inf, sel)
        mask = jnp.where(hit, 1.0, mask)
    idx_ref[...] = idx
    gate_ref[...] = gate / jnp.sum(gate, axis=-1, keepdims=True) * ROUTED_SCALE
    r_i = lax.broadcasted_iota(jnp.int32, (tm, tm), 0)
    c_i = lax.broadcasted_iota(jnp.int32, (tm, tm), 1)
    before = jnp.where(c_i < r_i, 1.0, 0.0).astype(BF16)
    prefix = _dot(before, mask.astype(BF16)) + carry_ref[...]
    rank = jnp.zeros((tm, TOP_K), F32)
    for k in range(TOP_K):
        rank = jnp.where(slot == k, jnp.sum(jnp.where(hits[k], prefix, 0.0), axis=-1, keepdims=True), rank)
    rank_ref[...] = rank.astype(jnp.int32)
    carry_ref[...] = carry_ref[...] + jnp.sum(mask, axis=0, keepdims=True)
    cnt_ref[...] = carry_ref[...]


def _router(rows, n_rows, x1, mod_l, w_router, router_bias):
    d, n_exp = w_router.shape
    tm = rows.tm
    kspec = pl.BlockSpec((tm, TOP_K), lambda i: (i, 0))
    return pl.pallas_call(
        _router_kernel,
        grid=(n_rows // tm,),
        in_specs=[pl.BlockSpec((tm, d), lambda i: (i, 0)), _mod_spec(rows, d, 4), _mod_spec(rows, d, 3),
                  pl.BlockSpec((d, n_exp), lambda i: (0, 0)), _vec_spec(n_exp)],
        out_specs=[kspec, kspec, kspec, _vec_spec(n_exp)],
        out_shape=[jax.ShapeDtypeStruct((n_rows, TOP_K), jnp.int32), jax.ShapeDtypeStruct((n_rows, TOP_K), F32),
                   jax.ShapeDtypeStruct((n_rows, TOP_K), jnp.int32), jax.ShapeDtypeStruct((1, n_exp), F32)],
        scratch_shapes=[pltpu.VMEM((1, n_exp), F32)],
        compiler_params=_cparams(("arbitrary",)),
        name="router",
    )(x1, mod_l, mod_l, w_router, router_bias.reshape(1, n_exp))


def _expert_kernel(be_ref, nu_ref, x_ref, w1_ref, w3_ref, w2_ref, o_ref, w1b, w3b, w2b, prev_ref):
    blk = pl.program_id(0)
    e = be_ref[blk]

    @pl.when(blk == 0)
    def _():
        prev_ref[0] = -1

    @pl.when(blk < nu_ref[0])
    def _():
        @pl.when(e != prev_ref[0])
        def _():
            w1b[...] = w1_ref[0, 0].astype(BF16)
            w3b[...] = w3_ref[0, 0].astype(BF16)
            w2b[...] = w2_ref[0, 0].astype(BF16)
            prev_ref[0] = e

        x = x_ref[...]
        hidden = _silu(_dot(x, w1b[...])) * _dot(x, w3b[...])
        o_ref[...] = _dot(hidden.astype(BF16), w2b[...]).astype(o_ref.dtype)

    @pl.when(blk >= nu_ref[0])
    def _():
        o_ref[...] = jnp.zeros_like(o_ref)


def _experts(xs, block_expert, n_used, w1, w3, w2, layer):
    n_slots, d = xs.shape
    ff = w1.shape[-1]
    n_blocks = n_slots // MOE_ROWS
    grid_spec = pltpu.PrefetchScalarGridSpec(
        num_scalar_prefetch=2,
        grid=(n_blocks,),
        in_specs=[pl.BlockSpec((MOE_ROWS, d), lambda i, be, nu: (i, 0)),
                  pl.BlockSpec((1, 1, d, ff), lambda i, be, nu: (layer, be[i], 0, 0)),
                  pl.BlockSpec((1, 1, d, ff), lambda i, be, nu: (layer, be[i], 0, 0)),
                  pl.BlockSpec((1, 1, ff, d), lambda i, be, nu: (layer, be[i], 0, 0))],
        out_specs=pl.BlockSpec((MOE_ROWS, d), lambda i, be, nu: (i, 0)),
        scratch_shapes=[pltpu.VMEM((d, ff), BF16), pltpu.VMEM((d, ff), BF16), pltpu.VMEM((ff, d), BF16),
                        pltpu.SMEM((1,), jnp.int32)],
    )
    return pl.pallas_call(
        _expert_kernel,
        grid_spec=grid_spec,
        out_shape=jax.ShapeDtypeStruct((n_slots, d), BF16),
        compiler_params=_cparams(("arbitrary",)),
        name="experts",
    )(block_expert, n_used, xs, w1, w3, w2)


def _combine_kernel(*refs, alpha, d, emit_h):
    (ys_ref, gate_ref, h_ref, ws1_ref, ws3_ref, ws2_ref, x_ref, gt_ref, g_ref, b_ref) = refs[:10]
    hidden = _silu(_dot(h_ref[...], ws1_ref[...])) * _dot(h_ref[...], ws3_ref[...])
    f = _dot(hidden.astype(BF16), ws2_ref[...])
    gate = gate_ref[...]
    for k in range(TOP_K):
        f = f + ys_ref[k].astype(F32) * gate[:, k:k + 1]
    x2 = _layernorm(alpha * x_ref[...] + gt_ref[0, 0] * f) * g_ref[...] + b_ref[...]
    if emit_h:
        sc_ref, sh_ref, x2_ref, hn_ref = refs[10:]
        x2_ref[...] = x2
        hn_ref[...] = (x2 * (1.0 + sc_ref[0, 0]) + sh_ref[0, 0]).astype(BF16)
    else:
        refs[10][...] = x2


def _combine(rows, n_rows, ys_g, gate, h2, ws1, ws3, ws2, x1, mod_l, ln_g, ln_b, alpha, mod_next):
    d = x1.shape[1]
    sf = ws1.shape[1]
    tm = rows.tm
    rspec = pl.BlockSpec((tm, d), lambda i: (i, 0))
    in_specs = [pl.BlockSpec((TOP_K, tm, d), lambda i: (0, i, 0)), pl.BlockSpec((tm, TOP_K), lambda i: (i, 0)), rspec,
                pl.BlockSpec((d, sf), lambda i: (0, 0)), pl.BlockSpec((d, sf), lambda i: (0, 0)),
                pl.BlockSpec((sf, d), lambda i: (0, 0)), rspec, _mod_spec(rows, d, 5), _vec_spec(d), _vec_spec(d)]
    args = [ys_g, gate, h2, ws1, ws3, ws2, x1, mod_l, ln_g, ln_b]
    emit_h = mod_next is not None
    if emit_h:
        in_specs += [_mod_spec(rows, d, 1), _mod_spec(rows, d, 0)]
        args += [mod_next, mod_next]
        out_specs = [rspec, rspec]
        out_shape = [jax.ShapeDtypeStruct((n_rows, d), F32), jax.ShapeDtypeStruct((n_rows, d), BF16)]
    else:
        out_specs = rspec
        out_shape = jax.ShapeDtypeStruct((n_rows, d), F32)
    return pl.pallas_call(
        functools.partial(_combine_kernel, alpha=alpha, d=d, emit_h=emit_h),
        grid=(n_rows // tm,),
        in_specs=in_specs,
        out_specs=out_specs,
        out_shape=out_shape,
        compiler_params=_cparams(("parallel",)),
        name="moe_combine_norm",
    )(*args)


def _permute_w_in(w):
    a_end = A_Q + 2 * A_KV
    b_end = a_end + B_Q + 2 * B_KV
    c_end = b_end + C_CONV + C_V
    ab_end = c_end + 4 * C_V_HEADS
    main = jnp.concatenate([w[:, b_end:c_end], w[:, :b_end]], axis=1).astype(BF16)
    ab = jnp.pad(w[:, c_end:ab_end], ((0, 0), (0, AB_W - 4 * C_V_HEADS))).astype(BF16)
    gates = w[:, ab_end:].astype(BF16)
    return main, gates, ab


def kernel(x, c, ctx, c_ctx, w_mod, b_mod, w_in, q_norm_a, k_norm_a, sink_b, conv_c, a_log_c, dt_bias_c, norm_c,
           w_br_a, w_br_b, w_br_c, w_out, ln1_g, ln1_b, w_router, router_bias, w1, w3, w2, ws1, ws3, ws2,
           ln2_g, ln2_b):
    b, s, d = x.shape
    n_ctx = ctx.shape[1]
    depth = w_mod.shape[0]
    n_exp = w_router.shape[-1]
    alpha = (2.0 * depth) ** 0.25
    rows = _Rows(b, s, n_ctx)
    tm = rows.tm

    c_all = jnp.concatenate([c, c_ctx[None, :], jnp.zeros((8 - b - 1, d), F32)], axis=0)
    mod = _modulation(c_all, w_mod, b_mod).reshape(depth, 8, N_MOD, 1, d)
    tables = _rope_tables(s, tm)

    x_all = jnp.concatenate([x.reshape(b * s, d), ctx.reshape(b * n_ctx, d)], axis=0)
    x_cur, h = _entry(rows, x_all, mod[0])

    for l in range(depth):
        need_ctx = l < depth - 1
        n_rows = rows.r if need_ctx else rows.n_lat
        mod_l = mod[l]
        w_main, w_gates, w_ab = _permute_w_in(w_in[l])
        proj_main = _matmul(h, w_main, F32)
        gates = _matmul(h, w_gates, F32)
        proj_ab = _matmul(h, w_ab, F32)

        qkv = _attn_prep(rows, proj_main, tables, q_norm_a[l].reshape(1, -1), k_norm_a[l].reshape(1, -1))
        a_k, a_v = A_Q, A_Q + A_KV
        o_a = _attn_full(qkv, batch=b, n_q=s, q_base=0, sources=[(rows.n_lat, n_ctx), (0, s)],
                         q_col=0, k_col=a_k, v_col=a_v, heads=A_HEADS, kv_heads=A_KV_HEADS)
        o_b = _attn_window(qkv, sink_b[l], rows)
        if need_ctx:
            b_q = A_Q + 2 * A_KV
            o_a_c = _attn_full(qkv, batch=b, n_q=n_ctx, q_base=rows.n_lat, sources=[(rows.n_lat, n_ctx)],
                               q_col=0, k_col=a_k, v_col=a_v, heads=A_HEADS, kv_heads=A_KV_HEADS)
            o_b_c = _attn_full(qkv, batch=b, n_q=n_ctx, q_base=rows.n_lat, sources=[(rows.n_lat, n_ctx)],
                               q_col=b_q, k_col=b_q + B_Q, v_col=b_q + B_Q + B_KV, heads=B_HEADS,
                               kv_heads=B_KV_HEADS, sink=sink_b[l])
            o_a = jnp.concatenate([o_a, o_a_c], axis=0)
            o_b = jnp.concatenate([o_b, o_b_c], axis=0)

        dqkv, gbeta = _delta_prep(rows, proj_main, proj_ab, conv_c[l], a_log_c[l], dt_bias_c[l])
        o_f, o_r = _delta_scan(rows, dqkv, gbeta)

        y = _merge(n_rows, tm, o_a, o_b, o_f, o_r, proj_main, gates, norm_c[l].reshape(1, -1),
                   w_br_a[l].astype(BF16), w_br_b[l].astype(BF16), w_br_c[l].astype(BF16))
        x1, h2 = _outproj(rows, n_rows, y, w_out[l].astype(BF16), x_cur, mod_l,
                          ln1_g[l].reshape(1, -1), ln1_b[l].reshape(1, -1), alpha)

        idx, gate, rank, counts = _router(rows, n_rows, x1, mod_l, w_router[l], router_bias[l])
        counts = counts[0].astype(jnp.int32)
        padded = (counts + MOE_ROWS - 1) // MOE_ROWS * MOE_ROWS
        padded_end = jnp.cumsum(padded)
        group_start = padded_end - padded
        experts_iota = jnp.arange(n_exp, dtype=jnp.int32)
        dest = rank + jnp.sum(jnp.where(idx[:, :, None] == experts_iota, group_start, 0), axis=-1)
        n_blocks = -(-(n_rows * TOP_K) // MOE_ROWS) + n_exp
        block_start = jnp.arange(n_blocks, dtype=jnp.int32)[:, None] * MOE_ROWS
        block_expert = jnp.minimum(jnp.sum((padded_end[None, :] <= block_start).astype(jnp.int32), axis=-1), n_exp - 1)
        n_used = (padded_end[-1:] // MOE_ROWS).astype(jnp.int32)
        tok = jnp.broadcast_to(jnp.arange(n_rows, dtype=jnp.int32)[:, None], dest.shape)
        slot_tok = jnp.zeros((n_blocks * MOE_ROWS,), jnp.int32).at[dest.reshape(-1)].set(tok.reshape(-1))
        xs = jnp.take(h2, slot_tok, axis=0)
        ys = _experts(xs, block_expert, n_used, w1, w3, w2, l)
        ys_g = jnp.take(ys, dest.T.reshape(-1), axis=0).reshape(TOP_K, n_rows, d)

        mod_next = mod[l + 1] if need_ctx else None
        out = _combine(rows, n_rows, ys_g, gate, h2, ws1[l].astype(BF16), ws3[l].astype(BF16), ws2[l].astype(BF16),
                       x1, mod_l, ln2_g[l].reshape(1, -1), ln2_b[l].reshape(1, -1), alpha, mod_next)
        if need_ctx:
            x_cur, h = out
        else:
            x_cur = out
    return x_cur[:rows.n_lat].reshape(b, s, d)
```

```python
import functools
import math

import jax
import jax.numpy as jnp
from jax import lax
from jax.experimental import pallas as pl
from jax.experimental.pallas import tpu as pltpu

F32 = jnp.float32
BF16 = jnp.bfloat16
HIGHEST = lax.Precision.HIGHEST

HEAD_DIM = 128
GRID_W = 64
ROPE_THETA = 10000.0
A_HEADS, A_KV_HEADS = 8, 2
B_HEADS, B_KV_HEADS = 8, 2
WINDOW = 128
Q_BLOCK = 128
C_QK_HEADS, C_V_HEADS = 4, 8
CONV_K = 5
CHUNK = 64
TOP_K = 8
ROUTED_SCALE = 2.5
LN_EPS = 1e-5
RMS_EPS = 1e-6
MOE_ROWS = 256
N_MOD = 6
HALO = 8
VMEM_LIMIT = 56 * 2**20

A_Q, A_KV = A_HEADS * HEAD_DIM, A_KV_HEADS * HEAD_DIM
B_Q, B_KV = B_HEADS * HEAD_DIM, B_KV_HEADS * HEAD_DIM
C_QK, C_V = C_QK_HEADS * HEAD_DIM, C_V_HEADS * HEAD_DIM
C_CONV = 2 * C_QK + C_V
ATT_W = A_Q + 2 * A_KV + B_Q + 2 * B_KV
MAIN_W = C_CONV + C_V + ATT_W
AB_W = 128


def _cparams(sem):
    return pltpu.CompilerParams(dimension_semantics=sem, vmem_limit_bytes=VMEM_LIMIT)


def _pick(n, prefs):
    for p in prefs:
        if n % p == 0:
            return p
    return n


def _dot(a, b, **kw):
    return jnp.dot(a, b, preferred_element_type=F32, **kw)


def _dot_nt(a, b):
    return lax.dot_general(a, b, (((1,), (1,)), ((), ())), preferred_element_type=F32)


def _dot_tn(a, b):
    return lax.dot_general(a, b, (((0,), (0,)), ((), ())), preferred_element_type=F32)


def _silu(x):
    return x * jax.nn.sigmoid(x)


def _mod_kernel(c_ref, w_ref, b_ref, o_ref):
    o_ref[0] = _dot(_silu(c_ref[...]), w_ref[0], precision=HIGHEST) + b_ref[0]


def _modulation(c_all, w_mod, b_mod):
    depth, d, width = w_mod.shape
    tn = _pick(width, (1024, 512, 256, 128))
    return pl.pallas_call(
        _mod_kernel,
        grid=(depth, width // tn),
        in_specs=[pl.BlockSpec((8, d), lambda l, j: (0, 0)),
                  pl.BlockSpec((1, d, tn), lambda l, j: (l, 0, j)),
                  pl.BlockSpec((1, 1, tn), lambda l, j: (l, 0, j))],
        out_specs=pl.BlockSpec((1, 8, tn), lambda l, j: (l, 0, j)),
        out_shape=jax.ShapeDtypeStruct((depth, 8, width), F32),
        compiler_params=_cparams(("parallel", "parallel")),
        name="modulation",
    )(c_all, w_mod, b_mod.reshape(depth, 1, width))


class _Rows:
    def __init__(self, b, s, c):
        self.b, self.s, self.c = b, s, c
        self.tm = 256 if c % 256 == 0 else 128
        self.n_lat = b * s
        self.r = b * s + b * c
        self.lat_blocks = self.n_lat // self.tm
        self.blocks = self.r // self.tm

    def group(self, i):
        return jnp.where(i < self.lat_blocks, i // (self.s // self.tm), self.b)


def _mod_spec(rows, d, which):
    return pl.BlockSpec((1, 1, 1, d), lambda i: (rows.group(i), which, 0, 0))


def _vec_spec(d):
    return pl.BlockSpec((1, d), lambda i: (0, 0))


def _layernorm(x):
    mu = jnp.mean(x, axis=-1, keepdims=True)
    xc = x - mu
    var = jnp.mean(xc * xc, axis=-1, keepdims=True)
    return xc * lax.rsqrt(var + LN_EPS)


def _entry_kernel(x_ref, sc_ref, sh_ref, x0_ref, h_ref):
    xn = _layernorm(x_ref[...])
    x0_ref[...] = xn
    h_ref[...] = (xn * (1.0 + sc_ref[0, 0]) + sh_ref[0, 0]).astype(BF16)


def _entry(rows, x_all, mod_l):
    r, d = x_all.shape
    tm = rows.tm
    row_spec = pl.BlockSpec((tm, d), lambda i: (i, 0))
    return pl.pallas_call(
        _entry_kernel,
        grid=(r // tm,),
        in_specs=[row_spec, _mod_spec(rows, d, 1), _mod_spec(rows, d, 0)],
        out_specs=[row_spec, row_spec],
        out_shape=[jax.ShapeDtypeStruct((r, d), F32), jax.ShapeDtypeStruct((r, d), BF16)],
        compiler_params=_cparams(("parallel",)),
        name="entry_norm",
    )(x_all, mod_l, mod_l)


def _mm_kernel(a_ref, w_ref, o_ref):
    o_ref[...] = _dot(a_ref[...], w_ref[...]).astype(o_ref.dtype)


def _matmul(a, w, out_dtype):
    m, k = a.shape
    n = w.shape[1]
    tm = _pick(m, (1024, 512, 256, 128))
    tn = _pick(n, (1024, 512, 256, 128))
    return pl.pallas_call(
        _mm_kernel,
        grid=(n // tn, m // tm),
        in_specs=[pl.BlockSpec((tm, k), lambda j, i: (i, 0)),
                  pl.BlockSpec((k, tn), lambda j, i: (0, j))],
        out_specs=pl.BlockSpec((tm, tn), lambda j, i: (i, j)),
        out_shape=jax.ShapeDtypeStruct((m, n), out_dtype),
        compiler_params=_cparams(("parallel", "parallel")),
        name="in_proj",
    )(a, w)


def _rope(t, cos, sin_lo, sin_hi):
    return t * cos + pltpu.roll(t, HEAD_DIM - 32, 1) * sin_lo + pltpu.roll(t, 32, 1) * sin_hi


def _rms(t, g):
    return t * lax.rsqrt(jnp.mean(t * t, axis=-1, keepdims=True) + RMS_EPS) * g


def _attn_prep_kernel(x_ref, cos_ref, slo_ref, shi_ref, qn_ref, kn_ref, o_ref):
    cos, slo, shi = cos_ref[...], slo_ref[...], shi_ref[...]
    scale = HEAD_DIM ** -0.5

    def head(j):
        return x_ref[:, j * HEAD_DIM:(j + 1) * HEAD_DIM]

    def put(j, v):
        o_ref[:, j * HEAD_DIM:(j + 1) * HEAD_DIM] = v.astype(BF16)

    j = 0
    for _ in range(A_HEADS):
        put(j, _rope(_rms(head(j), qn_ref[...]), cos, slo, shi) * scale)
        j += 1
    for _ in range(A_KV_HEADS):
        put(j, _rope(_rms(head(j), kn_ref[...]), cos, slo, shi))
        j += 1
    for _ in range(A_KV_HEADS):
        put(j, head(j))
        j += 1
    for _ in range(B_HEADS):
        put(j, _rope(head(j), cos, slo, shi) * scale)
        j += 1
    for _ in range(B_KV_HEADS):
        put(j, _rope(head(j), cos, slo, shi))
        j += 1
    for _ in range(B_KV_HEADS):
        put(j, head(j))
        j += 1


def _attn_prep(rows, proj_main, tables, q_norm, k_norm):
    r = proj_main.shape[0]
    tm = rows.tm
    pos_blocks = rows.s // tm
    col_blk = (C_CONV + C_V) // ATT_W
    assert col_blk * ATT_W == C_CONV + C_V

    def tab_map(i):
        return (jnp.where(i < rows.lat_blocks, i % pos_blocks, pos_blocks), 0)

    tab_spec = pl.BlockSpec((tm, HEAD_DIM), tab_map)
    return pl.pallas_call(
        _attn_prep_kernel,
        grid=(r // tm,),
        in_specs=[pl.BlockSpec((tm, ATT_W), lambda i: (i, col_blk)), tab_spec, tab_spec, tab_spec,
                  _vec_spec(HEAD_DIM), _vec_spec(HEAD_DIM)],
        out_specs=pl.BlockSpec((tm, ATT_W), lambda i: (i, 0)),
        out_shape=jax.ShapeDtypeStruct((r, ATT_W), BF16),
        compiler_params=_cparams(("parallel",)),
        name="attn_prep",
    )(proj_main, *tables, q_norm, k_norm)


def _rope_tables(s, tm):
    row = jnp.repeat(jnp.arange(s // GRID_W, dtype=F32), GRID_W)
    col = jnp.tile(jnp.arange(GRID_W, dtype=F32), s // GRID_W)
    axis_dim = HEAD_DIM // 2
    inv_freq = ROPE_THETA ** (-jnp.arange(0, axis_dim, 2, dtype=F32) / axis_dim)
    ang_r = row[:, None] * inv_freq[None, :]
    ang_c = col[:, None] * inv_freq[None, :]
    ang = jnp.concatenate([ang_r, ang_r, ang_c, ang_c], axis=-1)
    cos, sin = jnp.cos(ang), jnp.sin(ang)
    lo = (jnp.arange(HEAD_DIM) % axis_dim) < (axis_dim // 2)
    sin_lo = jnp.where(lo, -sin, 0.0)
    sin_hi = jnp.where(lo, 0.0, sin)
    one, zero = jnp.ones((tm, HEAD_DIM), F32), jnp.zeros((tm, HEAD_DIM), F32)
    return (jnp.concatenate([cos, one]), jnp.concatenate([sin_lo, zero]), jnp.concatenate([sin_hi, zero]))


def _attn_full_kernel(*refs, n_src, has_sink, groups):
    if has_sink:
        sink_ref, refs = refs[0], refs[1:]
    q_ref, o_ref = refs[0], refs[1 + 2 * n_src]
    kv = refs[1:1 + 2 * n_src]
    kvh = pl.program_id(1)
    for g in range(groups):
        q = q_ref[:, g * HEAD_DIM:(g + 1) * HEAD_DIM]
        scores = [_dot_nt(q, kv[2 * j][...]) for j in range(n_src)]
        m = functools.reduce(jnp.maximum, [jnp.max(s, axis=-1, keepdims=True) for s in scores])
        if has_sink:
            sink = sink_ref[kvh * groups + g]
            m = jnp.maximum(m, sink)
        es = [jnp.exp(s - m) for s in scores]
        den = functools.reduce(jnp.add, [jnp.sum(e, axis=-1, keepdims=True) for e in es])
        if has_sink:
            den = den + jnp.exp(sink - m)
        acc = functools.reduce(jnp.add, [_dot(es[j].astype(BF16), kv[2 * j + 1][...]) for j in range(n_src)])
        o_ref[:, g * HEAD_DIM:(g + 1) * HEAD_DIM] = (acc / den).astype(o_ref.dtype)


def _attn_full(qkv, *, batch, n_q, q_base, sources, q_col, k_col, v_col, heads, kv_heads, sink=None):
    groups = heads // kv_heads
    tq = _pick(n_q, (256, 128))
    qw = groups * HEAD_DIM
    nq = n_q // tq
    in_specs = []
    args = []
    if sink is not None:
        in_specs.append(pl.BlockSpec(memory_space=pltpu.SMEM))
        args.append(sink)
    in_specs.append(pl.BlockSpec((tq, qw), lambda b, h, i: (q_base // tq + b * nq + i, q_col // qw + h)))
    args.append(qkv)
    for base, n in sources:
        for col in (k_col, v_col):
            in_specs.append(pl.BlockSpec(
                (n, HEAD_DIM), lambda b, h, i, base=base, n=n, col=col: (base // n + b, col // HEAD_DIM + h)))
            args.append(qkv)
    kern = functools.partial(_attn_full_kernel, n_src=len(sources), has_sink=sink is not None, groups=groups)
    return pl.pallas_call(
        kern,
        grid=(batch, kv_heads, nq),
        in_specs=in_specs,
        out_specs=pl.BlockSpec((tq, qw), lambda b, h, i: (b * nq + i, h)),
        out_shape=jax.ShapeDtypeStruct((batch * n_q, heads * HEAD_DIM), BF16),
        compiler_params=_cparams(("parallel", "parallel", "arbitrary")),
        name="attn_full",
    )(*args)


def _attn_win_kernel(sink_ref, q_ref, k_ref, v_ref, kc_ref, vc_ref, o_ref, *, groups, seq):
    kvh = pl.program_id(1)
    n = pl.program_id(2)
    span = 3 * Q_BLOCK
    start = jnp.clip((n - 1) * Q_BLOCK, 0, seq - span)
    start = pl.multiple_of(start, Q_BLOCK)
    kb = k_ref[pl.ds(start, span), :]
    vb = v_ref[pl.ds(start, span), :]
    q_pos = n * Q_BLOCK + lax.broadcasted_iota(jnp.int32, (Q_BLOCK, span), 0)
    k_pos = start + lax.broadcasted_iota(jnp.int32, (Q_BLOCK, span), 1)
    valid = jnp.abs(k_pos - q_pos) <= WINDOW
    for g in range(groups):
        q = q_ref[:, g * HEAD_DIM:(g + 1) * HEAD_DIM]
        s_win = jnp.where(valid, _dot_nt(q, kb), -jnp.inf)
        s_ctx = _dot_nt(q, kc_ref[...])
        sink = sink_ref[kvh * groups + g]
        m = jnp.maximum(jnp.maximum(jnp.max(s_win, axis=-1, keepdims=True),
                                    jnp.max(s_ctx, axis=-1, keepdims=True)), sink)
        e_win = jnp.exp(s_win - m)
        e_ctx = jnp.exp(s_ctx - m)
        den = (jnp.sum(e_win, axis=-1, keepdims=True) + jnp.sum(e_ctx, axis=-1, keepdims=True)
               + jnp.exp(sink - m))
        acc = _dot(e_win.astype(BF16), vb) + _dot(e_ctx.astype(BF16), vc_ref[...])
        o_ref[:, g * HEAD_DIM:(g + 1) * HEAD_DIM] = (acc / den).astype(o_ref.dtype)


def _attn_window(qkv, sink, rows):
    b, s, c = rows.b, rows.s, rows.c
    groups = B_HEADS // B_KV_HEADS
    qw = groups * HEAD_DIM
    nq = s // Q_BLOCK
    q_col = A_Q + 2 * A_KV
    k_col = q_col + B_Q
    v_col = k_col + B_KV
    ctx_base = rows.n_lat // c

    def kv_spec(n, base, col):
        return pl.BlockSpec((n, HEAD_DIM), lambda bb, h, i: (base + bb, col // HEAD_DIM + h))

    return pl.pallas_call(
        functools.partial(_attn_win_kernel, groups=groups, seq=s),
        grid=(b, B_KV_HEADS, nq),
        in_specs=[pl.BlockSpec(memory_space=pltpu.SMEM),
                  pl.BlockSpec((Q_BLOCK, qw), lambda bb, h, i: (bb * nq + i, q_col // qw + h)),
                  kv_spec(s, 0, k_col), kv_spec(s, 0, v_col),
                  kv_spec(c, ctx_base, k_col), kv_spec(c, ctx_base, v_col)],
        out_specs=pl.BlockSpec((Q_BLOCK, qw), lambda bb, h, i: (bb * nq + i, h)),
        out_shape=jax.ShapeDtypeStruct((b * s, B_Q), BF16),
        compiler_params=_cparams(("parallel", "parallel", "arbitrary")),
        name="attn_window",
    )(sink, qkv, qkv, qkv, qkv, qkv)


def _delta_prep_kernel(prev_ref, cur_ref, next_ref, ab_ref, w_ref, alog_ref, dtb_ref, qkv_ref, gb_ref,
                       *, lat_blocks, seq_blocks, ctx_blocks):
    i = pl.program_id(0)
    tm = cur_ref.shape[0]
    is_lat = i < lat_blocks
    pos = jnp.where(is_lat, i % seq_blocks, (i - lat_blocks) % ctx_blocks)
    n_pos = jnp.where(is_lat, seq_blocks, ctx_blocks)
    first = pos == 0
    last = pos == n_pos - 1
    keep_prev = jnp.where(first, 0.0, 1.0)
    keep_next = jnp.where(last, 0.0, 1.0)
    ext = tm + 2 * HALO
    n_qk = 2 * C_QK_HEADS
    for j in range(C_CONV // HEAD_DIM):
        sl = slice(j * HEAD_DIM, (j + 1) * HEAD_DIM)
        xt = jnp.concatenate([prev_ref[:, sl] * keep_prev, cur_ref[:, sl], next_ref[:, sl] * keep_next], axis=0)
        acc = None
        for tap in range(CONV_K):
            shift = (CONV_K // 2 - tap) % ext
            xs = xt if shift == 0 else pltpu.roll(xt, shift, 0)
            term = xs[HALO:HALO + tm] * w_ref[tap:tap + 1, sl]
            acc = term if acc is None else acc + term
        y = _silu(acc)
        if j < n_qk:
            y = y * lax.rsqrt(jnp.sum(y * y, axis=-1, keepdims=True) + RMS_EPS)
            if j < C_QK_HEADS:
                y = y * (HEAD_DIM ** -0.5)
        qkv_ref[:, sl] = y
    ab = ab_ref[...]
    lane = lax.broadcasted_iota(jnp.int32, ab.shape, 1)
    n_g = 2 * C_V_HEADS
    z = ab + dtb_ref[...]
    softplus = jnp.maximum(z, 0.0) + jnp.log(1.0 + jnp.exp(-jnp.abs(z)))
    g = -jnp.exp(alog_ref[...]) * softplus
    gb_ref[...] = jnp.where(lane < n_g, g, jax.nn.sigmoid(ab))


def _delta_prep(rows, proj_main, proj_ab, conv_w, a_log, dt_bias):
    r = proj_main.shape[0]
    tm = rows.tm
    hb = tm // HALO
    last_halo = r // HALO - 1
    pad = AB_W - a_log.size
    alog = jnp.pad(a_log.reshape(1, -1), ((0, 0), (0, pad)))
    dtb = jnp.pad(dt_bias.reshape(1, -1), ((0, 0), (0, pad)))
    kern = functools.partial(_delta_prep_kernel, lat_blocks=rows.lat_blocks, seq_blocks=rows.s // tm,
                             ctx_blocks=rows.c // tm)
    return pl.pallas_call(
        kern,
        grid=(r // tm,),
        in_specs=[pl.BlockSpec((HALO, C_CONV), lambda i: (jnp.maximum(i * hb - 1, 0), 0)),
                  pl.BlockSpec((tm, C_CONV), lambda i: (i, 0)),
                  pl.BlockSpec((HALO, C_CONV), lambda i: (jnp.minimum((i + 1) * hb, last_halo), 0)),
                  pl.BlockSpec((tm, AB_W), lambda i: (i, 0)),
                  pl.BlockSpec((CONV_K, C_CONV), lambda i: (0, 0)),
                  _vec_spec(AB_W), _vec_spec(AB_W)],
        out_specs=[pl.BlockSpec((tm, C_CONV), lambda i: (i, 0)), pl.BlockSpec((tm, AB_W), lambda i: (i, 0))],
        out_shape=[jax.ShapeDtypeStruct((r, C_CONV), F32), jax.ShapeDtypeStruct((r, AB_W), F32)],
        compiler_params=_cparams(("parallel",)),
        name="delta_prep",
    )(proj_main, proj_main, proj_main, proj_ab, conv_w, alog, dtb)


def _delta_kernel(qkv_f_ref, gb_f_ref, qkv_r_ref, gb_r_ref, o_f_ref, o_r_ref, state_ref):
    @pl.when(pl.program_id(1) == 0)
    def _():
        state_ref[...] = jnp.zeros_like(state_ref)

    row = lax.broadcasted_iota(jnp.int32, (CHUNK, CHUNK), 0)
    col = lax.broadcasted_iota(jnp.int32, (CHUNK, CHUNK), 1)
    eye = jnp.where(row == col, 1.0, 0.0)
    rep = C_V_HEADS // C_QK_HEADS
    n_sq = int(math.log2(CHUNK)) - 1
    chains = []
    for direction, (qkv_ref, gb_ref, o_ref) in enumerate(((qkv_f_ref, gb_f_ref, o_f_ref),
                                                          (qkv_r_ref, gb_r_ref, o_r_ref))):
        if direction == 0:
            incl, strict, last = row >= col, row > col, CHUNK - 1
        else:
            incl, strict, last = row <= col, row < col, 0
        gb = gb_ref[...]
        gcum = _dot(jnp.where(incl, 1.0, 0.0), gb, precision=HIGHEST)
        gcum_t = gcum.T
        kq = []
        for qh in range(C_QK_HEADS):
            q = qkv_ref[:, qh * HEAD_DIM:(qh + 1) * HEAD_DIM]
            k = qkv_ref[:, C_QK + qh * HEAD_DIM:C_QK + (qh + 1) * HEAD_DIM]
            prod = _dot_nt(jnp.concatenate([k, q], axis=0).astype(BF16), k.astype(BF16))
            kq.append((q, k, prod[:CHUNK], prod[CHUNK:]))
        for h in range(C_V_HEADS):
            lane = direction * C_V_HEADS + h
            q, k, kk, qk = kq[h // rep]
            g_col = gcum[:, lane:lane + 1]
            beta = gb[:, 2 * C_V_HEADS + lane:2 * C_V_HEADS + lane + 1]
            decay = jnp.where(incl, jnp.exp(jnp.minimum(g_col - gcum_t[lane:lane + 1, :], 0.0)), 0.0)
            v = qkv_ref[:, 2 * C_QK + h * HEAD_DIM:2 * C_QK + (h + 1) * HEAD_DIM]
            p = jnp.where(strict, -(kk * beta) * decay, 0.0)
            chains.append(dict(
                h=h, slot=lane, o_ref=o_ref, q=q, k=k, g_col=g_col, e_g=jnp.exp(g_col),
                g_last=gcum[last:last + 1, lane:lane + 1], beta=beta, v=v, qk=qk * decay, p=p, t=eye + p))
    for _ in range(n_sq):
        for ch in chains:
            pb = ch["p"].astype(BF16)
            ch["p"] = _dot(pb, pb)
        for ch in chains:
            ch["t"] = ch["t"] + _dot(ch["t"].astype(BF16), ch["p"].astype(BF16))
    for ch in chains:
        rhs = jnp.concatenate([ch["v"] * ch["beta"], ch["k"] * (ch["beta"] * ch["e_g"])], axis=1)
        uw = _dot(ch["t"].astype(BF16), rhs.astype(BF16))
        ch["u"], ch["w"] = uw[:, :HEAD_DIM], uw[:, HEAD_DIM:]
    for ch in chains:
        ch["state"] = state_ref[ch["slot"]]
        lhs = jnp.concatenate([ch["w"], ch["q"] * ch["e_g"]], axis=0)
        ch["ws_qs"] = _dot(lhs.astype(BF16), ch["state"].astype(BF16))
    for ch in chains:
        ch["v_new"] = (ch["u"] - ch["ws_qs"][:CHUNK]).astype(BF16)
    for ch in chains:
        h = ch["h"]
        ch["o_ref"][:, h * HEAD_DIM:(h + 1) * HEAD_DIM] = ch["ws_qs"][CHUNK:] + _dot(ch["qk"].astype(BF16), ch["v_new"])
    for ch in chains:
        k_dec = ch["k"] * jnp.exp(ch["g_last"] - ch["g_col"])
        state_ref[ch["slot"]] = ch["state"] * jnp.exp(ch["g_last"]) + _dot_tn(k_dec.astype(BF16), ch["v_new"])


def _delta_scan(rows, dqkv, gb):
    b, s, c = rows.b, rows.s, rows.c
    ctx_chunks, lat_chunks = c // CHUNK, s // CHUNK
    ctx_base = rows.n_lat // CHUNK

    def fwd_map(bb, t):
        return (jnp.where(t < ctx_chunks, ctx_base + bb * ctx_chunks + t, bb * lat_chunks + t - ctx_chunks), 0)

    def rev_map(bb, t):
        ctx_i, lat_i = ctx_chunks - 1 - t, lat_chunks - 1 - (t - ctx_chunks)
        return (jnp.where(t < ctx_chunks, ctx_base + bb * ctx_chunks + ctx_i, bb * lat_chunks + lat_i), 0)

    out = jax.ShapeDtypeStruct((rows.r, C_V), F32)
    return pl.pallas_call(
        _delta_kernel,
        grid=(b, ctx_chunks + lat_chunks),
        in_specs=[pl.BlockSpec((CHUNK, C_CONV), fwd_map), pl.BlockSpec((CHUNK, AB_W), fwd_map),
                  pl.BlockSpec((CHUNK, C_CONV), rev_map), pl.BlockSpec((CHUNK, AB_W), rev_map)],
        out_specs=[pl.BlockSpec((CHUNK, C_V), fwd_map), pl.BlockSpec((CHUNK, C_V), rev_map)],
        out_shape=[out, out],
        scratch_shapes=[pltpu.VMEM((2 * C_V_HEADS, HEAD_DIM, HEAD_DIM), F32)],
        compiler_params=_cparams(("parallel", "arbitrary")),
        name="delta_scan",
    )(dqkv, gb, dqkv, gb)


def _merge_kernel(oa_ref, ob_ref, of_ref, or_ref, z_ref, nc_ref, ga_ref, gb_ref, gc_ref,
                  wa_ref, wb_ref, wc_ref, y_ref, oc_ref):
    for h in range(C_V_HEADS):
        sl = slice(h * HEAD_DIM, (h + 1) * HEAD_DIM)
        o = of_ref[:, sl] + or_ref[:, sl]
        oc_ref[:, sl] = (_rms(o, nc_ref[...]) * _silu(z_ref[:, sl])).astype(BF16)
    y = jax.nn.sigmoid(ga_ref[...]) * _dot(oa_ref[...], wa_ref[...])
    y = y + jax.nn.sigmoid(gb_ref[...]) * _dot(ob_ref[...], wb_ref[...])
    y = y + jax.nn.sigmoid(gc_ref[...]) * _dot(oc_ref[...], wc_ref[...])
    y_ref[...] = y.astype(BF16)


def _merge(n_rows, tm, o_a, o_b, o_f, o_r, proj_main, gates, norm_c, w_a, w_b, w_c):
    d = w_a.shape[1]
    z_blk = C_CONV // C_V

    def rspec(w, col=0):
        return pl.BlockSpec((tm, w), lambda i: (i, col))

    def wspec(kdim):
        return pl.BlockSpec((kdim, d), lambda i: (0, 0))

    return pl.pallas_call(
        _merge_kernel,
        grid=(n_rows // tm,),
        in_specs=[rspec(A_Q), rspec(B_Q), rspec(C_V), rspec(C_V), rspec(C_V, z_blk), _vec_spec(HEAD_DIM),
                  rspec(d, 0), rspec(d, 1), rspec(d, 2), wspec(A_Q), wspec(B_Q), wspec(C_V)],
        out_specs=rspec(d),
        out_shape=jax.ShapeDtypeStruct((n_rows, d), BF16),
        scratch_shapes=[pltpu.VMEM((tm, C_V), BF16)],
        compiler_params=_cparams(("parallel",)),
        name="branch_merge",
    )(o_a, o_b, o_f, o_r, proj_main, norm_c, gates, gates, gates, w_a, w_b, w_c)


def _pack_halves(x):
    half = x.shape[1] // 2
    lo = lax.bitcast_convert_type(x[:, :half].astype(BF16).astype(F32), jnp.uint32)
    hi = lax.bitcast_convert_type(x[:, half:].astype(BF16).astype(F32), jnp.uint32)
    return (lo >> 16) | hi


def _unpack_halves(p):
    lo = lax.bitcast_convert_type(p << 16, F32).astype(BF16)
    hi = lax.bitcast_convert_type(p & jnp.uint32(0xFFFF0000), F32).astype(BF16)
    return lo, hi


def _outproj_kernel(y_ref, w_ref, x_ref, gt_ref, g_ref, b_ref, sc_ref, sh_ref, x1_ref, h_ref, hp_ref, *, alpha):
    y = _dot(y_ref[...], w_ref[...])
    x1 = _layernorm(alpha * x_ref[...] + gt_ref[0, 0] * y) * g_ref[...] + b_ref[...]
    x1_ref[...] = x1
    h = x1 * (1.0 + sc_ref[0, 0]) + sh_ref[0, 0]
    h_ref[...] = h.astype(BF16)
    hp_ref[...] = _pack_halves(h)


def _outproj(rows, n_rows, y, w_out, x, mod_l, ln_g, ln_b, alpha):
    d = x.shape[1]
    tm = rows.tm
    rspec = pl.BlockSpec((tm, d), lambda i: (i, 0))
    return pl.pallas_call(
        functools.partial(_outproj_kernel, alpha=alpha),
        grid=(n_rows // tm,),
        in_specs=[rspec, pl.BlockSpec((d, d), lambda i: (0, 0)), rspec, _mod_spec(rows, d, 2),
                  _vec_spec(d), _vec_spec(d), _mod_spec(rows, d, 4), _mod_spec(rows, d, 3)],
        out_specs=[rspec, rspec, pl.BlockSpec((tm, d // 2), lambda i: (i, 0))],
        out_shape=[jax.ShapeDtypeStruct((n_rows, d), F32), jax.ShapeDtypeStruct((n_rows, d), BF16),
                   jax.ShapeDtypeStruct((n_rows, d // 2), jnp.uint32)],
        compiler_params=_cparams(("parallel",)),
        name="out_proj_norm",
    )(y, w_out, x, mod_l, ln_g, ln_b, mod_l, mod_l)


def _router_kernel(x_ref, sc_ref, sh_ref, wr_ref, rb_ref, idx_ref, gate_ref, rank_ref, cnt_ref, carry_ref):
    @pl.when(pl.program_id(0) == 0)
    def _():
        carry_ref[...] = jnp.zeros_like(carry_ref)

    h = x_ref[...] * (1.0 + sc_ref[0, 0]) + sh_ref[0, 0]
    scores = jax.nn.sigmoid(_dot(h, wr_ref[...], precision=HIGHEST))
    tm, n_exp = scores.shape
    sel = scores + rb_ref[...]
    lane = lax.broadcasted_iota(jnp.int32, (tm, n_exp), 1)
    slot = lax.broadcasted_iota(jnp.int32, (tm, TOP_K), 1)
    idx = jnp.zeros((tm, TOP_K), jnp.int32)
    gate = jnp.zeros((tm, TOP_K), F32)
    mask = jnp.zeros((tm, n_exp), F32)
    hits = []
    for k in range(TOP_K):
        best = jnp.max(sel, axis=-1, keepdims=True)
        choice = jnp.min(jnp.where(sel == best, lane, n_exp), axis=-1, keepdims=True)
        hit = lane == choice
        hits.append(hit)
        idx = jnp.where(slot == k, choice, idx)
        gate = jnp.where(slot == k, jnp.sum(jnp.where(hit, scores, 0.0), axis=-1, keepdims=True), gate)
        sel = jnp.where(hit, -jnp.inf, sel)
        mask = jnp.where(hit, 1.0, mask)
    idx_ref[...] = idx
    gate_ref[...] = gate / jnp.sum(gate, axis=-1, keepdims=True) * ROUTED_SCALE
    r_i = lax.broadcasted_iota(jnp.int32, (tm, tm), 0)
    c_i = lax.broadcasted_iota(jnp.int32, (tm, tm), 1)
    before = jnp.where(c_i < r_i, 1.0, 0.0).astype(BF16)
    prefix = _dot(before, mask.astype(BF16)) + carry_ref[...]
    rank = jnp.zeros((tm, TOP_K), F32)
    for k in range(TOP_K):
        rank = jnp.where(slot == k, jnp.sum(jnp.where(hits[k], prefix, 0.0), axis=-1, keepdims=True), rank)
    rank_ref[...] = rank.astype(jnp.int32)
    carry_ref[...] = carry_ref[...] + jnp.sum(mask, axis=0, keepdims=True)
    cnt_ref[...] = carry_ref[...]


def _router(rows, n_rows, x1, mod_l, w_router, router_bias):
    d, n_exp = w_router.shape
    tm = rows.tm
    kspec = pl.BlockSpec((tm, TOP_K), lambda i: (i, 0))
    return pl.pallas_call(
        _router_kernel,
        grid=(n_rows // tm,),
        in_specs=[pl.BlockSpec((tm, d), lambda i: (i, 0)), _mod_spec(rows, d, 4), _mod_spec(rows, d, 3),
                  pl.BlockSpec((d, n_exp), lambda i: (0, 0)), _vec_spec(n_exp)],
        out_specs=[kspec, kspec, kspec, _vec_spec(n_exp)],
        out_shape=[jax.ShapeDtypeStruct((n_rows, TOP_K), jnp.int32), jax.ShapeDtypeStruct((n_rows, TOP_K), F32),
                   jax.ShapeDtypeStruct((n_rows, TOP_K), jnp.int32), jax.ShapeDtypeStruct((1, n_exp), F32)],
        scratch_shapes=[pltpu.VMEM((1, n_exp), F32)],
        compiler_params=_cparams(("arbitrary",)),
        name="router",
    )(x1, mod_l, mod_l, w_router, router_bias.reshape(1, n_exp))


def _expert_kernel(be_ref, nu_ref, src_ref, dst_ref, hp_ref, w1_ref, w3_ref, w2_ref, yg_ref,
                   xbuf, obuf, xs_ref, gsem, ssem, w1b, w3b, w2b, prev_ref):
    blk = pl.program_id(0)
    n_used = nu_ref[0]
    g_slot = blk % 2
    c_slot = 1 - g_slot

    def gather(buf_slot):
        for j in range(MOE_ROWS):
            pltpu.make_async_copy(hp_ref.at[pl.ds(src_ref[0, 0, j], 1)], xbuf.at[buf_slot, pl.ds(j, 1)],
                                  gsem.at[buf_slot]).start()

    def wait_gather(buf_slot):
        pltpu.make_async_copy(hp_ref.at[pl.ds(0, MOE_ROWS)], xbuf.at[buf_slot], gsem.at[buf_slot]).wait()

    def wait_scatter(buf_slot):
        pltpu.make_async_copy(obuf.at[buf_slot], yg_ref.at[pl.ds(0, MOE_ROWS)], ssem.at[buf_slot]).wait()

    @pl.when(blk == 0)
    def _():
        prev_ref[0] = -1
        obuf[...] = jnp.zeros_like(obuf)
        n_real = yg_ref.shape[0] - 2 * MOE_ROWS
        for s in range(2):
            spare = pltpu.make_async_copy(obuf.at[s], yg_ref.at[pl.ds(n_real + s * MOE_ROWS, MOE_ROWS)], ssem.at[s])
            spare.start()
            spare.wait()

    def compute(with_gather):
        wait_gather(c_slot)
        half = xbuf.shape[-1]
        x_lo, x_hi = _unpack_halves(xbuf[c_slot])
        xs_ref[:, :half] = x_lo
        xs_ref[:, half:] = x_hi
        if with_gather:
            gather(g_slot)
        h1 = _dot(xs_ref[...], w1b[...])
        h3 = _dot(xs_ref[...], w3b[...])
        y = _dot((_silu(h1) * h3).astype(BF16), w2b[...])
        obuf[c_slot] = _pack_halves(y)
        for j in range(MOE_ROWS):
            pltpu.make_async_copy(obuf.at[c_slot, pl.ds(j, 1)], yg_ref.at[pl.ds(dst_ref[0, 0, j], 1)],
                                  ssem.at[c_slot]).start()

        @pl.when(blk >= 2)
        def _():
            wait_scatter(g_slot)

    @pl.when(jnp.logical_and(blk == 0, n_used > 0))
    def _():
        gather(g_slot)

    @pl.when(jnp.logical_and(blk >= 1, blk <= n_used))
    def _():
        e = be_ref[blk]

        @pl.when(e != prev_ref[0])
        def _():
            w1b[...] = w1_ref[0, 0].astype(BF16)
            w3b[...] = w3_ref[0, 0].astype(BF16)
            w2b[...] = w2_ref[0, 0].astype(BF16)
            prev_ref[0] = e

        @pl.when(blk < n_used)
        def _():
            compute(True)

        @pl.when(blk == n_used)
        def _():
            compute(False)
            wait_scatter(c_slot)


def _experts(hp, slot_src, slot_dst, block_expert, n_used, w1, w3, w2, layer, n_out_rows):
    half = hp.shape[1]
    d = 2 * half
    ff = w1.shape[-1]
    n_blocks = slot_src.shape[0]
    last = n_blocks - 1

    def wmap(i, be, nu):
        return (layer, be[i], 0, 0)

    grid_spec = pltpu.PrefetchScalarGridSpec(
        num_scalar_prefetch=2,
        grid=(n_blocks + 1,),
        in_specs=[pl.BlockSpec((1, 1, MOE_ROWS), lambda i, be, nu: (jnp.minimum(i, last), 0, 0),
                               memory_space=pltpu.SMEM),
                  pl.BlockSpec((1, 1, MOE_ROWS), lambda i, be, nu: (jnp.maximum(i - 1, 0), 0, 0),
                               memory_space=pltpu.SMEM),
                  pl.BlockSpec(memory_space=pl.ANY),
                  pl.BlockSpec((1, 1, d, ff), wmap), pl.BlockSpec((1, 1, d, ff), wmap),
                  pl.BlockSpec((1, 1, ff, d), wmap)],
        out_specs=pl.BlockSpec(memory_space=pl.ANY),
        scratch_shapes=[pltpu.VMEM((2, MOE_ROWS, half), jnp.uint32), pltpu.VMEM((2, MOE_ROWS, half), jnp.uint32),
                        pltpu.VMEM((MOE_ROWS, d), BF16),
                        pltpu.SemaphoreType.DMA((2,)), pltpu.SemaphoreType.DMA((2,)),
                        pltpu.VMEM((d, ff), BF16), pltpu.VMEM((d, ff), BF16), pltpu.VMEM((ff, d), BF16),
                        pltpu.SMEM((1,), jnp.int32)],
    )
    be_steps = jnp.concatenate([block_expert[:1], block_expert])
    return pl.pallas_call(
        _expert_kernel,
        grid_spec=grid_spec,
        out_shape=jax.ShapeDtypeStruct((n_out_rows, half), jnp.uint32),
        compiler_params=_cparams(("arbitrary",)),
        name="experts",
    )(be_steps, n_used, slot_src, slot_dst, hp, w1, w3, w2)


def _combine_kernel(*refs, alpha, emit_h):
    ys_refs, refs = refs[:TOP_K], refs[TOP_K:]
    (gate_ref, h_ref, ws1_ref, ws3_ref, ws2_ref, x_ref, gt_ref, g_ref, b_ref) = refs[:9]
    hidden = _silu(_dot(h_ref[...], ws1_ref[...])) * _dot(h_ref[...], ws3_ref[...])
    f = _dot(hidden.astype(BF16), ws2_ref[...])
    gate = gate_ref[...]
    routed_lo, routed_hi = None, None
    for k in range(TOP_K):
        p = ys_refs[k][...]
        lo = lax.bitcast_convert_type(p << 16, F32) * gate[:, k:k + 1]
        hi = lax.bitcast_convert_type(p & jnp.uint32(0xFFFF0000), F32) * gate[:, k:k + 1]
        routed_lo = lo if routed_lo is None else routed_lo + lo
        routed_hi = hi if routed_hi is None else routed_hi + hi
    f = f + jnp.concatenate([routed_lo, routed_hi], axis=1)
    x2 = _layernorm(alpha * x_ref[...] + gt_ref[0, 0] * f) * g_ref[...] + b_ref[...]
    if emit_h:
        sc_ref, sh_ref, x2_ref, hn_ref = refs[9:]
        x2_ref[...] = x2
        hn_ref[...] = (x2 * (1.0 + sc_ref[0, 0]) + sh_ref[0, 0]).astype(BF16)
    else:
        refs[9][...] = x2


def _combine(rows, n_rows, yg, gate, h2, ws1, ws3, ws2, x1, mod_l, ln_g, ln_b, alpha, mod_next):
    d = x1.shape[1]
    sf = ws1.shape[1]
    tm = rows.tm
    k_blocks = n_rows // tm
    rspec = pl.BlockSpec((tm, d), lambda i: (i, 0))
    in_specs = [pl.BlockSpec((tm, d // 2), lambda i, k=k: (k * k_blocks + i, 0)) for k in range(TOP_K)]
    in_specs += [pl.BlockSpec((tm, TOP_K), lambda i: (i, 0)), rspec,
                 pl.BlockSpec((d, sf), lambda i: (0, 0)), pl.BlockSpec((d, sf), lambda i: (0, 0)),
                 pl.BlockSpec((sf, d), lambda i: (0, 0)), rspec, _mod_spec(rows, d, 5), _vec_spec(d), _vec_spec(d)]
    args = [yg] * TOP_K + [gate, h2, ws1, ws3, ws2, x1, mod_l, ln_g, ln_b]
    emit_h = mod_next is not None
    if emit_h:
        in_specs += [_mod_spec(rows, d, 1), _mod_spec(rows, d, 0)]
        args += [mod_next, mod_next]
        out_specs = [rspec, rspec]
        out_shape = [jax.ShapeDtypeStruct((n_rows, d), F32), jax.ShapeDtypeStruct((n_rows, d), BF16)]
    else:
        out_specs = rspec
        out_shape = jax.ShapeDtypeStruct((n_rows, d), F32)
    return pl.pallas_call(
        functools.partial(_combine_kernel, alpha=alpha, emit_h=emit_h),
        grid=(n_rows // tm,),
        in_specs=in_specs,
        out_specs=out_specs,
        out_shape=out_shape,
        compiler_params=_cparams(("parallel",)),
        name="moe_combine_norm",
    )(*args)


def _permute_w_in(w):
    a_end = A_Q + 2 * A_KV
    b_end = a_end + B_Q + 2 * B_KV
    c_end = b_end + C_CONV + C_V
    ab_end = c_end + 4 * C_V_HEADS
    main = jnp.concatenate([w[:, b_end:c_end], w[:, :b_end]], axis=1).astype(BF16)
    ab = jnp.pad(w[:, c_end:ab_end], ((0, 0), (0, AB_W - 4 * C_V_HEADS))).astype(BF16)
    gates = w[:, ab_end:].astype(BF16)
    return main, gates, ab


def kernel(x, c, ctx, c_ctx, w_mod, b_mod, w_in, q_norm_a, k_norm_a, sink_b, conv_c, a_log_c, dt_bias_c, norm_c,
           w_br_a, w_br_b, w_br_c, w_out, ln1_g, ln1_b, w_router, router_bias, w1, w3, w2, ws1, ws3, ws2,
           ln2_g, ln2_b):
    b, s, d = x.shape
    n_ctx = ctx.shape[1]
    depth = w_mod.shape[0]
    n_exp = w_router.shape[-1]
    alpha = (2.0 * depth) ** 0.25
    rows = _Rows(b, s, n_ctx)
    tm = rows.tm

    c_all = jnp.concatenate([c, c_ctx[None, :], jnp.zeros((8 - b - 1, d), F32)], axis=0)
    mod = _modulation(c_all, w_mod, b_mod).reshape(depth, 8, N_MOD, 1, d)
    tables = _rope_tables(s, tm)

    x_all = jnp.concatenate([x.reshape(b * s, d), ctx.reshape(b * n_ctx, d)], axis=0)
    x_cur, h = _entry(rows, x_all, mod[0])

    for l in range(depth):
        need_ctx = l < depth - 1
        n_rows = rows.r if need_ctx else rows.n_lat
        mod_l = mod[l]
        w_main, w_gates, w_ab = _permute_w_in(w_in[l])
        proj_main = _matmul(h, w_main, F32)
        gates = _matmul(h, w_gates, F32)
        proj_ab = _matmul(h, w_ab, F32)

        qkv = _attn_prep(rows, proj_main, tables, q_norm_a[l].reshape(1, -1), k_norm_a[l].reshape(1, -1))
        a_k, a_v = A_Q, A_Q + A_KV
        o_a = _attn_full(qkv, batch=b, n_q=s, q_base=0, sources=[(rows.n_lat, n_ctx), (0, s)],
                         q_col=0, k_col=a_k, v_col=a_v, heads=A_HEADS, kv_heads=A_KV_HEADS)
        o_b = _attn_window(qkv, sink_b[l], rows)
        if need_ctx:
            b_q = A_Q + 2 * A_KV
            o_a_c = _attn_full(qkv, batch=b, n_q=n_ctx, q_base=rows.n_lat, sources=[(rows.n_lat, n_ctx)],
                               q_col=0, k_col=a_k, v_col=a_v, heads=A_HEADS, kv_heads=A_KV_HEADS)
            o_b_c = _attn_full(qkv, batch=b, n_q=n_ctx, q_base=rows.n_lat, sources=[(rows.n_lat, n_ctx)],
                               q_col=b_q, k_col=b_q + B_Q, v_col=b_q + B_Q + B_KV, heads=B_HEADS,
                               kv_heads=B_KV_HEADS, sink=sink_b[l])
            o_a = jnp.concatenate([o_a, o_a_c], axis=0)
            o_b = jnp.concatenate([o_b, o_b_c], axis=0)

        dqkv, gbeta = _delta_prep(rows, proj_main, proj_ab, conv_c[l], a_log_c[l], dt_bias_c[l])
        o_f, o_r = _delta_scan(rows, dqkv, gbeta)

        y = _merge(n_rows, tm, o_a, o_b, o_f, o_r, proj_main, gates, norm_c[l].reshape(1, -1),
                   w_br_a[l].astype(BF16), w_br_b[l].astype(BF16), w_br_c[l].astype(BF16))
        x1, h2, h2p = _outproj(rows, n_rows, y, w_out[l].astype(BF16), x_cur, mod_l,
                          ln1_g[l].reshape(1, -1), ln1_b[l].reshape(1, -1), alpha)

        idx, gate, rank, counts = _router(rows, n_rows, x1, mod_l, w_router[l], router_bias[l])
        counts = counts[0].astype(jnp.int32)
        padded = (counts + MOE_ROWS - 1) // MOE_ROWS * MOE_ROWS
        padded_end = jnp.cumsum(padded)
        group_start = padded_end - padded
        experts_iota = jnp.arange(n_exp, dtype=jnp.int32)
        dest = rank + jnp.sum(jnp.where(idx[:, :, None] == experts_iota, group_start, 0), axis=-1)
        n_blocks = -(-(n_rows * TOP_K) // MOE_ROWS) + n_exp
        block_start = jnp.arange(n_blocks, dtype=jnp.int32)[:, None] * MOE_ROWS
        block_expert = jnp.minimum(jnp.sum((padded_end[None, :] <= block_start).astype(jnp.int32), axis=-1), n_exp - 1)
        n_used = (padded_end[-1:] // MOE_ROWS).astype(jnp.int32)
        n_slots = n_blocks * MOE_ROWS
        code = (jnp.arange(n_rows, dtype=jnp.int32)[:, None] * TOP_K + jnp.arange(TOP_K, dtype=jnp.int32)[None, :])
        slot_code = jnp.full((n_slots,), -1, jnp.int32).at[dest.reshape(-1)].set(code.reshape(-1))
        filled = slot_code >= 0
        slot_iota = jnp.arange(n_slots, dtype=jnp.int32)
        slot_src = jnp.where(filled, slot_code // TOP_K, 0)
        slot_dst = jnp.where(filled, (slot_code % TOP_K) * n_rows + slot_code // TOP_K,
                             TOP_K * n_rows + slot_iota % (2 * MOE_ROWS))
        yg = _experts(h2p, slot_src.reshape(n_blocks, 1, MOE_ROWS), slot_dst.reshape(n_blocks, 1, MOE_ROWS),
                      block_expert, n_used, w1, w3, w2, l, TOP_K * n_rows + 2 * MOE_ROWS)

        mod_next = mod[l + 1] if need_ctx else None
        out = _combine(rows, n_rows, yg, gate, h2, ws1[l].astype(BF16), ws3[l].astype(BF16), ws2[l].astype(BF16),
                       x1, mod_l, ln2_g[l].reshape(1, -1), ln2_b[l].reshape(1, -1), alpha, mod_next)
        if need_ctx:
            x_cur, h = out
        else:
            x_cur = out
    return x_cur[:rows.n_lat].reshape(b, s, d)
```

```python
import functools
import math

import jax
import jax.numpy as jnp
from jax import lax
from jax.experimental import pallas as pl
from jax.experimental.pallas import tpu as pltpu

F32 = jnp.float32
BF16 = jnp.bfloat16
HIGHEST = lax.Precision.HIGHEST

HEAD_DIM = 128
GRID_W = 64
ROPE_THETA = 10000.0
A_HEADS, A_KV_HEADS = 8, 2
B_HEADS, B_KV_HEADS = 8, 2
WINDOW = 128
Q_BLOCK = 128
C_QK_HEADS, C_V_HEADS = 4, 8
CONV_K = 5
CHUNK = 64
TOP_K = 8
ROUTED_SCALE = 2.5
LN_EPS = 1e-5
RMS_EPS = 1e-6
MOE_ROWS = 256
N_MOD = 6
HALO = 8
VMEM_LIMIT = 56 * 2**20

A_Q, A_KV = A_HEADS * HEAD_DIM, A_KV_HEADS * HEAD_DIM
B_Q, B_KV = B_HEADS * HEAD_DIM, B_KV_HEADS * HEAD_DIM
C_QK, C_V = C_QK_HEADS * HEAD_DIM, C_V_HEADS * HEAD_DIM
C_CONV = 2 * C_QK + C_V
ATT_W = A_Q + 2 * A_KV + B_Q + 2 * B_KV
MAIN_W = C_CONV + C_V + ATT_W
AB_W = 128


def _cparams(sem):
    return pltpu.CompilerParams(dimension_semantics=sem, vmem_limit_bytes=VMEM_LIMIT)


def _pick(n, prefs):
    for p in prefs:
        if n % p == 0:
            return p
    return n


def _dot(a, b, **kw):
    return jnp.dot(a, b, preferred_element_type=F32, **kw)


def _dot_nt(a, b):
    return lax.dot_general(a, b, (((1,), (1,)), ((), ())), preferred_element_type=F32)


def _dot_tn(a, b):
    return lax.dot_general(a, b, (((0,), (0,)), ((), ())), preferred_element_type=F32)


def _silu(x):
    return x * jax.nn.sigmoid(x)


def _mod_kernel(c_ref, w_ref, b_ref, o_ref):
    o_ref[0] = _dot(_silu(c_ref[...]), w_ref[0], precision=HIGHEST) + b_ref[0]


def _modulation(c_all, w_mod, b_mod):
    depth, d, width = w_mod.shape
    tn = _pick(width, (1024, 512, 256, 128))
    return pl.pallas_call(
        _mod_kernel,
        grid=(depth, width // tn),
        in_specs=[pl.BlockSpec((8, d), lambda l, j: (0, 0)),
                  pl.BlockSpec((1, d, tn), lambda l, j: (l, 0, j)),
                  pl.BlockSpec((1, 1, tn), lambda l, j: (l, 0, j))],
        out_specs=pl.BlockSpec((1, 8, tn), lambda l, j: (l, 0, j)),
        out_shape=jax.ShapeDtypeStruct((depth, 8, width), F32),
        compiler_params=_cparams(("parallel", "parallel")),
        name="modulation",
    )(c_all, w_mod, b_mod.reshape(depth, 1, width))


class _Rows:
    def __init__(self, b, s, c):
        self.b, self.s, self.c = b, s, c
        self.tm = 256 if c % 256 == 0 else 128
        self.n_lat = b * s
        self.r = b * s + b * c
        self.lat_blocks = self.n_lat // self.tm
        self.blocks = self.r // self.tm

    def group(self, i):
        return jnp.where(i < self.lat_blocks, i // (self.s // self.tm), self.b)


def _mod_spec(rows, d, which):
    return pl.BlockSpec((1, 1, 1, d), lambda i: (rows.group(i), which, 0, 0))


def _vec_spec(d):
    return pl.BlockSpec((1, d), lambda i: (0, 0))


def _layernorm(x):
    mu = jnp.mean(x, axis=-1, keepdims=True)
    xc = x - mu
    var = jnp.mean(xc * xc, axis=-1, keepdims=True)
    return xc * lax.rsqrt(var + LN_EPS)


def _entry_kernel(x_ref, sc_ref, sh_ref, x0_ref, h_ref):
    xn = _layernorm(x_ref[...])
    x0_ref[...] = xn
    h_ref[...] = (xn * (1.0 + sc_ref[0, 0]) + sh_ref[0, 0]).astype(BF16)


def _entry(rows, x_all, mod_l):
    r, d = x_all.shape
    tm = rows.tm
    row_spec = pl.BlockSpec((tm, d), lambda i: (i, 0))
    return pl.pallas_call(
        _entry_kernel,
        grid=(r // tm,),
        in_specs=[row_spec, _mod_spec(rows, d, 1), _mod_spec(rows, d, 0)],
        out_specs=[row_spec, row_spec],
        out_shape=[jax.ShapeDtypeStruct((r, d), F32), jax.ShapeDtypeStruct((r, d), BF16)],
        compiler_params=_cparams(("parallel",)),
        name="entry_norm",
    )(x_all, mod_l, mod_l)


def _mm_kernel(a_ref, w_ref, o_ref):
    o_ref[...] = _dot(a_ref[...], w_ref[...]).astype(o_ref.dtype)


def _matmul(a, w, out_dtype):
    m, k = a.shape
    n = w.shape[1]
    tm = _pick(m, (1024, 512, 256, 128))
    tn = _pick(n, (1024, 512, 256, 128))
    return pl.pallas_call(
        _mm_kernel,
        grid=(n // tn, m // tm),
        in_specs=[pl.BlockSpec((tm, k), lambda j, i: (i, 0)),
                  pl.BlockSpec((k, tn), lambda j, i: (0, j))],
        out_specs=pl.BlockSpec((tm, tn), lambda j, i: (i, j)),
        out_shape=jax.ShapeDtypeStruct((m, n), out_dtype),
        compiler_params=_cparams(("parallel", "parallel")),
        name="in_proj",
    )(a, w)


def _rope(t, cos, sin_lo, sin_hi):
    return t * cos + pltpu.roll(t, HEAD_DIM - 32, 1) * sin_lo + pltpu.roll(t, 32, 1) * sin_hi


def _rms(t, g):
    return t * lax.rsqrt(jnp.mean(t * t, axis=-1, keepdims=True) + RMS_EPS) * g


def _attn_prep_kernel(x_ref, cos_ref, slo_ref, shi_ref, qn_ref, kn_ref, o_ref):
    cos, slo, shi = cos_ref[...], slo_ref[...], shi_ref[...]
    scale = HEAD_DIM ** -0.5

    def head(j):
        return x_ref[:, j * HEAD_DIM:(j + 1) * HEAD_DIM]

    def put(j, v):
        o_ref[:, j * HEAD_DIM:(j + 1) * HEAD_DIM] = v.astype(BF16)

    j = 0
    for _ in range(A_HEADS):
        put(j, _rope(_rms(head(j), qn_ref[...]), cos, slo, shi) * scale)
        j += 1
    for _ in range(A_KV_HEADS):
        put(j, _rope(_rms(head(j), kn_ref[...]), cos, slo, shi))
        j += 1
    for _ in range(A_KV_HEADS):
        put(j, head(j))
        j += 1
    for _ in range(B_HEADS):
        put(j, _rope(head(j), cos, slo, shi) * scale)
        j += 1
    for _ in range(B_KV_HEADS):
        put(j, _rope(head(j), cos, slo, shi))
        j += 1
    for _ in range(B_KV_HEADS):
        put(j, head(j))
        j += 1


def _attn_prep(rows, proj_main, tables, q_norm, k_norm):
    r = proj_main.shape[0]
    tm = rows.tm
    pos_blocks = rows.s // tm
    col_blk = (C_CONV + C_V) // ATT_W
    assert col_blk * ATT_W == C_CONV + C_V

    def tab_map(i):
        return (jnp.where(i < rows.lat_blocks, i % pos_blocks, pos_blocks), 0)

    tab_spec = pl.BlockSpec((tm, HEAD_DIM), tab_map)
    return pl.pallas_call(
        _attn_prep_kernel,
        grid=(r // tm,),
        in_specs=[pl.BlockSpec((tm, ATT_W), lambda i: (i, col_blk)), tab_spec, tab_spec, tab_spec,
                  _vec_spec(HEAD_DIM), _vec_spec(HEAD_DIM)],
        out_specs=pl.BlockSpec((tm, ATT_W), lambda i: (i, 0)),
        out_shape=jax.ShapeDtypeStruct((r, ATT_W), BF16),
        compiler_params=_cparams(("parallel",)),
        name="attn_prep",
    )(proj_main, *tables, q_norm, k_norm)


def _rope_tables(s, tm):
    row = jnp.repeat(jnp.arange(s // GRID_W, dtype=F32), GRID_W)
    col = jnp.tile(jnp.arange(GRID_W, dtype=F32), s // GRID_W)
    axis_dim = HEAD_DIM // 2
    inv_freq = ROPE_THETA ** (-jnp.arange(0, axis_dim, 2, dtype=F32) / axis_dim)
    ang_r = row[:, None] * inv_freq[None, :]
    ang_c = col[:, None] * inv_freq[None, :]
    ang = jnp.concatenate([ang_r, ang_r, ang_c, ang_c], axis=-1)
    cos, sin = jnp.cos(ang), jnp.sin(ang)
    lo = (jnp.arange(HEAD_DIM) % axis_dim) < (axis_dim // 2)
    sin_lo = jnp.where(lo, -sin, 0.0)
    sin_hi = jnp.where(lo, 0.0, sin)
    one, zero = jnp.ones((tm, HEAD_DIM), F32), jnp.zeros((tm, HEAD_DIM), F32)
    return (jnp.concatenate([cos, one]), jnp.concatenate([sin_lo, zero]), jnp.concatenate([sin_hi, zero]))


def _attn_full_kernel(*refs, n_src, has_sink, groups):
    if has_sink:
        sink_ref, refs = refs[0], refs[1:]
    q_ref, o_ref = refs[0], refs[1 + 2 * n_src]
    kv = refs[1:1 + 2 * n_src]
    kvh = pl.program_id(1)
    for g in range(groups):
        q = q_ref[:, g * HEAD_DIM:(g + 1) * HEAD_DIM]
        scores = [_dot_nt(q, kv[2 * j][...]) for j in range(n_src)]
        m = functools.reduce(jnp.maximum, [jnp.max(s, axis=-1, keepdims=True) for s in scores])
        if has_sink:
            sink = sink_ref[kvh * groups + g]
            m = jnp.maximum(m, sink)
        es = [jnp.exp(s - m) for s in scores]
        den = functools.reduce(jnp.add, [jnp.sum(e, axis=-1, keepdims=True) for e in es])
        if has_sink:
            den = den + jnp.exp(sink - m)
        acc = functools.reduce(jnp.add, [_dot(es[j].astype(BF16), kv[2 * j + 1][...]) for j in range(n_src)])
        o_ref[:, g * HEAD_DIM:(g + 1) * HEAD_DIM] = (acc / den).astype(o_ref.dtype)


def _attn_full(qkv, *, batch, n_q, q_base, sources, q_col, k_col, v_col, heads, kv_heads, sink=None):
    groups = heads // kv_heads
    tq = _pick(n_q, (256, 128))
    qw = groups * HEAD_DIM
    nq = n_q // tq
    in_specs = []
    args = []
    if sink is not None:
        in_specs.append(pl.BlockSpec(memory_space=pltpu.SMEM))
        args.append(sink)
    in_specs.append(pl.BlockSpec((tq, qw), lambda b, h, i: (q_base // tq + b * nq + i, q_col // qw + h)))
    args.append(qkv)
    for base, n in sources:
        for col in (k_col, v_col):
            in_specs.append(pl.BlockSpec(
                (n, HEAD_DIM), lambda b, h, i, base=base, n=n, col=col: (base // n + b, col // HEAD_DIM + h)))
            args.append(qkv)
    kern = functools.partial(_attn_full_kernel, n_src=len(sources), has_sink=sink is not None, groups=groups)
    return pl.pallas_call(
        kern,
        grid=(batch, kv_heads, nq),
        in_specs=in_specs,
        out_specs=pl.BlockSpec((tq, qw), lambda b, h, i: (b * nq + i, h)),
        out_shape=jax.ShapeDtypeStruct((batch * n_q, heads * HEAD_DIM), BF16),
        compiler_params=_cparams(("parallel", "parallel", "arbitrary")),
        name="attn_full",
    )(*args)


def _attn_win_kernel(sink_ref, q_ref, k_ref, v_ref, kc_ref, vc_ref, o_ref, *, groups, seq):
    kvh = pl.program_id(1)
    n = pl.program_id(2)
    span = 3 * Q_BLOCK
    start = jnp.clip((n - 1) * Q_BLOCK, 0, seq - span)
    start = pl.multiple_of(start, Q_BLOCK)
    kb = k_ref[pl.ds(start, span), :]
    vb = v_ref[pl.ds(start, span), :]
    q_pos = n * Q_BLOCK + lax.broadcasted_iota(jnp.int32, (Q_BLOCK, span), 0)
    k_pos = start + lax.broadcasted_iota(jnp.int32, (Q_BLOCK, span), 1)
    valid = jnp.abs(k_pos - q_pos) <= WINDOW
    for g in range(groups):
        q = q_ref[:, g * HEAD_DIM:(g + 1) * HEAD_DIM]
        s_win = jnp.where(valid, _dot_nt(q, kb), -jnp.inf)
        s_ctx = _dot_nt(q, kc_ref[...])
        sink = sink_ref[kvh * groups + g]
        m = jnp.maximum(jnp.maximum(jnp.max(s_win, axis=-1, keepdims=True),
                                    jnp.max(s_ctx, axis=-1, keepdims=True)), sink)
        e_win = jnp.exp(s_win - m)
        e_ctx = jnp.exp(s_ctx - m)
        den = (jnp.sum(e_win, axis=-1, keepdims=True) + jnp.sum(e_ctx, axis=-1, keepdims=True)
               + jnp.exp(sink - m))
        acc = _dot(e_win.astype(BF16), vb) + _dot(e_ctx.astype(BF16), vc_ref[...])
        o_ref[:, g * HEAD_DIM:(g + 1) * HEAD_DIM] = (acc / den).astype(o_ref.dtype)


def _attn_window(qkv, sink, rows):
    b, s, c = rows.b, rows.s, rows.c
    groups = B_HEADS // B_KV_HEADS
    qw = groups * HEAD_DIM
    nq = s // Q_BLOCK
    q_col = A_Q + 2 * A_KV
    k_col = q_col + B_Q
    v_col = k_col + B_KV
    ctx_base = rows.n_lat // c

    def kv_spec(n, base, col):
        return pl.BlockSpec((n, HEAD_DIM), lambda bb, h, i: (base + bb, col // HEAD_DIM + h))

    return pl.pallas_call(
        functools.partial(_attn_win_kernel, groups=groups, seq=s),
        grid=(b, B_KV_HEADS, nq),
        in_specs=[pl.BlockSpec(memory_space=pltpu.SMEM),
                  pl.BlockSpec((Q_BLOCK, qw), lambda bb, h, i: (bb * nq + i, q_col // qw + h)),
                  kv_spec(s, 0, k_col), kv_spec(s, 0, v_col),
                  kv_spec(c, ctx_base, k_col), kv_spec(c, ctx_base, v_col)],
        out_specs=pl.BlockSpec((Q_BLOCK, qw), lambda bb, h, i: (bb * nq + i, h)),
        out_shape=jax.ShapeDtypeStruct((b * s, B_Q), BF16),
        compiler_params=_cparams(("parallel", "parallel", "arbitrary")),
        name="attn_window",
    )(sink, qkv, qkv, qkv, qkv, qkv)


def _delta_prep_kernel(prev_ref, cur_ref, next_ref, ab_ref, w_ref, alog_ref, dtb_ref, qkv_ref, gb_ref,
                       *, lat_blocks, seq_blocks, ctx_blocks):
    i = pl.program_id(0)
    tm = cur_ref.shape[0]
    is_lat = i < lat_blocks
    pos = jnp.where(is_lat, i % seq_blocks, (i - lat_blocks) % ctx_blocks)
    n_pos = jnp.where(is_lat, seq_blocks, ctx_blocks)
    first = pos == 0
    last = pos == n_pos - 1
    keep_prev = jnp.where(first, 0.0, 1.0)
    keep_next = jnp.where(last, 0.0, 1.0)
    ext = tm + 2 * HALO
    n_qk = 2 * C_QK_HEADS
    for j in range(C_CONV // HEAD_DIM):
        sl = slice(j * HEAD_DIM, (j + 1) * HEAD_DIM)
        xt = jnp.concatenate([prev_ref[:, sl] * keep_prev, cur_ref[:, sl], next_ref[:, sl] * keep_next], axis=0)
        acc = None
        for tap in range(CONV_K):
            shift = (CONV_K // 2 - tap) % ext
            xs = xt if shift == 0 else pltpu.roll(xt, shift, 0)
            term = xs[HALO:HALO + tm] * w_ref[tap:tap + 1, sl]
            acc = term if acc is None else acc + term
        y = _silu(acc)
        if j < n_qk:
            y = y * lax.rsqrt(jnp.sum(y * y, axis=-1, keepdims=True) + RMS_EPS)
            if j < C_QK_HEADS:
                y = y * (HEAD_DIM ** -0.5)
        qkv_ref[:, sl] = y
    ab = ab_ref[...]
    lane = lax.broadcasted_iota(jnp.int32, ab.shape, 1)
    n_g = 2 * C_V_HEADS
    z = ab + dtb_ref[...]
    softplus = jnp.maximum(z, 0.0) + jnp.log(1.0 + jnp.exp(-jnp.abs(z)))
    g = -jnp.exp(alog_ref[...]) * softplus
    gb_ref[...] = jnp.where(lane < n_g, g, jax.nn.sigmoid(ab))


def _delta_prep(rows, proj_main, proj_ab, conv_w, a_log, dt_bias):
    r = proj_main.shape[0]
    tm = rows.tm
    hb = tm // HALO
    last_halo = r // HALO - 1
    pad = AB_W - a_log.size
    alog = jnp.pad(a_log.reshape(1, -1), ((0, 0), (0, pad)))
    dtb = jnp.pad(dt_bias.reshape(1, -1), ((0, 0), (0, pad)))
    kern = functools.partial(_delta_prep_kernel, lat_blocks=rows.lat_blocks, seq_blocks=rows.s // tm,
                             ctx_blocks=rows.c // tm)
    return pl.pallas_call(
        kern,
        grid=(r // tm,),
        in_specs=[pl.BlockSpec((HALO, C_CONV), lambda i: (jnp.maximum(i * hb - 1, 0), 0)),
                  pl.BlockSpec((tm, C_CONV), lambda i: (i, 0)),
                  pl.BlockSpec((HALO, C_CONV), lambda i: (jnp.minimum((i + 1) * hb, last_halo), 0)),
                  pl.BlockSpec((tm, AB_W), lambda i: (i, 0)),
                  pl.BlockSpec((CONV_K, C_CONV), lambda i: (0, 0)),
                  _vec_spec(AB_W), _vec_spec(AB_W)],
        out_specs=[pl.BlockSpec((tm, C_CONV), lambda i: (i, 0)), pl.BlockSpec((tm, AB_W), lambda i: (i, 0))],
        out_shape=[jax.ShapeDtypeStruct((r, C_CONV), F32), jax.ShapeDtypeStruct((r, AB_W), F32)],
        compiler_params=_cparams(("parallel",)),
        name="delta_prep",
    )(proj_main, proj_main, proj_main, proj_ab, conv_w, alog, dtb)


def _delta_kernel(qkv_f_ref, gb_f_ref, qkv_r_ref, gb_r_ref, o_f_ref, o_r_ref, state_ref):
    @pl.when(pl.program_id(1) == 0)
    def _():
        state_ref[...] = jnp.zeros_like(state_ref)

    row = lax.broadcasted_iota(jnp.int32, (CHUNK, CHUNK), 0)
    col = lax.broadcasted_iota(jnp.int32, (CHUNK, CHUNK), 1)
    eye = jnp.where(row == col, 1.0, 0.0)
    rep = C_V_HEADS // C_QK_HEADS
    n_sq = int(math.log2(CHUNK)) - 1
    chains = []
    for direction, (qkv_ref, gb_ref, o_ref) in enumerate(((qkv_f_ref, gb_f_ref, o_f_ref),
                                                          (qkv_r_ref, gb_r_ref, o_r_ref))):
        if direction == 0:
            incl, strict, last = row >= col, row > col, CHUNK - 1
        else:
            incl, strict, last = row <= col, row < col, 0
        gb = gb_ref[...]
        gcum = _dot(jnp.where(incl, 1.0, 0.0), gb, precision=HIGHEST)
        gcum_t = gcum.T
        kq = []
        for qh in range(C_QK_HEADS):
            q = qkv_ref[:, qh * HEAD_DIM:(qh + 1) * HEAD_DIM]
            k = qkv_ref[:, C_QK + qh * HEAD_DIM:C_QK + (qh + 1) * HEAD_DIM]
            prod = _dot_nt(jnp.concatenate([k, q], axis=0).astype(BF16), k.astype(BF16))
            kq.append((q, k, prod[:CHUNK], prod[CHUNK:]))
        for h in range(C_V_HEADS):
            lane = direction * C_V_HEADS + h
            q, k, kk, qk = kq[h // rep]
            g_col = gcum[:, lane:lane + 1]
            beta = gb[:, 2 * C_V_HEADS + lane:2 * C_V_HEADS + lane + 1]
            decay = jnp.where(incl, jnp.exp(jnp.minimum(g_col - gcum_t[lane:lane + 1, :], 0.0)), 0.0)
            v = qkv_ref[:, 2 * C_QK + h * HEAD_DIM:2 * C_QK + (h + 1) * HEAD_DIM]
            p = jnp.where(strict, -(kk * beta) * decay, 0.0)
            chains.append(dict(
                h=h, slot=lane, o_ref=o_ref, q=q, k=k, g_col=g_col, e_g=jnp.exp(g_col),
                g_last=gcum[last:last + 1, lane:lane + 1], beta=beta, v=v, qk=qk * decay, p=p, t=eye + p))
    for _ in range(n_sq):
        for ch in chains:
            pb = ch["p"].astype(BF16)
            ch["p"] = _dot(pb, pb)
        for ch in chains:
            ch["t"] = ch["t"] + _dot(ch["t"].astype(BF16), ch["p"].astype(BF16))
    for ch in chains:
        rhs = jnp.concatenate([ch["v"] * ch["beta"], ch["k"] * (ch["beta"] * ch["e_g"])], axis=1)
        uw = _dot(ch["t"].astype(BF16), rhs.astype(BF16))
        ch["u"], ch["w"] = uw[:, :HEAD_DIM], uw[:, HEAD_DIM:]
    for ch in chains:
        ch["state"] = state_ref[ch["slot"]]
        lhs = jnp.concatenate([ch["w"], ch["q"] * ch["e_g"]], axis=0)
        ch["ws_qs"] = _dot(lhs.astype(BF16), ch["state"].astype(BF16))
    for ch in chains:
        ch["v_new"] = (ch["u"] - ch["ws_qs"][:CHUNK]).astype(BF16)
    for ch in chains:
        h = ch["h"]
        ch["o_ref"][:, h * HEAD_DIM:(h + 1) * HEAD_DIM] = ch["ws_qs"][CHUNK:] + _dot(ch["qk"].astype(BF16), ch["v_new"])
    for ch in chains:
        k_dec = ch["k"] * jnp.exp(ch["g_last"] - ch["g_col"])
        state_ref[ch["slot"]] = ch["state"] * jnp.exp(ch["g_last"]) + _dot_tn(k_dec.astype(BF16), ch["v_new"])


def _delta_scan(rows, dqkv, gb):
    b, s, c = rows.b, rows.s, rows.c
    ctx_chunks, lat_chunks = c // CHUNK, s // CHUNK
    ctx_base = rows.n_lat // CHUNK

    def fwd_map(bb, t):
        return (jnp.where(t < ctx_chunks, ctx_base + bb * ctx_chunks + t, bb * lat_chunks + t - ctx_chunks), 0)

    def rev_map(bb, t):
        ctx_i, lat_i = ctx_chunks - 1 - t, lat_chunks - 1 - (t - ctx_chunks)
        return (jnp.where(t < ctx_chunks, ctx_base + bb * ctx_chunks + ctx_i, bb * lat_chunks + lat_i), 0)

    out = jax.ShapeDtypeStruct((rows.r, C_V), F32)
    return pl.pallas_call(
        _delta_kernel,
        grid=(b, ctx_chunks + lat_chunks),
        in_specs=[pl.BlockSpec((CHUNK, C_CONV), fwd_map), pl.BlockSpec((CHUNK, AB_W), fwd_map),
                  pl.BlockSpec((CHUNK, C_CONV), rev_map), pl.BlockSpec((CHUNK, AB_W), rev_map)],
        out_specs=[pl.BlockSpec((CHUNK, C_V), fwd_map), pl.BlockSpec((CHUNK, C_V), rev_map)],
        out_shape=[out, out],
        scratch_shapes=[pltpu.VMEM((2 * C_V_HEADS, HEAD_DIM, HEAD_DIM), F32)],
        compiler_params=_cparams(("parallel", "arbitrary")),
        name="delta_scan",
    )(dqkv, gb, dqkv, gb)


def _merge_kernel(oa_ref, ob_ref, of_ref, or_ref, z_ref, nc_ref, ga_ref, gb_ref, gc_ref,
                  wa_ref, wb_ref, wc_ref, y_ref, oc_ref):
    for h in range(C_V_HEADS):
        sl = slice(h * HEAD_DIM, (h + 1) * HEAD_DIM)
        o = of_ref[:, sl] + or_ref[:, sl]
        oc_ref[:, sl] = (_rms(o, nc_ref[...]) * _silu(z_ref[:, sl])).astype(BF16)
    y = jax.nn.sigmoid(ga_ref[...]) * _dot(oa_ref[...], wa_ref[...])
    y = y + jax.nn.sigmoid(gb_ref[...]) * _dot(ob_ref[...], wb_ref[...])
    y = y + jax.nn.sigmoid(gc_ref[...]) * _dot(oc_ref[...], wc_ref[...])
    y_ref[...] = y.astype(BF16)


def _merge(n_rows, tm, o_a, o_b, o_f, o_r, proj_main, gates, norm_c, w_a, w_b, w_c):
    d = w_a.shape[1]
    z_blk = C_CONV // C_V

    def rspec(w, col=0):
        return pl.BlockSpec((tm, w), lambda i: (i, col))

    def wspec(kdim):
        return pl.BlockSpec((kdim, d), lambda i: (0, 0))

    return pl.pallas_call(
        _merge_kernel,
        grid=(n_rows // tm,),
        in_specs=[rspec(A_Q), rspec(B_Q), rspec(C_V), rspec(C_V), rspec(C_V, z_blk), _vec_spec(HEAD_DIM),
                  rspec(d, 0), rspec(d, 1), rspec(d, 2), wspec(A_Q), wspec(B_Q), wspec(C_V)],
        out_specs=rspec(d),
        out_shape=jax.ShapeDtypeStruct((n_rows, d), BF16),
        scratch_shapes=[pltpu.VMEM((tm, C_V), BF16)],
        compiler_params=_cparams(("parallel",)),
        name="branch_merge",
    )(o_a, o_b, o_f, o_r, proj_main, norm_c, gates, gates, gates, w_a, w_b, w_c)


LANES = 128


def _token_rows(ref_rows, tokens, s, per_token):
    return pl.ds(s, tokens, stride=per_token) if per_token > 1 else pl.ds(0, tokens)


def _store_packed(ref, lead, x):
    m, d = x.shape
    half = d // 2
    per_token = half // LANES
    for s in range(per_token):
        lo = lax.bitcast_convert_type(x[:, s * LANES:(s + 1) * LANES].astype(BF16).astype(F32), jnp.uint32)
        hi = lax.bitcast_convert_type(x[:, half + s * LANES:half + (s + 1) * LANES].astype(BF16).astype(F32),
                                      jnp.uint32)
        ref[(*lead, _token_rows(ref, m, s, per_token), slice(None))] = (lo >> 16) | hi


def _load_packed(ref, lead, m, d):
    per_token = d // 2 // LANES
    out = []
    for s in range(per_token):
        p = ref[(*lead, _token_rows(ref, m, s, per_token), slice(None))]
        out.append((lax.bitcast_convert_type(p << 16, F32),
                    lax.bitcast_convert_type(p & jnp.uint32(0xFFFF0000), F32)))
    return out


def _outproj_kernel(y_ref, w_ref, x_ref, gt_ref, g_ref, b_ref, sc_ref, sh_ref, x1_ref, h_ref, hp_ref, *, alpha):
    y = _dot(y_ref[...], w_ref[...])
    x1 = _layernorm(alpha * x_ref[...] + gt_ref[0, 0] * y) * g_ref[...] + b_ref[...]
    x1_ref[...] = x1
    h = x1 * (1.0 + sc_ref[0, 0]) + sh_ref[0, 0]
    h_ref[...] = h.astype(BF16)
    _store_packed(hp_ref, (), h)


def _outproj(rows, n_rows, y, w_out, x, mod_l, ln_g, ln_b, alpha):
    d = x.shape[1]
    tm = rows.tm
    per_token = d // 2 // LANES
    rspec = pl.BlockSpec((tm, d), lambda i: (i, 0))
    return pl.pallas_call(
        functools.partial(_outproj_kernel, alpha=alpha),
        grid=(n_rows // tm,),
        in_specs=[rspec, pl.BlockSpec((d, d), lambda i: (0, 0)), rspec, _mod_spec(rows, d, 2),
                  _vec_spec(d), _vec_spec(d), _mod_spec(rows, d, 4), _mod_spec(rows, d, 3)],
        out_specs=[rspec, rspec, pl.BlockSpec((tm * per_token, LANES), lambda i: (i, 0))],
        out_shape=[jax.ShapeDtypeStruct((n_rows, d), F32), jax.ShapeDtypeStruct((n_rows, d), BF16),
                   jax.ShapeDtypeStruct((n_rows * per_token, LANES), jnp.uint32)],
        compiler_params=_cparams(("parallel",)),
        name="out_proj_norm",
    )(y, w_out, x, mod_l, ln_g, ln_b, mod_l, mod_l)


def _router_kernel(x_ref, sc_ref, sh_ref, wr_ref, rb_ref, idx_ref, gate_ref, rank_ref, cnt_ref, carry_ref):
    @pl.when(pl.program_id(0) == 0)
    def _():
        carry_ref[...] = jnp.zeros_like(carry_ref)

    h = x_ref[...] * (1.0 + sc_ref[0, 0]) + sh_ref[0, 0]
    scores = jax.nn.sigmoid(_dot(h, wr_ref[...], precision=HIGHEST))
    tm, n_exp = scores.shape
    sel = scores + rb_ref[...]
    lane = lax.broadcasted_iota(jnp.int32, (tm, n_exp), 1)
    slot = lax.broadcasted_iota(jnp.int32, (tm, TOP_K), 1)
    idx = jnp.zeros((tm, TOP_K), jnp.int32)
    gate = jnp.zeros((tm, TOP_K), F32)
    mask = jnp.zeros((tm, n_exp), F32)
    hits = []
    for k in range(TOP_K):
        best = jnp.max(sel, axis=-1, keepdims=True)
        choice = jnp.min(jnp.where(sel == best, lane, n_exp), axis=-1, keepdims=True)
        hit = lane == choice
        hits.append(hit)
        idx = jnp.where(slot == k, choice, idx)
        gate = jnp.where(slot == k, jnp.sum(jnp.where(hit, scores, 0.0), axis=-1, keepdims=True), gate)
        sel = jnp.where(hit, -jnp.inf, sel)
        mask = jnp.where(hit, 1.0, mask)
    idx_ref[...] = idx
    gate_ref[...] = gate / jnp.sum(gate, axis=-1, keepdims=True) * ROUTED_SCALE
    r_i = lax.broadcasted_iota(jnp.int32, (tm, tm), 0)
    c_i = lax.broadcasted_iota(jnp.int32, (tm, tm), 1)
    before = jnp.where(c_i < r_i, 1.0, 0.0).astype(BF16)
    prefix = _dot(before, mask.astype(BF16)) + carry_ref[...]
    rank = jnp.zeros((tm, TOP_K), F32)
    for k in range(TOP_K):
        rank = jnp.where(slot == k, jnp.sum(jnp.where(hits[k], prefix, 0.0), axis=-1, keepdims=True), rank)
    rank_ref[...] = rank.astype(jnp.int32)
    carry_ref[...] = carry_ref[...] + jnp.sum(mask, axis=0, keepdims=True)
    cnt_ref[...] = carry_ref[...]


def _router(rows, n_rows, x1, mod_l, w_router, router_bias):
    d, n_exp = w_router.shape
    tm = rows.tm
    kspec = pl.BlockSpec((tm, TOP_K), lambda i: (i, 0))
    return pl.pallas_call(
        _router_kernel,
        grid=(n_rows // tm,),
        in_specs=[pl.BlockSpec((tm, d), lambda i: (i, 0)), _mod_spec(rows, d, 4), _mod_spec(rows, d, 3),
                  pl.BlockSpec((d, n_exp), lambda i: (0, 0)), _vec_spec(n_exp)],
        out_specs=[kspec, kspec, kspec, _vec_spec(n_exp)],
        out_shape=[jax.ShapeDtypeStruct((n_rows, TOP_K), jnp.int32), jax.ShapeDtypeStruct((n_rows, TOP_K), F32),
                   jax.ShapeDtypeStruct((n_rows, TOP_K), jnp.int32), jax.ShapeDtypeStruct((1, n_exp), F32)],
        scratch_shapes=[pltpu.VMEM((1, n_exp), F32)],
        compiler_params=_cparams(("arbitrary",)),
        name="router",
    )(x1, mod_l, mod_l, w_router, router_bias.reshape(1, n_exp))


def _expert_kernel(be_ref, nu_ref, src_ref, dst_ref, hp_ref, w1_ref, w3_ref, w2_ref, yg_ref,
                   xbuf, obuf, xs_ref, gsem, ssem, w1b, w3b, w2b, prev_ref, *, per_token):
    blk = pl.program_id(0)
    n_used = nu_ref[0]
    g_slot = blk % 2
    c_slot = 1 - g_slot
    block_rows = MOE_ROWS * per_token
    d = xs_ref.shape[1]

    def token_rows(start):
        return pl.ds(pl.multiple_of(start, per_token), per_token)

    def gather(buf_slot):
        for j in range(MOE_ROWS):
            pltpu.make_async_copy(hp_ref.at[token_rows(src_ref[0, 0, j])],
                                  xbuf.at[buf_slot, pl.ds(j * per_token, per_token)], gsem.at[buf_slot]).start()

    def wait_gather(buf_slot):
        pltpu.make_async_copy(hp_ref.at[pl.ds(0, block_rows)], xbuf.at[buf_slot], gsem.at[buf_slot]).wait()

    def wait_scatter(buf_slot):
        pltpu.make_async_copy(obuf.at[buf_slot], yg_ref.at[pl.ds(0, block_rows)], ssem.at[buf_slot]).wait()

    @pl.when(blk == 0)
    def _():
        prev_ref[0] = -1
        obuf[...] = jnp.zeros_like(obuf)
        n_real = yg_ref.shape[0] - 2 * block_rows
        for s in range(2):
            spare = pltpu.make_async_copy(obuf.at[s], yg_ref.at[pl.ds(n_real + s * block_rows, block_rows)], ssem.at[s])
            spare.start()
            spare.wait()

    def compute(with_gather):
        wait_gather(c_slot)
        for s, (lo, hi) in enumerate(_load_packed(xbuf, (c_slot,), MOE_ROWS, d)):
            xs_ref[:, s * LANES:(s + 1) * LANES] = lo.astype(BF16)
            xs_ref[:, d // 2 + s * LANES:d // 2 + (s + 1) * LANES] = hi.astype(BF16)
        if with_gather:
            gather(g_slot)
        h1 = _dot(xs_ref[...], w1b[...])
        h3 = _dot(xs_ref[...], w3b[...])
        y = _dot((_silu(h1) * h3).astype(BF16), w2b[...])
        _store_packed(obuf, (c_slot,), y)
        for j in range(MOE_ROWS):
            pltpu.make_async_copy(obuf.at[c_slot, pl.ds(j * per_token, per_token)],
                                  yg_ref.at[token_rows(dst_ref[0, 0, j])], ssem.at[c_slot]).start()

        @pl.when(blk >= 2)
        def _():
            wait_scatter(g_slot)

    @pl.when(jnp.logical_and(blk == 0, n_used > 0))
    def _():
        gather(g_slot)

    @pl.when(jnp.logical_and(blk >= 1, blk <= n_used))
    def _():
        e = be_ref[blk]

        @pl.when(e != prev_ref[0])
        def _():
            w1b[...] = w1_ref[0, 0].astype(BF16)
            w3b[...] = w3_ref[0, 0].astype(BF16)
            w2b[...] = w2_ref[0, 0].astype(BF16)
            prev_ref[0] = e

        @pl.when(blk < n_used)
        def _():
            compute(True)

        @pl.when(blk == n_used)
        def _():
            compute(False)
            wait_scatter(c_slot)


def _experts(hp, slot_src, slot_dst, block_expert, n_used, w1, w3, w2, layer, n_out_tokens):
    d, ff = w1.shape[-2:]
    per_token = d // 2 // LANES
    n_blocks = slot_src.shape[0]
    last = n_blocks - 1

    def wmap(i, be, nu):
        return (layer, be[i], 0, 0)

    buf = pltpu.VMEM((2, MOE_ROWS * per_token, LANES), jnp.uint32)
    grid_spec = pltpu.PrefetchScalarGridSpec(
        num_scalar_prefetch=2,
        grid=(n_blocks + 1,),
        in_specs=[pl.BlockSpec((1, 1, MOE_ROWS), lambda i, be, nu: (jnp.minimum(i, last), 0, 0),
                               memory_space=pltpu.SMEM),
                  pl.BlockSpec((1, 1, MOE_ROWS), lambda i, be, nu: (jnp.maximum(i - 1, 0), 0, 0),
                               memory_space=pltpu.SMEM),
                  pl.BlockSpec(memory_space=pl.ANY),
                  pl.BlockSpec((1, 1, d, ff), wmap), pl.BlockSpec((1, 1, d, ff), wmap),
                  pl.BlockSpec((1, 1, ff, d), wmap)],
        out_specs=pl.BlockSpec(memory_space=pl.ANY),
        scratch_shapes=[buf, buf, pltpu.VMEM((MOE_ROWS, d), BF16),
                        pltpu.SemaphoreType.DMA((2,)), pltpu.SemaphoreType.DMA((2,)),
                        pltpu.VMEM((d, ff), BF16), pltpu.VMEM((d, ff), BF16), pltpu.VMEM((ff, d), BF16),
                        pltpu.SMEM((1,), jnp.int32)],
    )
    be_steps = jnp.concatenate([block_expert[:1], block_expert])
    return pl.pallas_call(
        functools.partial(_expert_kernel, per_token=per_token),
        grid_spec=grid_spec,
        out_shape=jax.ShapeDtypeStruct((n_out_tokens * per_token, LANES), jnp.uint32),
        compiler_params=_cparams(("arbitrary",)),
        name="experts",
    )(be_steps, n_used, slot_src * per_token, slot_dst * per_token, hp, w1, w3, w2)


def _combine_kernel(*refs, alpha, emit_h):
    ys_refs, refs = refs[:TOP_K], refs[TOP_K:]
    (gate_ref, h_ref, ws1_ref, ws3_ref, ws2_ref, x_ref, gt_ref, g_ref, b_ref) = refs[:9]
    hidden = _silu(_dot(h_ref[...], ws1_ref[...])) * _dot(h_ref[...], ws3_ref[...])
    f = _dot(hidden.astype(BF16), ws2_ref[...])
    gate = gate_ref[...]
    tm, d = f.shape
    routed = None
    for k in range(TOP_K):
        chunks = _load_packed(ys_refs[k], (), tm, d)
        term = jnp.concatenate([lo for lo, _ in chunks] + [hi for _, hi in chunks], axis=1) * gate[:, k:k + 1]
        routed = term if routed is None else routed + term
    f = f + routed
    x2 = _layernorm(alpha * x_ref[...] + gt_ref[0, 0] * f) * g_ref[...] + b_ref[...]
    if emit_h:
        sc_ref, sh_ref, x2_ref, hn_ref = refs[9:]
        x2_ref[...] = x2
        hn_ref[...] = (x2 * (1.0 + sc_ref[0, 0]) + sh_ref[0, 0]).astype(BF16)
    else:
        refs[9][...] = x2


def _combine(rows, n_rows, yg, gate, h2, ws1, ws3, ws2, x1, mod_l, ln_g, ln_b, alpha, mod_next):
    d = x1.shape[1]
    sf = ws1.shape[1]
    tm = rows.tm
    k_blocks = n_rows // tm
    rspec = pl.BlockSpec((tm, d), lambda i: (i, 0))
    per_token = d // 2 // LANES
    in_specs = [pl.BlockSpec((tm * per_token, LANES), lambda i, k=k: (k * k_blocks + i, 0)) for k in range(TOP_K)]
    in_specs += [pl.BlockSpec((tm, TOP_K), lambda i: (i, 0)), rspec,
                 pl.BlockSpec((d, sf), lambda i: (0, 0)), pl.BlockSpec((d, sf), lambda i: (0, 0)),
                 pl.BlockSpec((sf, d), lambda i: (0, 0)), rspec, _mod_spec(rows, d, 5), _vec_spec(d), _vec_spec(d)]
    args = [yg] * TOP_K + [gate, h2, ws1, ws3, ws2, x1, mod_l, ln_g, ln_b]
    emit_h = mod_next is not None
    if emit_h:
        in_specs += [_mod_spec(rows, d, 1), _mod_spec(rows, d, 0)]
        args += [mod_next, mod_next]
        out_specs = [rspec, rspec]
        out_shape = [jax.ShapeDtypeStruct((n_rows, d), F32), jax.ShapeDtypeStruct((n_rows, d), BF16)]
    else:
        out_specs = rspec
        out_shape = jax.ShapeDtypeStruct((n_rows, d), F32)
    return pl.pallas_call(
        functools.partial(_combine_kernel, alpha=alpha, emit_h=emit_h),
        grid=(n_rows // tm,),
        in_specs=in_specs,
        out_specs=out_specs,
        out_shape=out_shape,
        compiler_params=_cparams(("parallel",)),
        name="moe_combine_norm",
    )(*args)


def _permute_w_in(w):
    a_end = A_Q + 2 * A_KV
    b_end = a_end + B_Q + 2 * B_KV
    c_end = b_end + C_CONV + C_V
    ab_end = c_end + 4 * C_V_HEADS
    main = jnp.concatenate([w[:, b_end:c_end], w[:, :b_end]], axis=1).astype(BF16)
    ab = jnp.pad(w[:, c_end:ab_end], ((0, 0), (0, AB_W - 4 * C_V_HEADS))).astype(BF16)
    gates = w[:, ab_end:].astype(BF16)
    return main, gates, ab


def kernel(x, c, ctx, c_ctx, w_mod, b_mod, w_in, q_norm_a, k_norm_a, sink_b, conv_c, a_log_c, dt_bias_c, norm_c,
           w_br_a, w_br_b, w_br_c, w_out, ln1_g, ln1_b, w_router, router_bias, w1, w3, w2, ws1, ws3, ws2,
           ln2_g, ln2_b):
    b, s, d = x.shape
    n_ctx = ctx.shape[1]
    depth = w_mod.shape[0]
    n_exp = w_router.shape[-1]
    alpha = (2.0 * depth) ** 0.25
    rows = _Rows(b, s, n_ctx)
    tm = rows.tm

    c_all = jnp.concatenate([c, c_ctx[None, :], jnp.zeros((8 - b - 1, d), F32)], axis=0)
    mod = _modulation(c_all, w_mod, b_mod).reshape(depth, 8, N_MOD, 1, d)
    tables = _rope_tables(s, tm)

    x_all = jnp.concatenate([x.reshape(b * s, d), ctx.reshape(b * n_ctx, d)], axis=0)
    x_cur, h = _entry(rows, x_all, mod[0])

    for l in range(depth):
        need_ctx = l < depth - 1
        n_rows = rows.r if need_ctx else rows.n_lat
        mod_l = mod[l]
        w_main, w_gates, w_ab = _permute_w_in(w_in[l])
        proj_main = _matmul(h, w_main, F32)
        gates = _matmul(h, w_gates, F32)
        proj_ab = _matmul(h, w_ab, F32)

        qkv = _attn_prep(rows, proj_main, tables, q_norm_a[l].reshape(1, -1), k_norm_a[l].reshape(1, -1))
        a_k, a_v = A_Q, A_Q + A_KV
        o_a = _attn_full(qkv, batch=b, n_q=s, q_base=0, sources=[(rows.n_lat, n_ctx), (0, s)],
                         q_col=0, k_col=a_k, v_col=a_v, heads=A_HEADS, kv_heads=A_KV_HEADS)
        o_b = _attn_window(qkv, sink_b[l], rows)
        if need_ctx:
            b_q = A_Q + 2 * A_KV
            o_a_c = _attn_full(qkv, batch=b, n_q=n_ctx, q_base=rows.n_lat, sources=[(rows.n_lat, n_ctx)],
                               q_col=0, k_col=a_k, v_col=a_v, heads=A_HEADS, kv_heads=A_KV_HEADS)
            o_b_c = _attn_full(qkv, batch=b, n_q=n_ctx, q_base=rows.n_lat, sources=[(rows.n_lat, n_ctx)],
                               q_col=b_q, k_col=b_q + B_Q, v_col=b_q + B_Q + B_KV, heads=B_HEADS,
                               kv_heads=B_KV_HEADS, sink=sink_b[l])
            o_a = jnp.concatenate([o_a, o_a_c], axis=0)
            o_b = jnp.concatenate([o_b, o_b_c], axis=0)

        dqkv, gbeta = _delta_prep(rows, proj_main, proj_ab, conv_c[l], a_log_c[l], dt_bias_c[l])
        o_f, o_r = _delta_scan(rows, dqkv, gbeta)

        y = _merge(n_rows, tm, o_a, o_b, o_f, o_r, proj_main, gates, norm_c[l].reshape(1, -1),
                   w_br_a[l].astype(BF16), w_br_b[l].astype(BF16), w_br_c[l].astype(BF16))
        x1, h2, h2p = _outproj(rows, n_rows, y, w_out[l].astype(BF16), x_cur, mod_l,
                          ln1_g[l].reshape(1, -1), ln1_b[l].reshape(1, -1), alpha)

        idx, gate, rank, counts = _router(rows, n_rows, x1, mod_l, w_router[l], router_bias[l])
        counts = counts[0].astype(jnp.int32)
        padded = (counts + MOE_ROWS - 1) // MOE_ROWS * MOE_ROWS
        padded_end = jnp.cumsum(padded)
        group_start = padded_end - padded
        experts_iota = jnp.arange(n_exp, dtype=jnp.int32)
        dest = rank + jnp.sum(jnp.where(idx[:, :, None] == experts_iota, group_start, 0), axis=-1)
        n_blocks = -(-(n_rows * TOP_K) // MOE_ROWS) + n_exp
        block_start = jnp.arange(n_blocks, dtype=jnp.int32)[:, None] * MOE_ROWS
        block_expert = jnp.minimum(jnp.sum((padded_end[None, :] <= block_start).astype(jnp.int32), axis=-1), n_exp - 1)
        n_used = (padded_end[-1:] // MOE_ROWS).astype(jnp.int32)
        n_slots = n_blocks * MOE_ROWS
        code = (jnp.arange(n_rows, dtype=jnp.int32)[:, None] * TOP_K + jnp.arange(TOP_K, dtype=jnp.int32)[None, :])
        slot_code = jnp.full((n_slots,), -1, jnp.int32).at[dest.reshape(-1)].set(
            code.reshape(-1), unique_indices=True, mode="promise_in_bounds")
        filled = slot_code >= 0
        slot_iota = jnp.arange(n_slots, dtype=jnp.int32)
        slot_src = jnp.where(filled, slot_code // TOP_K, 0)
        slot_dst = jnp.where(filled, (slot_code % TOP_K) * n_rows + slot_code // TOP_K,
                             TOP_K * n_rows + slot_iota % (2 * MOE_ROWS))
        yg = _experts(h2p, slot_src.reshape(n_blocks, 1, MOE_ROWS), slot_dst.reshape(n_blocks, 1, MOE_ROWS),
                      block_expert, n_used, w1, w3, w2, l, TOP_K * n_rows + 2 * MOE_ROWS)

        mod_next = mod[l + 1] if need_ctx else None
        out = _combine(rows, n_rows, yg, gate, h2, ws1[l].astype(BF16), ws3[l].astype(BF16), ws2[l].astype(BF16),
                       x1, mod_l, ln2_g[l].reshape(1, -1), ln2_b[l].reshape(1, -1), alpha, mod_next)
        if need_ctx:
            x_cur, h = out
        else:
            x_cur = out
    return x_cur[:rows.n_lat].reshape(b, s, d)
```

```python
import functools
import math

import jax
import jax.numpy as jnp
from jax import lax
from jax.experimental import pallas as pl
from jax.experimental.pallas import tpu as pltpu

F32 = jnp.float32
BF16 = jnp.bfloat16
HIGHEST = lax.Precision.HIGHEST

HEAD_DIM = 128
GRID_W = 64
ROPE_THETA = 10000.0
A_HEADS, A_KV_HEADS = 8, 2
B_HEADS, B_KV_HEADS = 8, 2
WINDOW = 128
Q_BLOCK = 128
C_QK_HEADS, C_V_HEADS = 4, 8
CONV_K = 5
CHUNK = 64
TOP_K = 8
ROUTED_SCALE = 2.5
LN_EPS = 1e-5
RMS_EPS = 1e-6
MOE_ROWS = 256
DMA_BURSTS = 8
N_MOD = 6
HALO = 8
VMEM_LIMIT = 56 * 2**20

A_Q, A_KV = A_HEADS * HEAD_DIM, A_KV_HEADS * HEAD_DIM
B_Q, B_KV = B_HEADS * HEAD_DIM, B_KV_HEADS * HEAD_DIM
C_QK, C_V = C_QK_HEADS * HEAD_DIM, C_V_HEADS * HEAD_DIM
C_CONV = 2 * C_QK + C_V
ATT_W = A_Q + 2 * A_KV + B_Q + 2 * B_KV
MAIN_W = C_CONV + C_V + ATT_W
AB_W = 128


def _cparams(sem):
    return pltpu.CompilerParams(dimension_semantics=sem, vmem_limit_bytes=VMEM_LIMIT)


def _pick(n, prefs):
    for p in prefs:
        if n % p == 0:
            return p
    return n


def _dot(a, b, **kw):
    return jnp.dot(a, b, preferred_element_type=F32, **kw)


def _dot_nt(a, b):
    return lax.dot_general(a, b, (((1,), (1,)), ((), ())), preferred_element_type=F32)


def _dot_tn(a, b):
    return lax.dot_general(a, b, (((0,), (0,)), ((), ())), preferred_element_type=F32)


def _silu(x):
    return x * jax.nn.sigmoid(x)


def _mod_kernel(c_ref, w_ref, b_ref, o_ref):
    o_ref[0] = _dot(_silu(c_ref[...]), w_ref[0], precision=HIGHEST) + b_ref[0]


def _modulation(c_all, w_mod, b_mod):
    depth, d, width = w_mod.shape
    tn = _pick(width, (1024, 512, 256, 128))
    return pl.pallas_call(
        _mod_kernel,
        grid=(depth, width // tn),
        in_specs=[pl.BlockSpec((8, d), lambda l, j: (0, 0)),
                  pl.BlockSpec((1, d, tn), lambda l, j: (l, 0, j)),
                  pl.BlockSpec((1, 1, tn), lambda l, j: (l, 0, j))],
        out_specs=pl.BlockSpec((1, 8, tn), lambda l, j: (l, 0, j)),
        out_shape=jax.ShapeDtypeStruct((depth, 8, width), F32),
        compiler_params=_cparams(("parallel", "parallel")),
        name="modulation",
    )(c_all, w_mod, b_mod.reshape(depth, 1, width))


class _Rows:
    def __init__(self, b, s, c):
        self.b, self.s, self.c = b, s, c
        self.tm = 256 if c % 256 == 0 else 128
        self.n_lat = b * s
        self.r = b * s + b * c
        self.lat_blocks = self.n_lat // self.tm
        self.blocks = self.r // self.tm

    def group(self, i):
        return jnp.where(i < self.lat_blocks, i // (self.s // self.tm), self.b)


def _mod_spec(rows, d, which):
    return pl.BlockSpec((1, 1, 1, d), lambda i: (rows.group(i), which, 0, 0))


def _vec_spec(d):
    return pl.BlockSpec((1, d), lambda i: (0, 0))


def _layernorm(x):
    mu = jnp.mean(x, axis=-1, keepdims=True)
    xc = x - mu
    var = jnp.mean(xc * xc, axis=-1, keepdims=True)
    return xc * lax.rsqrt(var + LN_EPS)


def _entry_kernel(x_ref, sc_ref, sh_ref, x0_ref, h_ref):
    xn = _layernorm(x_ref[...])
    x0_ref[...] = xn
    h_ref[...] = (xn * (1.0 + sc_ref[0, 0]) + sh_ref[0, 0]).astype(BF16)


def _entry(rows, x_all, mod_l):
    r, d = x_all.shape
    tm = rows.tm
    row_spec = pl.BlockSpec((tm, d), lambda i: (i, 0))
    return pl.pallas_call(
        _entry_kernel,
        grid=(r // tm,),
        in_specs=[row_spec, _mod_spec(rows, d, 1), _mod_spec(rows, d, 0)],
        out_specs=[row_spec, row_spec],
        out_shape=[jax.ShapeDtypeStruct((r, d), F32), jax.ShapeDtypeStruct((r, d), BF16)],
        compiler_params=_cparams(("parallel",)),
        name="entry_norm",
    )(x_all, mod_l, mod_l)


def _mm_kernel(a_ref, w_ref, o_ref):
    o_ref[...] = _dot(a_ref[...], w_ref[...]).astype(o_ref.dtype)


def _matmul(a, w, out_dtype):
    m, k = a.shape
    n = w.shape[1]
    tm = _pick(m, (1024, 512, 256, 128))
    tn = _pick(n, (1024, 512, 256, 128))
    return pl.pallas_call(
        _mm_kernel,
        grid=(n // tn, m // tm),
        in_specs=[pl.BlockSpec((tm, k), lambda j, i: (i, 0)),
                  pl.BlockSpec((k, tn), lambda j, i: (0, j))],
        out_specs=pl.BlockSpec((tm, tn), lambda j, i: (i, j)),
        out_shape=jax.ShapeDtypeStruct((m, n), out_dtype),
        compiler_params=_cparams(("parallel", "parallel")),
        name="in_proj",
    )(a, w)


def _rope(t, cos, sin_lo, sin_hi):
    return t * cos + pltpu.roll(t, HEAD_DIM - 32, 1) * sin_lo + pltpu.roll(t, 32, 1) * sin_hi


def _rms(t, g):
    return t * lax.rsqrt(jnp.mean(t * t, axis=-1, keepdims=True) + RMS_EPS) * g


def _attn_prep_kernel(x_ref, cos_ref, slo_ref, shi_ref, qn_ref, kn_ref, o_ref):
    cos, slo, shi = cos_ref[...], slo_ref[...], shi_ref[...]
    scale = HEAD_DIM ** -0.5

    def head(j):
        return x_ref[:, j * HEAD_DIM:(j + 1) * HEAD_DIM]

    def put(j, v):
        o_ref[:, j * HEAD_DIM:(j + 1) * HEAD_DIM] = v.astype(BF16)

    j = 0
    for _ in range(A_HEADS):
        put(j, _rope(_rms(head(j), qn_ref[...]), cos, slo, shi) * scale)
        j += 1
    for _ in range(A_KV_HEADS):
        put(j, _rope(_rms(head(j), kn_ref[...]), cos, slo, shi))
        j += 1
    for _ in range(A_KV_HEADS):
        put(j, head(j))
        j += 1
    for _ in range(B_HEADS):
        put(j, _rope(head(j), cos, slo, shi) * scale)
        j += 1
    for _ in range(B_KV_HEADS):
        put(j, _rope(head(j), cos, slo, shi))
        j += 1
    for _ in range(B_KV_HEADS):
        put(j, head(j))
        j += 1


def _attn_prep(rows, proj_main, tables, q_norm, k_norm):
    r = proj_main.shape[0]
    tm = rows.tm
    pos_blocks = rows.s // tm
    col_blk = (C_CONV + C_V) // ATT_W
    assert col_blk * ATT_W == C_CONV + C_V

    def tab_map(i):
        return (jnp.where(i < rows.lat_blocks, i % pos_blocks, pos_blocks), 0)

    tab_spec = pl.BlockSpec((tm, HEAD_DIM), tab_map)
    return pl.pallas_call(
        _attn_prep_kernel,
        grid=(r // tm,),
        in_specs=[pl.BlockSpec((tm, ATT_W), lambda i: (i, col_blk)), tab_spec, tab_spec, tab_spec,
                  _vec_spec(HEAD_DIM), _vec_spec(HEAD_DIM)],
        out_specs=pl.BlockSpec((tm, ATT_W), lambda i: (i, 0)),
        out_shape=jax.ShapeDtypeStruct((r, ATT_W), BF16),
        compiler_params=_cparams(("parallel",)),
        name="attn_prep",
    )(proj_main, *tables, q_norm, k_norm)


def _rope_tables(s, tm):
    row = jnp.repeat(jnp.arange(s // GRID_W, dtype=F32), GRID_W)
    col = jnp.tile(jnp.arange(GRID_W, dtype=F32), s // GRID_W)
    axis_dim = HEAD_DIM // 2
    inv_freq = ROPE_THETA ** (-jnp.arange(0, axis_dim, 2, dtype=F32) / axis_dim)
    ang_r = row[:, None] * inv_freq[None, :]
    ang_c = col[:, None] * inv_freq[None, :]
    ang = jnp.concatenate([ang_r, ang_r, ang_c, ang_c], axis=-1)
    cos, sin = jnp.cos(ang), jnp.sin(ang)
    lo = (jnp.arange(HEAD_DIM) % axis_dim) < (axis_dim // 2)
    sin_lo = jnp.where(lo, -sin, 0.0)
    sin_hi = jnp.where(lo, 0.0, sin)
    one, zero = jnp.ones((tm, HEAD_DIM), F32), jnp.zeros((tm, HEAD_DIM), F32)
    return (jnp.concatenate([cos, one]), jnp.concatenate([sin_lo, zero]), jnp.concatenate([sin_hi, zero]))


def _attn_full_kernel(*refs, n_src, has_sink, groups):
    if has_sink:
        sink_ref, refs = refs[0], refs[1:]
    q_ref, o_ref = refs[0], refs[1 + 2 * n_src]
    kv = refs[1:1 + 2 * n_src]
    kvh = pl.program_id(1)
    for g in range(groups):
        q = q_ref[:, g * HEAD_DIM:(g + 1) * HEAD_DIM]
        scores = [_dot_nt(q, kv[2 * j][...]) for j in range(n_src)]
        m = functools.reduce(jnp.maximum, [jnp.max(s, axis=-1, keepdims=True) for s in scores])
        if has_sink:
            sink = sink_ref[kvh * groups + g]
            m = jnp.maximum(m, sink)
        es = [jnp.exp(s - m) for s in scores]
        den = functools.reduce(jnp.add, [jnp.sum(e, axis=-1, keepdims=True) for e in es])
        if has_sink:
            den = den + jnp.exp(sink - m)
        acc = functools.reduce(jnp.add, [_dot(es[j].astype(BF16), kv[2 * j + 1][...]) for j in range(n_src)])
        o_ref[:, g * HEAD_DIM:(g + 1) * HEAD_DIM] = (acc / den).astype(o_ref.dtype)


def _attn_full(qkv, *, batch, n_q, q_base, sources, q_col, k_col, v_col, heads, kv_heads, sink=None):
    groups = heads // kv_heads
    tq = _pick(n_q, (256, 128))
    qw = groups * HEAD_DIM
    nq = n_q // tq
    in_specs = []
    args = []
    if sink is not None:
        in_specs.append(pl.BlockSpec(memory_space=pltpu.SMEM))
        args.append(sink)
    in_specs.append(pl.BlockSpec((tq, qw), lambda b, h, i: (q_base // tq + b * nq + i, q_col // qw + h)))
    args.append(qkv)
    for base, n in sources:
        for col in (k_col, v_col):
            in_specs.append(pl.BlockSpec(
                (n, HEAD_DIM), lambda b, h, i, base=base, n=n, col=col: (base // n + b, col // HEAD_DIM + h)))
            args.append(qkv)
    kern = functools.partial(_attn_full_kernel, n_src=len(sources), has_sink=sink is not None, groups=groups)
    return pl.pallas_call(
        kern,
        grid=(batch, kv_heads, nq),
        in_specs=in_specs,
        out_specs=pl.BlockSpec((tq, qw), lambda b, h, i: (b * nq + i, h)),
        out_shape=jax.ShapeDtypeStruct((batch * n_q, heads * HEAD_DIM), BF16),
        compiler_params=_cparams(("parallel", "parallel", "arbitrary")),
        name="attn_full",
    )(*args)


def _attn_win_kernel(sink_ref, q_ref, k_ref, v_ref, kc_ref, vc_ref, o_ref, *, groups, seq):
    kvh = pl.program_id(1)
    n = pl.program_id(2)
    span = 3 * Q_BLOCK
    start = jnp.clip((n - 1) * Q_BLOCK, 0, seq - span)
    start = pl.multiple_of(start, Q_BLOCK)
    kb = k_ref[pl.ds(start, span), :]
    vb = v_ref[pl.ds(start, span), :]
    q_pos = n * Q_BLOCK + lax.broadcasted_iota(jnp.int32, (Q_BLOCK, span), 0)
    k_pos = start + lax.broadcasted_iota(jnp.int32, (Q_BLOCK, span), 1)
    valid = jnp.abs(k_pos - q_pos) <= WINDOW
    for g in range(groups):
        q = q_ref[:, g * HEAD_DIM:(g + 1) * HEAD_DIM]
        s_win = jnp.where(valid, _dot_nt(q, kb), -jnp.inf)
        s_ctx = _dot_nt(q, kc_ref[...])
        sink = sink_ref[kvh * groups + g]
        m = jnp.maximum(jnp.maximum(jnp.max(s_win, axis=-1, keepdims=True),
                                    jnp.max(s_ctx, axis=-1, keepdims=True)), sink)
        e_win = jnp.exp(s_win - m)
        e_ctx = jnp.exp(s_ctx - m)
        den = (jnp.sum(e_win, axis=-1, keepdims=True) + jnp.sum(e_ctx, axis=-1, keepdims=True)
               + jnp.exp(sink - m))
        acc = _dot(e_win.astype(BF16), vb) + _dot(e_ctx.astype(BF16), vc_ref[...])
        o_ref[:, g * HEAD_DIM:(g + 1) * HEAD_DIM] = (acc / den).astype(o_ref.dtype)


def _attn_window(qkv, sink, rows):
    b, s, c = rows.b, rows.s, rows.c
    groups = B_HEADS // B_KV_HEADS
    qw = groups * HEAD_DIM
    nq = s // Q_BLOCK
    q_col = A_Q + 2 * A_KV
    k_col = q_col + B_Q
    v_col = k_col + B_KV
    ctx_base = rows.n_lat // c

    def kv_spec(n, base, col):
        return pl.BlockSpec((n, HEAD_DIM), lambda bb, h, i: (base + bb, col // HEAD_DIM + h))

    return pl.pallas_call(
        functools.partial(_attn_win_kernel, groups=groups, seq=s),
        grid=(b, B_KV_HEADS, nq),
        in_specs=[pl.BlockSpec(memory_space=pltpu.SMEM),
                  pl.BlockSpec((Q_BLOCK, qw), lambda bb, h, i: (bb * nq + i, q_col // qw + h)),
                  kv_spec(s, 0, k_col), kv_spec(s, 0, v_col),
                  kv_spec(c, ctx_base, k_col), kv_spec(c, ctx_base, v_col)],
        out_specs=pl.BlockSpec((Q_BLOCK, qw), lambda bb, h, i: (bb * nq + i, h)),
        out_shape=jax.ShapeDtypeStruct((b * s, B_Q), BF16),
        compiler_params=_cparams(("parallel", "parallel", "arbitrary")),
        name="attn_window",
    )(sink, qkv, qkv, qkv, qkv, qkv)


def _delta_prep_kernel(prev_ref, cur_ref, next_ref, ab_ref, w_ref, alog_ref, dtb_ref, qkv_ref, gb_ref,
                       *, lat_blocks, seq_blocks, ctx_blocks):
    i = pl.program_id(0)
    tm = cur_ref.shape[0]
    is_lat = i < lat_blocks
    pos = jnp.where(is_lat, i % seq_blocks, (i - lat_blocks) % ctx_blocks)
    n_pos = jnp.where(is_lat, seq_blocks, ctx_blocks)
    first = pos == 0
    last = pos == n_pos - 1
    keep_prev = jnp.where(first, 0.0, 1.0)
    keep_next = jnp.where(last, 0.0, 1.0)
    ext = tm + 2 * HALO
    n_qk = 2 * C_QK_HEADS
    for j in range(C_CONV // HEAD_DIM):
        sl = slice(j * HEAD_DIM, (j + 1) * HEAD_DIM)
        xt = jnp.concatenate([prev_ref[:, sl] * keep_prev, cur_ref[:, sl], next_ref[:, sl] * keep_next], axis=0)
        acc = None
        for tap in range(CONV_K):
            shift = (CONV_K // 2 - tap) % ext
            xs = xt if shift == 0 else pltpu.roll(xt, shift, 0)
            term = xs[HALO:HALO + tm] * w_ref[tap:tap + 1, sl]
            acc = term if acc is None else acc + term
        y = _silu(acc)
        if j < n_qk:
            y = y * lax.rsqrt(jnp.sum(y * y, axis=-1, keepdims=True) + RMS_EPS)
            if j < C_QK_HEADS:
                y = y * (HEAD_DIM ** -0.5)
        qkv_ref[:, sl] = y
    ab = ab_ref[...]
    lane = lax.broadcasted_iota(jnp.int32, ab.shape, 1)
    n_g = 2 * C_V_HEADS
    z = ab + dtb_ref[...]
    softplus = jnp.maximum(z, 0.0) + jnp.log(1.0 + jnp.exp(-jnp.abs(z)))
    g = -jnp.exp(alog_ref[...]) * softplus
    gb_ref[...] = jnp.where(lane < n_g, g, jax.nn.sigmoid(ab))


def _delta_prep(rows, proj_main, proj_ab, conv_w, a_log, dt_bias):
    r = proj_main.shape[0]
    tm = rows.tm
    hb = tm // HALO
    last_halo = r // HALO - 1
    pad = AB_W - a_log.size
    alog = jnp.pad(a_log.reshape(1, -1), ((0, 0), (0, pad)))
    dtb = jnp.pad(dt_bias.reshape(1, -1), ((0, 0), (0, pad)))
    kern = functools.partial(_delta_prep_kernel, lat_blocks=rows.lat_blocks, seq_blocks=rows.s // tm,
                             ctx_blocks=rows.c // tm)
    return pl.pallas_call(
        kern,
        grid=(r // tm,),
        in_specs=[pl.BlockSpec((HALO, C_CONV), lambda i: (jnp.maximum(i * hb - 1, 0), 0)),
                  pl.BlockSpec((tm, C_CONV), lambda i: (i, 0)),
                  pl.BlockSpec((HALO, C_CONV), lambda i: (jnp.minimum((i + 1) * hb, last_halo), 0)),
                  pl.BlockSpec((tm, AB_W), lambda i: (i, 0)),
                  pl.BlockSpec((CONV_K, C_CONV), lambda i: (0, 0)),
                  _vec_spec(AB_W), _vec_spec(AB_W)],
        out_specs=[pl.BlockSpec((tm, C_CONV), lambda i: (i, 0)), pl.BlockSpec((tm, AB_W), lambda i: (i, 0))],
        out_shape=[jax.ShapeDtypeStruct((r, C_CONV), F32), jax.ShapeDtypeStruct((r, AB_W), F32)],
        compiler_params=_cparams(("parallel",)),
        name="delta_prep",
    )(proj_main, proj_main, proj_main, proj_ab, conv_w, alog, dtb)


def _delta_kernel(qkv_f_ref, gb_f_ref, qkv_r_ref, gb_r_ref, o_f_ref, o_r_ref, state_ref):
    @pl.when(pl.program_id(1) == 0)
    def _():
        state_ref[...] = jnp.zeros_like(state_ref)

    row = lax.broadcasted_iota(jnp.int32, (CHUNK, CHUNK), 0)
    col = lax.broadcasted_iota(jnp.int32, (CHUNK, CHUNK), 1)
    eye = jnp.where(row == col, 1.0, 0.0)
    rep = C_V_HEADS // C_QK_HEADS
    n_sq = int(math.log2(CHUNK)) - 1
    chains = []
    for direction, (qkv_ref, gb_ref, o_ref) in enumerate(((qkv_f_ref, gb_f_ref, o_f_ref),
                                                          (qkv_r_ref, gb_r_ref, o_r_ref))):
        if direction == 0:
            incl, strict, last = row >= col, row > col, CHUNK - 1
        else:
            incl, strict, last = row <= col, row < col, 0
        gb = gb_ref[...]
        gcum = _dot(jnp.where(incl, 1.0, 0.0), gb, precision=HIGHEST)
        gcum_t = gcum.T
        kq = []
        for qh in range(C_QK_HEADS):
            q = qkv_ref[:, qh * HEAD_DIM:(qh + 1) * HEAD_DIM]
            k = qkv_ref[:, C_QK + qh * HEAD_DIM:C_QK + (qh + 1) * HEAD_DIM]
            prod = _dot_nt(jnp.concatenate([k, q], axis=0).astype(BF16), k.astype(BF16))
            kq.append((q, k, prod[:CHUNK], prod[CHUNK:]))
        for h in range(C_V_HEADS):
            lane = direction * C_V_HEADS + h
            q, k, kk, qk = kq[h // rep]
            g_col = gcum[:, lane:lane + 1]
            beta = gb[:, 2 * C_V_HEADS + lane:2 * C_V_HEADS + lane + 1]
            decay = jnp.where(incl, jnp.exp(jnp.minimum(g_col - gcum_t[lane:lane + 1, :], 0.0)), 0.0)
            v = qkv_ref[:, 2 * C_QK + h * HEAD_DIM:2 * C_QK + (h + 1) * HEAD_DIM]
            p = jnp.where(strict, -(kk * beta) * decay, 0.0)
            chains.append(dict(
                h=h, slot=lane, o_ref=o_ref, q=q, k=k, g_col=g_col, e_g=jnp.exp(g_col),
                g_last=gcum[last:last + 1, lane:lane + 1], beta=beta, v=v, qk=qk * decay, p=p, t=eye + p))
    for _ in range(n_sq):
        for ch in chains:
            pb = ch["p"].astype(BF16)
            ch["p"] = _dot(pb, pb)
        for ch in chains:
            ch["t"] = ch["t"] + _dot(ch["t"].astype(BF16), ch["p"].astype(BF16))
    for ch in chains:
        rhs = jnp.concatenate([ch["v"] * ch["beta"], ch["k"] * (ch["beta"] * ch["e_g"])], axis=1)
        uw = _dot(ch["t"].astype(BF16), rhs.astype(BF16))
        ch["u"], ch["w"] = uw[:, :HEAD_DIM], uw[:, HEAD_DIM:]
    for ch in chains:
        ch["state"] = state_ref[ch["slot"]]
        lhs = jnp.concatenate([ch["w"], ch["q"] * ch["e_g"]], axis=0)
        ch["ws_qs"] = _dot(lhs.astype(BF16), ch["state"].astype(BF16))
    for ch in chains:
        ch["v_new"] = (ch["u"] - ch["ws_qs"][:CHUNK]).astype(BF16)
    for ch in chains:
        h = ch["h"]
        ch["o_ref"][:, h * HEAD_DIM:(h + 1) * HEAD_DIM] = ch["ws_qs"][CHUNK:] + _dot(ch["qk"].astype(BF16), ch["v_new"])
    for ch in chains:
        k_dec = ch["k"] * jnp.exp(ch["g_last"] - ch["g_col"])
        state_ref[ch["slot"]] = ch["state"] * jnp.exp(ch["g_last"]) + _dot_tn(k_dec.astype(BF16), ch["v_new"])


def _delta_scan(rows, dqkv, gb):
    b, s, c = rows.b, rows.s, rows.c
    ctx_chunks, lat_chunks = c // CHUNK, s // CHUNK
    ctx_base = rows.n_lat // CHUNK

    def fwd_map(bb, t):
        return (jnp.where(t < ctx_chunks, ctx_base + bb * ctx_chunks + t, bb * lat_chunks + t - ctx_chunks), 0)

    def rev_map(bb, t):
        ctx_i, lat_i = ctx_chunks - 1 - t, lat_chunks - 1 - (t - ctx_chunks)
        return (jnp.where(t < ctx_chunks, ctx_base + bb * ctx_chunks + ctx_i, bb * lat_chunks + lat_i), 0)

    out = jax.ShapeDtypeStruct((rows.r, C_V), F32)
    return pl.pallas_call(
        _delta_kernel,
        grid=(b, ctx_chunks + lat_chunks),
        in_specs=[pl.BlockSpec((CHUNK, C_CONV), fwd_map), pl.BlockSpec((CHUNK, AB_W), fwd_map),
                  pl.BlockSpec((CHUNK, C_CONV), rev_map), pl.BlockSpec((CHUNK, AB_W), rev_map)],
        out_specs=[pl.BlockSpec((CHUNK, C_V), fwd_map), pl.BlockSpec((CHUNK, C_V), rev_map)],
        out_shape=[out, out],
        scratch_shapes=[pltpu.VMEM((2 * C_V_HEADS, HEAD_DIM, HEAD_DIM), F32)],
        compiler_params=_cparams(("parallel", "arbitrary")),
        name="delta_scan",
    )(dqkv, gb, dqkv, gb)


def _merge_kernel(oa_ref, ob_ref, of_ref, or_ref, z_ref, nc_ref, ga_ref, gb_ref, gc_ref,
                  wa_ref, wb_ref, wc_ref, y_ref, oc_ref):
    for h in range(C_V_HEADS):
        sl = slice(h * HEAD_DIM, (h + 1) * HEAD_DIM)
        o = of_ref[:, sl] + or_ref[:, sl]
        oc_ref[:, sl] = (_rms(o, nc_ref[...]) * _silu(z_ref[:, sl])).astype(BF16)
    y = jax.nn.sigmoid(ga_ref[...]) * _dot(oa_ref[...], wa_ref[...])
    y = y + jax.nn.sigmoid(gb_ref[...]) * _dot(ob_ref[...], wb_ref[...])
    y = y + jax.nn.sigmoid(gc_ref[...]) * _dot(oc_ref[...], wc_ref[...])
    y_ref[...] = y.astype(BF16)


def _merge(n_rows, tm, o_a, o_b, o_f, o_r, proj_main, gates, norm_c, w_a, w_b, w_c):
    d = w_a.shape[1]
    z_blk = C_CONV // C_V

    def rspec(w, col=0):
        return pl.BlockSpec((tm, w), lambda i: (i, col))

    def wspec(kdim):
        return pl.BlockSpec((kdim, d), lambda i: (0, 0))

    return pl.pallas_call(
        _merge_kernel,
        grid=(n_rows // tm,),
        in_specs=[rspec(A_Q), rspec(B_Q), rspec(C_V), rspec(C_V), rspec(C_V, z_blk), _vec_spec(HEAD_DIM),
                  rspec(d, 0), rspec(d, 1), rspec(d, 2), wspec(A_Q), wspec(B_Q), wspec(C_V)],
        out_specs=rspec(d),
        out_shape=jax.ShapeDtypeStruct((n_rows, d), BF16),
        scratch_shapes=[pltpu.VMEM((tm, C_V), BF16)],
        compiler_params=_cparams(("parallel",)),
        name="branch_merge",
    )(o_a, o_b, o_f, o_r, proj_main, norm_c, gates, gates, gates, w_a, w_b, w_c)


LANES = 128


def _token_rows(ref_rows, tokens, s, per_token):
    return pl.ds(s, tokens, stride=per_token) if per_token > 1 else pl.ds(0, tokens)


def _store_packed(ref, lead, x):
    m, d = x.shape
    half = d // 2
    per_token = half // LANES
    for s in range(per_token):
        lo = lax.bitcast_convert_type(x[:, s * LANES:(s + 1) * LANES].astype(BF16).astype(F32), jnp.uint32)
        hi = lax.bitcast_convert_type(x[:, half + s * LANES:half + (s + 1) * LANES].astype(BF16).astype(F32),
                                      jnp.uint32)
        ref[(*lead, _token_rows(ref, m, s, per_token), slice(None))] = (lo >> 16) | hi


def _load_packed(ref, lead, m, d):
    per_token = d // 2 // LANES
    out = []
    for s in range(per_token):
        p = ref[(*lead, _token_rows(ref, m, s, per_token), slice(None))]
        out.append((lax.bitcast_convert_type(p << 16, F32),
                    lax.bitcast_convert_type(p & jnp.uint32(0xFFFF0000), F32)))
    return out


def _outproj_kernel(y_ref, w_ref, x_ref, gt_ref, g_ref, b_ref, sc_ref, sh_ref, x1_ref, h_ref, hp_ref, *, alpha):
    y = _dot(y_ref[...], w_ref[...])
    x1 = _layernorm(alpha * x_ref[...] + gt_ref[0, 0] * y) * g_ref[...] + b_ref[...]
    x1_ref[...] = x1
    h = x1 * (1.0 + sc_ref[0, 0]) + sh_ref[0, 0]
    h_ref[...] = h.astype(BF16)
    _store_packed(hp_ref, (), h)


def _outproj(rows, n_rows, y, w_out, x, mod_l, ln_g, ln_b, alpha):
    d = x.shape[1]
    tm = rows.tm
    per_token = d // 2 // LANES
    rspec = pl.BlockSpec((tm, d), lambda i: (i, 0))
    return pl.pallas_call(
        functools.partial(_outproj_kernel, alpha=alpha),
        grid=(n_rows // tm,),
        in_specs=[rspec, pl.BlockSpec((d, d), lambda i: (0, 0)), rspec, _mod_spec(rows, d, 2),
                  _vec_spec(d), _vec_spec(d), _mod_spec(rows, d, 4), _mod_spec(rows, d, 3)],
        out_specs=[rspec, rspec, pl.BlockSpec((tm * per_token, LANES), lambda i: (i, 0))],
        out_shape=[jax.ShapeDtypeStruct((n_rows, d), F32), jax.ShapeDtypeStruct((n_rows, d), BF16),
                   jax.ShapeDtypeStruct((n_rows * per_token, LANES), jnp.uint32)],
        compiler_params=_cparams(("parallel",)),
        name="out_proj_norm",
    )(y, w_out, x, mod_l, ln_g, ln_b, mod_l, mod_l)


def _router_kernel(x_ref, sc_ref, sh_ref, wr_ref, rb_ref, idx_ref, gate_ref, rank_ref, cnt_ref, carry_ref):
    @pl.when(pl.program_id(0) == 0)
    def _():
        carry_ref[...] = jnp.zeros_like(carry_ref)

    h = x_ref[...] * (1.0 + sc_ref[0, 0]) + sh_ref[0, 0]
    scores = jax.nn.sigmoid(_dot(h, wr_ref[...], precision=HIGHEST))
    tm, n_exp = scores.shape
    sel = scores + rb_ref[...]
    lane = lax.broadcasted_iota(jnp.int32, (tm, n_exp), 1)
    slot = lax.broadcasted_iota(jnp.int32, (tm, TOP_K), 1)
    idx = jnp.zeros((tm, TOP_K), jnp.int32)
    gate = jnp.zeros((tm, TOP_K), F32)
    mask = jnp.zeros((tm, n_exp), F32)
    hits = []
    for k in range(TOP_K):
        best = jnp.max(sel, axis=-1, keepdims=True)
        choice = jnp.min(jnp.where(sel == best, lane, n_exp), axis=-1, keepdims=True)
        hit = lane == choice
        hits.append(hit)
        idx = jnp.where(slot == k, choice, idx)
        gate = jnp.where(slot == k, jnp.sum(jnp.where(hit, scores, 0.0), axis=-1, keepdims=True), gate)
        sel = jnp.where(hit, -jnp.inf, sel)
        mask = jnp.where(hit, 1.0, mask)
    idx_ref[...] = idx
    gate_ref[...] = gate / jnp.sum(gate, axis=-1, keepdims=True) * ROUTED_SCALE
    r_i = lax.broadcasted_iota(jnp.int32, (tm, tm), 0)
    c_i = lax.broadcasted_iota(jnp.int32, (tm, tm), 1)
    before = jnp.where(c_i < r_i, 1.0, 0.0).astype(BF16)
    prefix = _dot(before, mask.astype(BF16)) + carry_ref[...]
    rank = jnp.zeros((tm, TOP_K), F32)
    for k in range(TOP_K):
        rank = jnp.where(slot == k, jnp.sum(jnp.where(hits[k], prefix, 0.0), axis=-1, keepdims=True), rank)
    rank_ref[...] = rank.astype(jnp.int32)
    carry_ref[...] = carry_ref[...] + jnp.sum(mask, axis=0, keepdims=True)
    cnt_ref[...] = carry_ref[...]


def _router(rows, n_rows, x1, mod_l, w_router, router_bias):
    d, n_exp = w_router.shape
    tm = rows.tm
    kspec = pl.BlockSpec((tm, TOP_K), lambda i: (i, 0))
    return pl.pallas_call(
        _router_kernel,
        grid=(n_rows // tm,),
        in_specs=[pl.BlockSpec((tm, d), lambda i: (i, 0)), _mod_spec(rows, d, 4), _mod_spec(rows, d, 3),
                  pl.BlockSpec((d, n_exp), lambda i: (0, 0)), _vec_spec(n_exp)],
        out_specs=[kspec, kspec, kspec, _vec_spec(n_exp)],
        out_shape=[jax.ShapeDtypeStruct((n_rows, TOP_K), jnp.int32), jax.ShapeDtypeStruct((n_rows, TOP_K), F32),
                   jax.ShapeDtypeStruct((n_rows, TOP_K), jnp.int32), jax.ShapeDtypeStruct((1, n_exp), F32)],
        scratch_shapes=[pltpu.VMEM((1, n_exp), F32)],
        compiler_params=_cparams(("arbitrary",)),
        name="router",
    )(x1, mod_l, mod_l, w_router, router_bias.reshape(1, n_exp))


def _expert_kernel(be_ref, nu_ref, src_ref, dst_ref, hp_ref, w1_ref, w3_ref, w2_ref, yg_ref,
                   xbuf, obuf, xs_ref, h1_ref, act_ref, gsem, ssem, w1b, w3b, w2b, prev_ref, *, per_token):
    blk = pl.program_id(0)
    n_used = nu_ref[0]
    cur = blk % 2
    oth = 1 - cur
    block_rows = MOE_ROWS * per_token
    d = xs_ref.shape[1]
    ff = h1_ref.shape[1]
    gather_on = blk < n_used
    compute_on = jnp.logical_and(blk >= 1, blk <= n_used)
    scatter_on = jnp.logical_and(blk >= 2, blk <= n_used + 1)
    scatter_wait_on = jnp.logical_and(blk >= 3, blk <= n_used + 2)

    def token_rows(start):
        return pl.ds(pl.multiple_of(start, per_token), per_token)

    def wait_gather(buf_slot):
        pltpu.make_async_copy(hp_ref.at[pl.ds(0, block_rows)], xbuf.at[buf_slot], gsem.at[buf_slot]).wait()

    def wait_scatter(buf_slot):
        pltpu.make_async_copy(obuf.at[buf_slot], yg_ref.at[pl.ds(0, block_rows)], ssem.at[buf_slot]).wait()

    def dma_burst(batch):
        rows = range(batch * MOE_ROWS // DMA_BURSTS, (batch + 1) * MOE_ROWS // DMA_BURSTS)

        @pl.when(gather_on)
        def _():
            for j in rows:
                pltpu.make_async_copy(hp_ref.at[token_rows(src_ref[0, 0, j])],
                                      xbuf.at[cur, pl.ds(j * per_token, per_token)], gsem.at[cur]).start()

        @pl.when(scatter_on)
        def _():
            for j in rows:
                pltpu.make_async_copy(obuf.at[cur, pl.ds(j * per_token, per_token)],
                                      yg_ref.at[token_rows(dst_ref[0, 0, j])], ssem.at[cur]).start()

    @pl.when(blk == 0)
    def _():
        prev_ref[0] = -1
        obuf[...] = jnp.zeros_like(obuf)
        n_real = yg_ref.shape[0] - 2 * block_rows
        for s in range(2):
            spare = pltpu.make_async_copy(obuf.at[s], yg_ref.at[pl.ds(n_real + s * block_rows, block_rows)], ssem.at[s])
            spare.start()
            spare.wait()

    def unpack_and_h1(cols):
        e = be_ref[blk]

        @pl.when(e != prev_ref[0])
        def _():
            w1b[...] = w1_ref[0, 0].astype(BF16)
            w3b[...] = w3_ref[0, 0].astype(BF16)
            w2b[...] = w2_ref[0, 0].astype(BF16)
            prev_ref[0] = e

        wait_gather(oth)
        for s, (lo, hi) in enumerate(_load_packed(xbuf, (oth,), MOE_ROWS, d)):
            xs_ref[:, s * LANES:(s + 1) * LANES] = lo.astype(BF16)
            xs_ref[:, d // 2 + s * LANES:d // 2 + (s + 1) * LANES] = hi.astype(BF16)
        h1_ref[:, cols] = _dot(xs_ref[...], w1b[:, cols])

    def h1_piece(cols):
        h1_ref[:, cols] = _dot(xs_ref[...], w1b[:, cols])

    def act_piece(cols):
        act_ref[:, cols] = (_silu(h1_ref[:, cols]) * _dot(xs_ref[...], w3b[:, cols])).astype(BF16)

    def y_piece(words):
        lo_cols = slice(words.start * LANES, words.stop * LANES)
        hi_cols = slice(d // 2 + words.start * LANES, d // 2 + words.stop * LANES)
        y = jnp.concatenate([_dot(act_ref[...], w2b[:, lo_cols]), _dot(act_ref[...], w2b[:, hi_cols])], axis=1)
        n = len(words) * LANES
        for k, s in enumerate(words):
            lo = lax.bitcast_convert_type(y[:, k * LANES:(k + 1) * LANES].astype(BF16).astype(F32), jnp.uint32)
            hi = lax.bitcast_convert_type(y[:, n + k * LANES:n + (k + 1) * LANES].astype(BF16).astype(F32), jnp.uint32)
            obuf[oth, _token_rows(obuf, MOE_ROWS, s, per_token), :] = (lo >> 16) | hi

    halves = (slice(0, ff // 2), slice(ff // 2, ff))
    n_y = min(4, per_token)
    word_groups = [range(q * per_token // n_y, (q + 1) * per_token // n_y) for q in range(n_y)]
    pieces = [functools.partial(unpack_and_h1, halves[0]), functools.partial(h1_piece, halves[1]),
              functools.partial(act_piece, halves[0]), functools.partial(act_piece, halves[1])]
    pieces += [functools.partial(y_piece, g) for g in word_groups]
    n_pieces = len(pieces)
    for p, piece in enumerate(pieces):
        if p == 4:
            @pl.when(scatter_wait_on)
            def _():
                wait_scatter(oth)

        pl.when(compute_on)(piece)
        for batch in range(p * DMA_BURSTS // n_pieces, (p + 1) * DMA_BURSTS // n_pieces):
            dma_burst(batch)


def _experts(hp, slot_src, slot_dst, block_expert, n_used, w1, w3, w2, layer, n_out_tokens):
    d, ff = w1.shape[-2:]
    per_token = d // 2 // LANES
    n_blocks = slot_src.shape[0]
    last = n_blocks - 1

    def wmap(i, be, nu):
        return (layer, be[i], 0, 0)

    buf = pltpu.VMEM((2, MOE_ROWS * per_token, LANES), jnp.uint32)
    grid_spec = pltpu.PrefetchScalarGridSpec(
        num_scalar_prefetch=2,
        grid=(n_blocks + 3,),
        in_specs=[pl.BlockSpec((1, 1, MOE_ROWS), lambda i, be, nu: (jnp.minimum(i, last), 0, 0),
                               memory_space=pltpu.SMEM),
                  pl.BlockSpec((1, 1, MOE_ROWS), lambda i, be, nu: (jnp.clip(i - 2, 0, last), 0, 0),
                               memory_space=pltpu.SMEM),
                  pl.BlockSpec(memory_space=pl.ANY),
                  pl.BlockSpec((1, 1, d, ff), wmap), pl.BlockSpec((1, 1, d, ff), wmap),
                  pl.BlockSpec((1, 1, ff, d), wmap)],
        out_specs=pl.BlockSpec(memory_space=pl.ANY),
        scratch_shapes=[buf, buf, pltpu.VMEM((MOE_ROWS, d), BF16),
                        pltpu.VMEM((MOE_ROWS, ff), F32), pltpu.VMEM((MOE_ROWS, ff), BF16),
                        pltpu.SemaphoreType.DMA((2,)), pltpu.SemaphoreType.DMA((2,)),
                        pltpu.VMEM((d, ff), BF16), pltpu.VMEM((d, ff), BF16), pltpu.VMEM((ff, d), BF16),
                        pltpu.SMEM((1,), jnp.int32)],
    )
    be_steps = jnp.concatenate([block_expert[:1], block_expert, block_expert[-1:], block_expert[-1:]])
    return pl.pallas_call(
        functools.partial(_expert_kernel, per_token=per_token),
        grid_spec=grid_spec,
        out_shape=jax.ShapeDtypeStruct((n_out_tokens * per_token, LANES), jnp.uint32),
        compiler_params=_cparams(("arbitrary",)),
        name="experts",
    )(be_steps, n_used, slot_src * per_token, slot_dst * per_token, hp, w1, w3, w2)


def _combine_kernel(*refs, alpha, emit_h):
    ys_refs, refs = refs[:TOP_K], refs[TOP_K:]
    (gate_ref, h_ref, ws1_ref, ws3_ref, ws2_ref, x_ref, gt_ref, g_ref, b_ref) = refs[:9]
    hidden = _silu(_dot(h_ref[...], ws1_ref[...])) * _dot(h_ref[...], ws3_ref[...])
    f = _dot(hidden.astype(BF16), ws2_ref[...])
    gate = gate_ref[...]
    tm, d = f.shape
    routed = None
    for k in range(TOP_K):
        chunks = _load_packed(ys_refs[k], (), tm, d)
        term = jnp.concatenate([lo for lo, _ in chunks] + [hi for _, hi in chunks], axis=1) * gate[:, k:k + 1]
        routed = term if routed is None else routed + term
    f = f + routed
    x2 = _layernorm(alpha * x_ref[...] + gt_ref[0, 0] * f) * g_ref[...] + b_ref[...]
    if emit_h:
        sc_ref, sh_ref, x2_ref, hn_ref = refs[9:]
        x2_ref[...] = x2
        hn_ref[...] = (x2 * (1.0 + sc_ref[0, 0]) + sh_ref[0, 0]).astype(BF16)
    else:
        refs[9][...] = x2


def _combine(rows, n_rows, yg, gate, h2, ws1, ws3, ws2, x1, mod_l, ln_g, ln_b, alpha, mod_next):
    d = x1.shape[1]
    sf = ws1.shape[1]
    tm = rows.tm
    k_blocks = n_rows // tm
    rspec = pl.BlockSpec((tm, d), lambda i: (i, 0))
    per_token = d // 2 // LANES
    in_specs = [pl.BlockSpec((tm * per_token, LANES), lambda i, k=k: (k * k_blocks + i, 0)) for k in range(TOP_K)]
    in_specs += [pl.BlockSpec((tm, TOP_K), lambda i: (i, 0)), rspec,
                 pl.BlockSpec((d, sf), lambda i: (0, 0)), pl.BlockSpec((d, sf), lambda i: (0, 0)),
                 pl.BlockSpec((sf, d), lambda i: (0, 0)), rspec, _mod_spec(rows, d, 5), _vec_spec(d), _vec_spec(d)]
    args = [yg] * TOP_K + [gate, h2, ws1, ws3, ws2, x1, mod_l, ln_g, ln_b]
    emit_h = mod_next is not None
    if emit_h:
        in_specs += [_mod_spec(rows, d, 1), _mod_spec(rows, d, 0)]
        args += [mod_next, mod_next]
        out_specs = [rspec, rspec]
        out_shape = [jax.ShapeDtypeStruct((n_rows, d), F32), jax.ShapeDtypeStruct((n_rows, d), BF16)]
    else:
        out_specs = rspec
        out_shape = jax.ShapeDtypeStruct((n_rows, d), F32)
    return pl.pallas_call(
        functools.partial(_combine_kernel, alpha=alpha, emit_h=emit_h),
        grid=(n_rows // tm,),
        in_specs=in_specs,
        out_specs=out_specs,
        out_shape=out_shape,
        compiler_params=_cparams(("parallel",)),
        name="moe_combine_norm",
    )(*args)


def _permute_w_in(w):
    a_end = A_Q + 2 * A_KV
    b_end = a_end + B_Q + 2 * B_KV
    c_end = b_end + C_CONV + C_V
    ab_end = c_end + 4 * C_V_HEADS
    main = jnp.concatenate([w[:, b_end:c_end], w[:, :b_end]], axis=1).astype(BF16)
    ab = jnp.pad(w[:, c_end:ab_end], ((0, 0), (0, AB_W - 4 * C_V_HEADS))).astype(BF16)
    gates = w[:, ab_end:].astype(BF16)
    return main, gates, ab


def kernel(x, c, ctx, c_ctx, w_mod, b_mod, w_in, q_norm_a, k_norm_a, sink_b, conv_c, a_log_c, dt_bias_c, norm_c,
           w_br_a, w_br_b, w_br_c, w_out, ln1_g, ln1_b, w_router, router_bias, w1, w3, w2, ws1, ws3, ws2,
           ln2_g, ln2_b):
    b, s, d = x.shape
    n_ctx = ctx.shape[1]
    depth = w_mod.shape[0]
    n_exp = w_router.shape[-1]
    alpha = (2.0 * depth) ** 0.25
    rows = _Rows(b, s, n_ctx)
    tm = rows.tm

    c_all = jnp.concatenate([c, c_ctx[None, :], jnp.zeros((8 - b - 1, d), F32)], axis=0)
    mod = _modulation(c_all, w_mod, b_mod).reshape(depth, 8, N_MOD, 1, d)
    tables = _rope_tables(s, tm)

    x_all = jnp.concatenate([x.reshape(b * s, d), ctx.reshape(b * n_ctx, d)], axis=0)
    x_cur, h = _entry(rows, x_all, mod[0])

    for l in range(depth):
        need_ctx = l < depth - 1
        n_rows = rows.r if need_ctx else rows.n_lat
        mod_l = mod[l]
        w_main, w_gates, w_ab = _permute_w_in(w_in[l])
        proj_main = _matmul(h, w_main, F32)
        gates = _matmul(h, w_gates, F32)
        proj_ab = _matmul(h, w_ab, F32)

        qkv = _attn_prep(rows, proj_main, tables, q_norm_a[l].reshape(1, -1), k_norm_a[l].reshape(1, -1))
        a_k, a_v = A_Q, A_Q + A_KV
        o_a = _attn_full(qkv, batch=b, n_q=s, q_base=0, sources=[(rows.n_lat, n_ctx), (0, s)],
                         q_col=0, k_col=a_k, v_col=a_v, heads=A_HEADS, kv_heads=A_KV_HEADS)
        o_b = _attn_window(qkv, sink_b[l], rows)
        if need_ctx:
            b_q = A_Q + 2 * A_KV
            o_a_c = _attn_full(qkv, batch=b, n_q=n_ctx, q_base=rows.n_lat, sources=[(rows.n_lat, n_ctx)],
                               q_col=0, k_col=a_k, v_col=a_v, heads=A_HEADS, kv_heads=A_KV_HEADS)
            o_b_c = _attn_full(qkv, batch=b, n_q=n_ctx, q_base=rows.n_lat, sources=[(rows.n_lat, n_ctx)],
                               q_col=b_q, k_col=b_q + B_Q, v_col=b_q + B_Q + B_KV, heads=B_HEADS,
                               kv_heads=B_KV_HEADS, sink=sink_b[l])
            o_a = jnp.concatenate([o_a, o_a_c], axis=0)
            o_b = jnp.concatenate([o_b, o_b_c], axis=0)

        dqkv, gbeta = _delta_prep(rows, proj_main, proj_ab, conv_c[l], a_log_c[l], dt_bias_c[l])
        o_f, o_r = _delta_scan(rows, dqkv, gbeta)

        y = _merge(n_rows, tm, o_a, o_b, o_f, o_r, proj_main, gates, norm_c[l].reshape(1, -1),
                   w_br_a[l].astype(BF16), w_br_b[l].astype(BF16), w_br_c[l].astype(BF16))
        x1, h2, h2p = _outproj(rows, n_rows, y, w_out[l].astype(BF16), x_cur, mod_l,
                          ln1_g[l].reshape(1, -1), ln1_b[l].reshape(1, -1), alpha)

        idx, gate, rank, counts = _router(rows, n_rows, x1, mod_l, w_router[l], router_bias[l])
        counts = counts[0].astype(jnp.int32)
        padded = (counts + MOE_ROWS - 1) // MOE_ROWS * MOE_ROWS
        padded_end = jnp.cumsum(padded)
        group_start = padded_end - padded
        experts_iota = jnp.arange(n_exp, dtype=jnp.int32)
        dest = rank + jnp.sum(jnp.where(idx[:, :, None] == experts_iota, group_start, 0), axis=-1)
        n_blocks = -(-(n_rows * TOP_K) // MOE_ROWS) + n_exp
        block_start = jnp.arange(n_blocks, dtype=jnp.int32)[:, None] * MOE_ROWS
        block_expert = jnp.minimum(jnp.sum((padded_end[None, :] <= block_start).astype(jnp.int32), axis=-1), n_exp - 1)
        n_used = (padded_end[-1:] // MOE_ROWS).astype(jnp.int32)
        n_slots = n_blocks * MOE_ROWS
        code = (jnp.arange(n_rows, dtype=jnp.int32)[:, None] * TOP_K + jnp.arange(TOP_K, dtype=jnp.int32)[None, :])
        slot_code = jnp.full((n_slots,), -1, jnp.int32).at[dest.reshape(-1)].set(
            code.reshape(-1), unique_indices=True, mode="promise_in_bounds")
        filled = slot_code >= 0
        slot_iota = jnp.arange(n_slots, dtype=jnp.int32)
        slot_src = jnp.where(filled, slot_code // TOP_K, 0)
        slot_dst = jnp.where(filled, (slot_code % TOP_K) * n_rows + slot_code // TOP_K,
                             TOP_K * n_rows + slot_iota % (2 * MOE_ROWS))
        yg = _experts(h2p, slot_src.reshape(n_blocks, 1, MOE_ROWS), slot_dst.reshape(n_blocks, 1, MOE_ROWS),
                      block_expert, n_used, w1, w3, w2, l, TOP_K * n_rows + 2 * MOE_ROWS)

        mod_next = mod[l + 1] if need_ctx else None
        out = _combine(rows, n_rows, yg, gate, h2, ws1[l].astype(BF16), ws3[l].astype(BF16), ws2[l].astype(BF16),
                       x1, mod_l, ln2_g[l].reshape(1, -1), ln2_b[l].reshape(1, -1), alpha, mod_next)
        if need_ctx:
            x_cur, h = out
        else:
            x_cur = out
    return x_cur[:rows.n_lat].reshape(b, s, d)
```

```python
import functools
import math

import jax
import jax.numpy as jnp
from jax import lax
from jax.experimental import pallas as pl
from jax.experimental.pallas import tpu as pltpu

F32 = jnp.float32
BF16 = jnp.bfloat16
HIGHEST = lax.Precision.HIGHEST

HEAD_DIM = 128
GRID_W = 64
ROPE_THETA = 10000.0
A_HEADS, A_KV_HEADS = 8, 2
B_HEADS, B_KV_HEADS = 8, 2
WINDOW = 128
Q_BLOCK = 128
C_QK_HEADS, C_V_HEADS = 4, 8
CONV_K = 5
CHUNK = 64
TOP_K = 8
ROUTED_SCALE = 2.5
LN_EPS = 1e-5
RMS_EPS = 1e-6
LOG2_E = 1.4426950408889634
MOE_ROWS = 256
N_MOD = 6
HALO = 8
VMEM_LIMIT = 56 * 2**20

A_Q, A_KV = A_HEADS * HEAD_DIM, A_KV_HEADS * HEAD_DIM
B_Q, B_KV = B_HEADS * HEAD_DIM, B_KV_HEADS * HEAD_DIM
C_QK, C_V = C_QK_HEADS * HEAD_DIM, C_V_HEADS * HEAD_DIM
C_CONV = 2 * C_QK + C_V
ATT_W = A_Q + 2 * A_KV + B_Q + 2 * B_KV
MAIN_W = C_CONV + C_V + ATT_W
AB_W = 128


def _cparams(sem):
    return pltpu.CompilerParams(dimension_semantics=sem, vmem_limit_bytes=VMEM_LIMIT)


def _pick(n, prefs):
    for p in prefs:
        if n % p == 0:
            return p
    return n


def _dot(a, b, **kw):
    return jnp.dot(a, b, preferred_element_type=F32, **kw)


def _dot_nt(a, b):
    return lax.dot_general(a, b, (((1,), (1,)), ((), ())), preferred_element_type=F32)


def _dot_tn(a, b):
    return lax.dot_general(a, b, (((0,), (0,)), ((), ())), preferred_element_type=F32)


def _silu(x):
    return x * jax.nn.sigmoid(x)


def _mod_kernel(c_ref, w_ref, b_ref, o_ref):
    o_ref[0] = _dot(_silu(c_ref[...]), w_ref[0], precision=HIGHEST) + b_ref[0]


def _modulation(c_all, w_mod, b_mod):
    depth, d, width = w_mod.shape
    tn = _pick(width, (1024, 512, 256, 128))
    return pl.pallas_call(
        _mod_kernel,
        grid=(depth, width // tn),
        in_specs=[pl.BlockSpec((8, d), lambda l, j: (0, 0)),
                  pl.BlockSpec((1, d, tn), lambda l, j: (l, 0, j)),
                  pl.BlockSpec((1, 1, tn), lambda l, j: (l, 0, j))],
        out_specs=pl.BlockSpec((1, 8, tn), lambda l, j: (l, 0, j)),
        out_shape=jax.ShapeDtypeStruct((depth, 8, width), F32),
        compiler_params=_cparams(("parallel", "parallel")),
        name="modulation",
    )(c_all, w_mod, b_mod.reshape(depth, 1, width))


class _Rows:
    def __init__(self, b, s, c):
        self.b, self.s, self.c = b, s, c
        self.tm = 256 if c % 256 == 0 else 128
        self.n_lat = b * s
        self.r = b * s + b * c
        self.lat_blocks = self.n_lat // self.tm
        self.blocks = self.r // self.tm

    def group(self, i):
        return jnp.where(i < self.lat_blocks, i // (self.s // self.tm), self.b)


def _mod_spec(rows, d, which):
    return pl.BlockSpec((1, 1, 1, d), lambda i: (rows.group(i), which, 0, 0))


def _vec_spec(d):
    return pl.BlockSpec((1, d), lambda i: (0, 0))


def _layernorm(x):
    mu = jnp.mean(x, axis=-1, keepdims=True)
    xc = x - mu
    var = jnp.mean(xc * xc, axis=-1, keepdims=True)
    return xc * lax.rsqrt(var + LN_EPS)


def _entry_kernel(x_ref, sc_ref, sh_ref, x0_ref, h_ref):
    xn = _layernorm(x_ref[...])
    x0_ref[...] = xn
    h_ref[...] = (xn * (1.0 + sc_ref[0, 0]) + sh_ref[0, 0]).astype(BF16)


def _entry(rows, x_all, mod_l):
    r, d = x_all.shape
    tm = rows.tm
    row_spec = pl.BlockSpec((tm, d), lambda i: (i, 0))
    return pl.pallas_call(
        _entry_kernel,
        grid=(r // tm,),
        in_specs=[row_spec, _mod_spec(rows, d, 1), _mod_spec(rows, d, 0)],
        out_specs=[row_spec, row_spec],
        out_shape=[jax.ShapeDtypeStruct((r, d), F32), jax.ShapeDtypeStruct((r, d), BF16)],
        compiler_params=_cparams(("parallel",)),
        name="entry_norm",
    )(x_all, mod_l, mod_l)


def _mm_kernel(a_ref, w_ref, o_ref):
    o_ref[...] = _dot(a_ref[...], w_ref[...]).astype(o_ref.dtype)


def _matmul(a, w, out_dtype):
    m, k = a.shape
    n = w.shape[1]
    tm = _pick(m, (1024, 512, 256, 128))
    tn = _pick(n, (1024, 512, 256, 128))
    return pl.pallas_call(
        _mm_kernel,
        grid=(n // tn, m // tm),
        in_specs=[pl.BlockSpec((tm, k), lambda j, i: (i, 0)),
                  pl.BlockSpec((k, tn), lambda j, i: (0, j))],
        out_specs=pl.BlockSpec((tm, tn), lambda j, i: (i, j)),
        out_shape=jax.ShapeDtypeStruct((m, n), out_dtype),
        compiler_params=_cparams(("parallel", "parallel")),
        name="in_proj",
    )(a, w)


def _rope(t, cos, sin_lo, sin_hi):
    return t * cos + pltpu.roll(t, HEAD_DIM - 32, 1) * sin_lo + pltpu.roll(t, 32, 1) * sin_hi


def _rms(t, g):
    return t * lax.rsqrt(jnp.mean(t * t, axis=-1, keepdims=True) + RMS_EPS) * g


def _attn_prep_kernel(x_ref, cos_ref, slo_ref, shi_ref, qn_ref, kn_ref, o_ref):
    cos, slo, shi = cos_ref[...], slo_ref[...], shi_ref[...]
    scale = HEAD_DIM ** -0.5 * LOG2_E

    def head(j):
        return x_ref[:, j * HEAD_DIM:(j + 1) * HEAD_DIM]

    def put(j, v):
        o_ref[:, j * HEAD_DIM:(j + 1) * HEAD_DIM] = v.astype(BF16)

    j = 0
    for _ in range(A_HEADS):
        put(j, _rope(_rms(head(j), qn_ref[...]), cos, slo, shi) * scale)
        j += 1
    for _ in range(A_KV_HEADS):
        put(j, _rope(_rms(head(j), kn_ref[...]), cos, slo, shi))
        j += 1
    for _ in range(A_KV_HEADS):
        put(j, head(j))
        j += 1
    for _ in range(B_HEADS):
        put(j, _rope(head(j), cos, slo, shi) * scale)
        j += 1
    for _ in range(B_KV_HEADS):
        put(j, _rope(head(j), cos, slo, shi))
        j += 1
    for _ in range(B_KV_HEADS):
        put(j, head(j))
        j += 1


def _attn_prep(rows, proj_main, tables, q_norm, k_norm):
    r = proj_main.shape[0]
    tm = rows.tm
    pos_blocks = rows.s // tm
    col_blk = (C_CONV + C_V) // ATT_W
    assert col_blk * ATT_W == C_CONV + C_V

    def tab_map(i):
        return (jnp.where(i < rows.lat_blocks, i % pos_blocks, pos_blocks), 0)

    tab_spec = pl.BlockSpec((tm, HEAD_DIM), tab_map)
    return pl.pallas_call(
        _attn_prep_kernel,
        grid=(r // tm,),
        in_specs=[pl.BlockSpec((tm, ATT_W), lambda i: (i, col_blk)), tab_spec, tab_spec, tab_spec,
                  _vec_spec(HEAD_DIM), _vec_spec(HEAD_DIM)],
        out_specs=pl.BlockSpec((tm, ATT_W), lambda i: (i, 0)),
        out_shape=jax.ShapeDtypeStruct((r, ATT_W), BF16),
        compiler_params=_cparams(("parallel",)),
        name="attn_prep",
    )(proj_main, *tables, q_norm, k_norm)


def _rope_tables(s, tm):
    row = jnp.repeat(jnp.arange(s // GRID_W, dtype=F32), GRID_W)
    col = jnp.tile(jnp.arange(GRID_W, dtype=F32), s // GRID_W)
    axis_dim = HEAD_DIM // 2
    inv_freq = ROPE_THETA ** (-jnp.arange(0, axis_dim, 2, dtype=F32) / axis_dim)
    ang_r = row[:, None] * inv_freq[None, :]
    ang_c = col[:, None] * inv_freq[None, :]
    ang = jnp.concatenate([ang_r, ang_r, ang_c, ang_c], axis=-1)
    cos, sin = jnp.cos(ang), jnp.sin(ang)
    lo = (jnp.arange(HEAD_DIM) % axis_dim) < (axis_dim // 2)
    sin_lo = jnp.where(lo, -sin, 0.0)
    sin_hi = jnp.where(lo, 0.0, sin)
    one, zero = jnp.ones((tm, HEAD_DIM), F32), jnp.zeros((tm, HEAD_DIM), F32)
    return (jnp.concatenate([cos, one]), jnp.concatenate([sin_lo, zero]), jnp.concatenate([sin_hi, zero]))


def _attn_full_kernel(*refs, n_src, has_sink, groups):
    if has_sink:
        sink_ref, refs = refs[0], refs[1:]
    q_ref, o_ref = refs[0], refs[1 + 2 * n_src]
    kv = refs[1:1 + 2 * n_src]
    kvh = pl.program_id(1)
    for g in range(groups):
        q = q_ref[:, g * HEAD_DIM:(g + 1) * HEAD_DIM]
        scores = [_dot_nt(q, kv[2 * j][...]) for j in range(n_src)]
        m = functools.reduce(jnp.maximum, [jnp.max(s, axis=-1, keepdims=True) for s in scores])
        if has_sink:
            sink = sink_ref[kvh * groups + g] * LOG2_E
            m = jnp.maximum(m, sink)
        es = [jnp.exp2(s - m) for s in scores]
        den = functools.reduce(jnp.add, [jnp.sum(e, axis=-1, keepdims=True) for e in es])
        if has_sink:
            den = den + jnp.exp2(sink - m)
        acc = functools.reduce(jnp.add, [_dot(es[j].astype(BF16), kv[2 * j + 1][...]) for j in range(n_src)])
        o_ref[:, g * HEAD_DIM:(g + 1) * HEAD_DIM] = (acc / den).astype(o_ref.dtype)


def _attn_full(qkv, *, batch, n_q, q_base, sources, q_col, k_col, v_col, heads, kv_heads, sink=None):
    groups = heads // kv_heads
    tq = _pick(n_q, (256, 128))
    qw = groups * HEAD_DIM
    nq = n_q // tq
    in_specs = []
    args = []
    if sink is not None:
        in_specs.append(pl.BlockSpec(memory_space=pltpu.SMEM))
        args.append(sink)
    in_specs.append(pl.BlockSpec((tq, qw), lambda b, h, i: (q_base // tq + b * nq + i, q_col // qw + h)))
    args.append(qkv)
    for base, n in sources:
        for col in (k_col, v_col):
            in_specs.append(pl.BlockSpec(
                (n, HEAD_DIM), lambda b, h, i, base=base, n=n, col=col: (base // n + b, col // HEAD_DIM + h)))
            args.append(qkv)
    kern = functools.partial(_attn_full_kernel, n_src=len(sources), has_sink=sink is not None, groups=groups)
    return pl.pallas_call(
        kern,
        grid=(batch, kv_heads, nq),
        in_specs=in_specs,
        out_specs=pl.BlockSpec((tq, qw), lambda b, h, i: (b * nq + i, h)),
        out_shape=jax.ShapeDtypeStruct((batch * n_q, heads * HEAD_DIM), BF16),
        compiler_params=_cparams(("parallel", "parallel", "arbitrary")),
        name="attn_full",
    )(*args)


def _attn_win_kernel(sink_ref, q_ref, k_ref, v_ref, kc_ref, vc_ref, o_ref, *, groups, seq):
    kvh = pl.program_id(1)
    n = pl.program_id(2)
    span = 3 * Q_BLOCK
    start = jnp.clip((n - 1) * Q_BLOCK, 0, seq - span)
    start = pl.multiple_of(start, Q_BLOCK)
    kb = k_ref[pl.ds(start, span), :]
    vb = v_ref[pl.ds(start, span), :]
    q_pos = n * Q_BLOCK + lax.broadcasted_iota(jnp.int32, (Q_BLOCK, span), 0)
    k_pos = start + lax.broadcasted_iota(jnp.int32, (Q_BLOCK, span), 1)
    valid = jnp.abs(k_pos - q_pos) <= WINDOW
    for g in range(groups):
        q = q_ref[:, g * HEAD_DIM:(g + 1) * HEAD_DIM]
        s_win = jnp.where(valid, _dot_nt(q, kb), -jnp.inf)
        s_ctx = _dot_nt(q, kc_ref[...])
        sink = sink_ref[kvh * groups + g] * LOG2_E
        m = jnp.maximum(jnp.maximum(jnp.max(s_win, axis=-1, keepdims=True),
                                    jnp.max(s_ctx, axis=-1, keepdims=True)), sink)
        e_win = jnp.exp2(s_win - m)
        e_ctx = jnp.exp2(s_ctx - m)
        den = (jnp.sum(e_win, axis=-1, keepdims=True) + jnp.sum(e_ctx, axis=-1, keepdims=True)
               + jnp.exp2(sink - m))
        acc = _dot(e_win.astype(BF16), vb) + _dot(e_ctx.astype(BF16), vc_ref[...])
        o_ref[:, g * HEAD_DIM:(g + 1) * HEAD_DIM] = (acc / den).astype(o_ref.dtype)


def _attn_window(qkv, sink, rows):
    b, s, c = rows.b, rows.s, rows.c
    groups = B_HEADS // B_KV_HEADS
    qw = groups * HEAD_DIM
    nq = s // Q_BLOCK
    q_col = A_Q + 2 * A_KV
    k_col = q_col + B_Q
    v_col = k_col + B_KV
    ctx_base = rows.n_lat // c

    def kv_spec(n, base, col):
        return pl.BlockSpec((n, HEAD_DIM), lambda bb, h, i: (base + bb, col // HEAD_DIM + h))

    return pl.pallas_call(
        functools.partial(_attn_win_kernel, groups=groups, seq=s),
        grid=(b, B_KV_HEADS, nq),
        in_specs=[pl.BlockSpec(memory_space=pltpu.SMEM),
                  pl.BlockSpec((Q_BLOCK, qw), lambda bb, h, i: (bb * nq + i, q_col // qw + h)),
                  kv_spec(s, 0, k_col), kv_spec(s, 0, v_col),
                  kv_spec(c, ctx_base, k_col), kv_spec(c, ctx_base, v_col)],
        out_specs=pl.BlockSpec((Q_BLOCK, qw), lambda bb, h, i: (bb * nq + i, h)),
        out_shape=jax.ShapeDtypeStruct((b * s, B_Q), BF16),
        compiler_params=_cparams(("parallel", "parallel", "arbitrary")),
        name="attn_window",
    )(sink, qkv, qkv, qkv, qkv, qkv)


def _delta_prep_kernel(prev_ref, cur_ref, next_ref, ab_ref, w_ref, alog_ref, dtb_ref, qkv_ref, gb_ref,
                       *, lat_blocks, seq_blocks, ctx_blocks):
    i = pl.program_id(0)
    tm = cur_ref.shape[0]
    is_lat = i < lat_blocks
    pos = jnp.where(is_lat, i % seq_blocks, (i - lat_blocks) % ctx_blocks)
    n_pos = jnp.where(is_lat, seq_blocks, ctx_blocks)
    first = pos == 0
    last = pos == n_pos - 1
    keep_prev = jnp.where(first, 0.0, 1.0)
    keep_next = jnp.where(last, 0.0, 1.0)
    ext = tm + 2 * HALO
    n_qk = 2 * C_QK_HEADS
    for j in range(C_CONV // HEAD_DIM):
        sl = slice(j * HEAD_DIM, (j + 1) * HEAD_DIM)
        xt = jnp.concatenate([prev_ref[:, sl] * keep_prev, cur_ref[:, sl], next_ref[:, sl] * keep_next], axis=0)
        acc = None
        for tap in range(CONV_K):
            shift = (CONV_K // 2 - tap) % ext
            xs = xt if shift == 0 else pltpu.roll(xt, shift, 0)
            term = xs[HALO:HALO + tm] * w_ref[tap:tap + 1, sl]
            acc = term if acc is None else acc + term
        y = _silu(acc)
        if j < n_qk:
            y = y * lax.rsqrt(jnp.sum(y * y, axis=-1, keepdims=True) + RMS_EPS)
            if j < C_QK_HEADS:
                y = y * (HEAD_DIM ** -0.5)
        qkv_ref[:, sl] = y
    ab = ab_ref[...]
    lane = lax.broadcasted_iota(jnp.int32, ab.shape, 1)
    n_g = 2 * C_V_HEADS
    z = ab + dtb_ref[...]
    softplus = jnp.maximum(z, 0.0) + jnp.log(1.0 + jnp.exp(-jnp.abs(z)))
    g = -jnp.exp(alog_ref[...]) * softplus
    gb_ref[...] = jnp.where(lane < n_g, g, jax.nn.sigmoid(ab))


def _delta_prep(rows, proj_main, proj_ab, conv_w, a_log, dt_bias):
    r = proj_main.shape[0]
    tm = rows.tm
    hb = tm // HALO
    last_halo = r // HALO - 1
    pad = AB_W - a_log.size
    alog = jnp.pad(a_log.reshape(1, -1), ((0, 0), (0, pad)))
    dtb = jnp.pad(dt_bias.reshape(1, -1), ((0, 0), (0, pad)))
    kern = functools.partial(_delta_prep_kernel, lat_blocks=rows.lat_blocks, seq_blocks=rows.s // tm,
                             ctx_blocks=rows.c // tm)
    return pl.pallas_call(
        kern,
        grid=(r // tm,),
        in_specs=[pl.BlockSpec((HALO, C_CONV), lambda i: (jnp.maximum(i * hb - 1, 0), 0)),
                  pl.BlockSpec((tm, C_CONV), lambda i: (i, 0)),
                  pl.BlockSpec((HALO, C_CONV), lambda i: (jnp.minimum((i + 1) * hb, last_halo), 0)),
                  pl.BlockSpec((tm, AB_W), lambda i: (i, 0)),
                  pl.BlockSpec((CONV_K, C_CONV), lambda i: (0, 0)),
                  _vec_spec(AB_W), _vec_spec(AB_W)],
        out_specs=[pl.BlockSpec((tm, C_CONV), lambda i: (i, 0)), pl.BlockSpec((tm, AB_W), lambda i: (i, 0))],
        out_shape=[jax.ShapeDtypeStruct((r, C_CONV), F32), jax.ShapeDtypeStruct((r, AB_W), F32)],
        compiler_params=_cparams(("parallel",)),
        name="delta_prep",
    )(proj_main, proj_main, proj_main, proj_ab, conv_w, alog, dtb)


def _delta_kernel(qkv_f_ref, gb_f_ref, qkv_r_ref, gb_r_ref, o_f_ref, o_r_ref, state_ref):
    @pl.when(pl.program_id(1) == 0)
    def _():
        state_ref[...] = jnp.zeros_like(state_ref)

    row = lax.broadcasted_iota(jnp.int32, (CHUNK, CHUNK), 0)
    col = lax.broadcasted_iota(jnp.int32, (CHUNK, CHUNK), 1)
    eye = jnp.where(row == col, 1.0, 0.0)
    rep = C_V_HEADS // C_QK_HEADS
    n_sq = int(math.log2(CHUNK)) - 1
    chains = []
    for direction, (qkv_ref, gb_ref, o_ref) in enumerate(((qkv_f_ref, gb_f_ref, o_f_ref),
                                                          (qkv_r_ref, gb_r_ref, o_r_ref))):
        if direction == 0:
            incl, strict, last = row >= col, row > col, CHUNK - 1
        else:
            incl, strict, last = row <= col, row < col, 0
        gb = gb_ref[...]
        gcum = _dot(jnp.where(incl, 1.0, 0.0), gb, precision=HIGHEST)
        gcum_t = gcum.T
        kq = []
        for qh in range(C_QK_HEADS):
            q = qkv_ref[:, qh * HEAD_DIM:(qh + 1) * HEAD_DIM]
            k = qkv_ref[:, C_QK + qh * HEAD_DIM:C_QK + (qh + 1) * HEAD_DIM]
            prod = _dot_nt(jnp.concatenate([k, q], axis=0).astype(BF16), k.astype(BF16))
            kq.append((q, k, prod[:CHUNK], prod[CHUNK:]))
        for h in range(C_V_HEADS):
            lane = direction * C_V_HEADS + h
            q, k, kk, qk = kq[h // rep]
            g_col = gcum[:, lane:lane + 1]
            beta = gb[:, 2 * C_V_HEADS + lane:2 * C_V_HEADS + lane + 1]
            decay = jnp.where(incl, jnp.exp(jnp.minimum(g_col - gcum_t[lane:lane + 1, :], 0.0)), 0.0)
            v = qkv_ref[:, 2 * C_QK + h * HEAD_DIM:2 * C_QK + (h + 1) * HEAD_DIM]
            p = jnp.where(strict, -(kk * beta) * decay, 0.0)
            chains.append(dict(
                h=h, slot=lane, o_ref=o_ref, q=q, k=k, g_col=g_col, e_g=jnp.exp(g_col),
                g_last=gcum[last:last + 1, lane:lane + 1], beta=beta, v=v, qk=qk * decay, p=p, t=eye + p))
    for _ in range(n_sq):
        for ch in chains:
            pb = ch["p"].astype(BF16)
            ch["p"] = _dot(pb, pb)
        for ch in chains:
            ch["t"] = ch["t"] + _dot(ch["t"].astype(BF16), ch["p"].astype(BF16))
    for ch in chains:
        rhs = jnp.concatenate([ch["v"] * ch["beta"], ch["k"] * (ch["beta"] * ch["e_g"])], axis=1)
        uw = _dot(ch["t"].astype(BF16), rhs.astype(BF16))
        ch["u"], ch["w"] = uw[:, :HEAD_DIM], uw[:, HEAD_DIM:]
    for ch in chains:
        ch["state"] = state_ref[ch["slot"]]
        lhs = jnp.concatenate([ch["w"], ch["q"] * ch["e_g"]], axis=0)
        ch["ws_qs"] = _dot(lhs.astype(BF16), ch["state"].astype(BF16))
    for ch in chains:
        ch["v_new"] = (ch["u"] - ch["ws_qs"][:CHUNK]).astype(BF16)
    for ch in chains:
        h = ch["h"]
        ch["o_ref"][:, h * HEAD_DIM:(h + 1) * HEAD_DIM] = ch["ws_qs"][CHUNK:] + _dot(ch["qk"].astype(BF16), ch["v_new"])
    for ch in chains:
        k_dec = ch["k"] * jnp.exp(ch["g_last"] - ch["g_col"])
        state_ref[ch["slot"]] = ch["state"] * jnp.exp(ch["g_last"]) + _dot_tn(k_dec.astype(BF16), ch["v_new"])


def _delta_scan(rows, dqkv, gb):
    b, s, c = rows.b, rows.s, rows.c
    ctx_chunks, lat_chunks = c // CHUNK, s // CHUNK
    ctx_base = rows.n_lat // CHUNK

    def fwd_map(bb, t):
        return (jnp.where(t < ctx_chunks, ctx_base + bb * ctx_chunks + t, bb * lat_chunks + t - ctx_chunks), 0)

    def rev_map(bb, t):
        ctx_i, lat_i = ctx_chunks - 1 - t, lat_chunks - 1 - (t - ctx_chunks)
        return (jnp.where(t < ctx_chunks, ctx_base + bb * ctx_chunks + ctx_i, bb * lat_chunks + lat_i), 0)

    out = jax.ShapeDtypeStruct((rows.r, C_V), F32)
    return pl.pallas_call(
        _delta_kernel,
        grid=(b, ctx_chunks + lat_chunks),
        in_specs=[pl.BlockSpec((CHUNK, C_CONV), fwd_map), pl.BlockSpec((CHUNK, AB_W), fwd_map),
                  pl.BlockSpec((CHUNK, C_CONV), rev_map), pl.BlockSpec((CHUNK, AB_W), rev_map)],
        out_specs=[pl.BlockSpec((CHUNK, C_V), fwd_map), pl.BlockSpec((CHUNK, C_V), rev_map)],
        out_shape=[out, out],
        scratch_shapes=[pltpu.VMEM((2 * C_V_HEADS, HEAD_DIM, HEAD_DIM), F32)],
        compiler_params=_cparams(("parallel", "arbitrary")),
        name="delta_scan",
    )(dqkv, gb, dqkv, gb)


def _merge_kernel(oa_ref, ob_ref, of_ref, or_ref, z_ref, nc_ref, ga_ref, gb_ref, gc_ref,
                  wa_ref, wb_ref, wc_ref, y_ref, oc_ref):
    for h in range(C_V_HEADS):
        sl = slice(h * HEAD_DIM, (h + 1) * HEAD_DIM)
        o = of_ref[:, sl] + or_ref[:, sl]
        oc_ref[:, sl] = (_rms(o, nc_ref[...]) * _silu(z_ref[:, sl])).astype(BF16)
    y = jax.nn.sigmoid(ga_ref[...]) * _dot(oa_ref[...], wa_ref[...])
    y = y + jax.nn.sigmoid(gb_ref[...]) * _dot(ob_ref[...], wb_ref[...])
    y = y + jax.nn.sigmoid(gc_ref[...]) * _dot(oc_ref[...], wc_ref[...])
    y_ref[...] = y.astype(BF16)


def _merge(n_rows, tm, o_a, o_b, o_f, o_r, proj_main, gates, norm_c, w_a, w_b, w_c):
    d = w_a.shape[1]
    z_blk = C_CONV // C_V

    def rspec(w, col=0):
        return pl.BlockSpec((tm, w), lambda i: (i, col))

    def wspec(kdim):
        return pl.BlockSpec((kdim, d), lambda i: (0, 0))

    return pl.pallas_call(
        _merge_kernel,
        grid=(n_rows // tm,),
        in_specs=[rspec(A_Q), rspec(B_Q), rspec(C_V), rspec(C_V), rspec(C_V, z_blk), _vec_spec(HEAD_DIM),
                  rspec(d, 0), rspec(d, 1), rspec(d, 2), wspec(A_Q), wspec(B_Q), wspec(C_V)],
        out_specs=rspec(d),
        out_shape=jax.ShapeDtypeStruct((n_rows, d), BF16),
        scratch_shapes=[pltpu.VMEM((tm, C_V), BF16)],
        compiler_params=_cparams(("parallel",)),
        name="branch_merge",
    )(o_a, o_b, o_f, o_r, proj_main, norm_c, gates, gates, gates, w_a, w_b, w_c)


LANES = 128


def _token_rows(ref_rows, tokens, s, per_token):
    return pl.ds(s, tokens, stride=per_token) if per_token > 1 else pl.ds(0, tokens)


def _store_packed(ref, lead, x):
    m, d = x.shape
    half = d // 2
    per_token = half // LANES
    for s in range(per_token):
        lo = lax.bitcast_convert_type(x[:, s * LANES:(s + 1) * LANES].astype(BF16).astype(F32), jnp.uint32)
        hi = lax.bitcast_convert_type(x[:, half + s * LANES:half + (s + 1) * LANES].astype(BF16).astype(F32),
                                      jnp.uint32)
        ref[(*lead, _token_rows(ref, m, s, per_token), slice(None))] = (lo >> 16) | hi


def _load_packed(ref, lead, m, d):
    per_token = d // 2 // LANES
    out = []
    for s in range(per_token):
        p = ref[(*lead, _token_rows(ref, m, s, per_token), slice(None))]
        out.append((lax.bitcast_convert_type(p << 16, F32),
                    lax.bitcast_convert_type(p & jnp.uint32(0xFFFF0000), F32)))
    return out


def _outproj_kernel(y_ref, w_ref, x_ref, gt_ref, g_ref, b_ref, sc_ref, sh_ref, x1_ref, h_ref, hp_ref, *, alpha):
    y = _dot(y_ref[...], w_ref[...])
    x1 = _layernorm(alpha * x_ref[...] + gt_ref[0, 0] * y) * g_ref[...] + b_ref[...]
    x1_ref[...] = x1
    h = x1 * (1.0 + sc_ref[0, 0]) + sh_ref[0, 0]
    h_ref[...] = h.astype(BF16)
    _store_packed(hp_ref, (), h)


def _outproj(rows, n_rows, y, w_out, x, mod_l, ln_g, ln_b, alpha):
    d = x.shape[1]
    tm = rows.tm
    per_token = d // 2 // LANES
    rspec = pl.BlockSpec((tm, d), lambda i: (i, 0))
    return pl.pallas_call(
        functools.partial(_outproj_kernel, alpha=alpha),
        grid=(n_rows // tm,),
        in_specs=[rspec, pl.BlockSpec((d, d), lambda i: (0, 0)), rspec, _mod_spec(rows, d, 2),
                  _vec_spec(d), _vec_spec(d), _mod_spec(rows, d, 4), _mod_spec(rows, d, 3)],
        out_specs=[rspec, rspec, pl.BlockSpec((tm * per_token, LANES), lambda i: (i, 0))],
        out_shape=[jax.ShapeDtypeStruct((n_rows, d), F32), jax.ShapeDtypeStruct((n_rows, d), BF16),
                   jax.ShapeDtypeStruct((n_rows * per_token, LANES), jnp.uint32)],
        compiler_params=_cparams(("parallel",)),
        name="out_proj_norm",
    )(y, w_out, x, mod_l, ln_g, ln_b, mod_l, mod_l)


def _router_kernel(x_ref, sc_ref, sh_ref, wr_ref, rb_ref, idx_ref, gate_ref, rank_ref, cnt_ref, carry_ref):
    @pl.when(pl.program_id(0) == 0)
    def _():
        carry_ref[...] = jnp.zeros_like(carry_ref)

    h = x_ref[...] * (1.0 + sc_ref[0, 0]) + sh_ref[0, 0]
    scores = jax.nn.sigmoid(_dot(h, wr_ref[...], precision=HIGHEST))
    tm, n_exp = scores.shape
    sel = scores + rb_ref[...]
    lane = lax.broadcasted_iota(jnp.int32, (tm, n_exp), 1)
    slot = lax.broadcasted_iota(jnp.int32, (tm, TOP_K), 1)
    idx = jnp.zeros((tm, TOP_K), jnp.int32)
    gate = jnp.zeros((tm, TOP_K), F32)
    mask = jnp.zeros((tm, n_exp), F32)
    hits = []
    for k in range(TOP_K):
        best = jnp.max(sel, axis=-1, keepdims=True)
        choice = jnp.min(jnp.where(sel == best, lane, n_exp), axis=-1, keepdims=True)
        hit = lane == choice
        hits.append(hit)
        idx = jnp.where(slot == k, choice, idx)
        gate = jnp.where(slot == k, jnp.sum(jnp.where(hit, scores, 0.0), axis=-1, keepdims=True), gate)
        sel = jnp.where(hit, -jnp.inf, sel)
        mask = jnp.where(hit, 1.0, mask)
    idx_ref[...] = idx
    gate_ref[...] = gate / jnp.sum(gate, axis=-1, keepdims=True) * ROUTED_SCALE
    r_i = lax.broadcasted_iota(jnp.int32, (tm, tm), 0)
    c_i = lax.broadcasted_iota(jnp.int32, (tm, tm), 1)
    before = jnp.where(c_i < r_i, 1.0, 0.0).astype(BF16)
    prefix = _dot(before, mask.astype(BF16)) + carry_ref[...]
    rank = jnp.zeros((tm, TOP_K), F32)
    for k in range(TOP_K):
        rank = jnp.where(slot == k, jnp.sum(jnp.where(hits[k], prefix, 0.0), axis=-1, keepdims=True), rank)
    rank_ref[...] = rank.astype(jnp.int32)
    carry_ref[...] = carry_ref[...] + jnp.sum(mask, axis=0, keepdims=True)
    cnt_ref[...] = carry_ref[...]


def _router(rows, n_rows, x1, mod_l, w_router, router_bias):
    d, n_exp = w_router.shape
    tm = rows.tm
    kspec = pl.BlockSpec((tm, TOP_K), lambda i: (i, 0))
    return pl.pallas_call(
        _router_kernel,
        grid=(n_rows // tm,),
        in_specs=[pl.BlockSpec((tm, d), lambda i: (i, 0)), _mod_spec(rows, d, 4), _mod_spec(rows, d, 3),
                  pl.BlockSpec((d, n_exp), lambda i: (0, 0)), _vec_spec(n_exp)],
        out_specs=[kspec, kspec, kspec, _vec_spec(n_exp)],
        out_shape=[jax.ShapeDtypeStruct((n_rows, TOP_K), jnp.int32), jax.ShapeDtypeStruct((n_rows, TOP_K), F32),
                   jax.ShapeDtypeStruct((n_rows, TOP_K), jnp.int32), jax.ShapeDtypeStruct((1, n_exp), F32)],
        scratch_shapes=[pltpu.VMEM((1, n_exp), F32)],
        compiler_params=_cparams(("arbitrary",)),
        name="router",
    )(x1, mod_l, mod_l, w_router, router_bias.reshape(1, n_exp))


def _expert_kernel(be_ref, nu_ref, src_ref, dst_ref, hp_ref, w1_ref, w3_ref, w2_ref, yg_ref,
                   xbuf, obuf, xs_ref, gsem, ssem, w1b, w3b, w2b, prev_ref, *, per_token):
    blk = pl.program_id(0)
    n_used = nu_ref[0]
    g_slot = blk % 2
    c_slot = 1 - g_slot
    block_rows = MOE_ROWS * per_token
    d = xs_ref.shape[1]

    def token_rows(start):
        return pl.ds(pl.multiple_of(start, per_token), per_token)

    def gather(buf_slot):
        for j in range(MOE_ROWS):
            pltpu.make_async_copy(hp_ref.at[token_rows(src_ref[0, 0, j])],
                                  xbuf.at[buf_slot, pl.ds(j * per_token, per_token)],
                                  gsem.at[buf_slot]).start(priority=j % 2)

    def wait_gather(buf_slot):
        pltpu.make_async_copy(hp_ref.at[pl.ds(0, block_rows)], xbuf.at[buf_slot], gsem.at[buf_slot]).wait()

    def wait_scatter(buf_slot):
        pltpu.make_async_copy(obuf.at[buf_slot], yg_ref.at[pl.ds(0, block_rows)], ssem.at[buf_slot]).wait()

    @pl.when(blk == 0)
    def _():
        prev_ref[0] = -1
        obuf[...] = jnp.zeros_like(obuf)
        n_real = yg_ref.shape[0] - 2 * block_rows
        for s in range(2):
            spare = pltpu.make_async_copy(obuf.at[s], yg_ref.at[pl.ds(n_real + s * block_rows, block_rows)], ssem.at[s])
            spare.start()
            spare.wait()

    def compute(with_gather):
        wait_gather(c_slot)
        for s, (lo, hi) in enumerate(_load_packed(xbuf, (c_slot,), MOE_ROWS, d)):
            xs_ref[:, s * LANES:(s + 1) * LANES] = lo.astype(BF16)
            xs_ref[:, d // 2 + s * LANES:d // 2 + (s + 1) * LANES] = hi.astype(BF16)
        if with_gather:
            gather(g_slot)
        h1 = _dot(xs_ref[...], w1b[...])
        h3 = _dot(xs_ref[...], w3b[...])
        y = _dot((_silu(h1) * h3).astype(BF16), w2b[...])
        _store_packed(obuf, (c_slot,), y)
        for j in range(MOE_ROWS):
            pltpu.make_async_copy(obuf.at[c_slot, pl.ds(j * per_token, per_token)],
                                  yg_ref.at[token_rows(dst_ref[0, 0, j])], ssem.at[c_slot]).start(priority=j % 2)

        @pl.when(blk >= 2)
        def _():
            wait_scatter(g_slot)

    @pl.when(jnp.logical_and(blk == 0, n_used > 0))
    def _():
        gather(g_slot)

    @pl.when(jnp.logical_and(blk >= 1, blk <= n_used))
    def _():
        e = be_ref[blk]

        @pl.when(e != prev_ref[0])
        def _():
            w1b[...] = w1_ref[0, 0].astype(BF16)
            w3b[...] = w3_ref[0, 0].astype(BF16)
            w2b[...] = w2_ref[0, 0].astype(BF16)
            prev_ref[0] = e

        @pl.when(blk < n_used)
        def _():
            compute(True)

        @pl.when(blk == n_used)
        def _():
            compute(False)
            wait_scatter(c_slot)


def _experts(hp, slot_src, slot_dst, block_expert, n_used, w1, w3, w2, layer, n_out_tokens):
    d, ff = w1.shape[-2:]
    per_token = d // 2 // LANES
    n_blocks = slot_src.shape[0]
    last = n_blocks - 1

    def wmap(i, be, nu):
        return (layer, be[i], 0, 0)

    buf = pltpu.VMEM((2, MOE_ROWS * per_token, LANES), jnp.uint32)
    grid_spec = pltpu.PrefetchScalarGridSpec(
        num_scalar_prefetch=2,
        grid=(n_blocks + 1,),
        in_specs=[pl.BlockSpec((1, 1, MOE_ROWS), lambda i, be, nu: (jnp.minimum(i, last), 0, 0),
                               memory_space=pltpu.SMEM),
                  pl.BlockSpec((1, 1, MOE_ROWS), lambda i, be, nu: (jnp.maximum(i - 1, 0), 0, 0),
                               memory_space=pltpu.SMEM),
                  pl.BlockSpec(memory_space=pl.ANY),
                  pl.BlockSpec((1, 1, d, ff), wmap), pl.BlockSpec((1, 1, d, ff), wmap),
                  pl.BlockSpec((1, 1, ff, d), wmap)],
        out_specs=pl.BlockSpec(memory_space=pl.ANY),
        scratch_shapes=[buf, buf, pltpu.VMEM((MOE_ROWS, d), BF16),
                        pltpu.SemaphoreType.DMA((2,)), pltpu.SemaphoreType.DMA((2,)),
                        pltpu.VMEM((d, ff), BF16), pltpu.VMEM((d, ff), BF16), pltpu.VMEM((ff, d), BF16),
                        pltpu.SMEM((1,), jnp.int32)],
    )
    be_steps = jnp.concatenate([block_expert[:1], block_expert])
    return pl.pallas_call(
        functools.partial(_expert_kernel, per_token=per_token),
        grid_spec=grid_spec,
        out_shape=jax.ShapeDtypeStruct((n_out_tokens * per_token, LANES), jnp.uint32),
        compiler_params=_cparams(("arbitrary",)),
        name="experts",
    )(be_steps, n_used, slot_src * per_token, slot_dst * per_token, hp, w1, w3, w2)


def _combine_kernel(*refs, alpha, emit_h):
    ys_refs, refs = refs[:TOP_K], refs[TOP_K:]
    (gate_ref, h_ref, ws1_ref, ws3_ref, ws2_ref, x_ref, gt_ref, g_ref, b_ref) = refs[:9]
    hidden = _silu(_dot(h_ref[...], ws1_ref[...])) * _dot(h_ref[...], ws3_ref[...])
    f = _dot(hidden.astype(BF16), ws2_ref[...])
    gate = gate_ref[...]
    tm, d = f.shape
    routed = None
    for k in range(TOP_K):
        chunks = _load_packed(ys_refs[k], (), tm, d)
        term = jnp.concatenate([lo for lo, _ in chunks] + [hi for _, hi in chunks], axis=1) * gate[:, k:k + 1]
        routed = term if routed is None else routed + term
    f = f + routed
    x2 = _layernorm(alpha * x_ref[...] + gt_ref[0, 0] * f) * g_ref[...] + b_ref[...]
    if emit_h:
        sc_ref, sh_ref, x2_ref, hn_ref = refs[9:]
        x2_ref[...] = x2
        hn_ref[...] = (x2 * (1.0 + sc_ref[0, 0]) + sh_ref[0, 0]).astype(BF16)
    else:
        refs[9][...] = x2


def _combine(rows, n_rows, yg, gate, h2, ws1, ws3, ws2, x1, mod_l, ln_g, ln_b, alpha, mod_next):
    d = x1.shape[1]
    sf = ws1.shape[1]
    tm = rows.tm
    k_blocks = n_rows // tm
    rspec = pl.BlockSpec((tm, d), lambda i: (i, 0))
    per_token = d // 2 // LANES
    in_specs = [pl.BlockSpec((tm * per_token, LANES), lambda i, k=k: (k * k_blocks + i, 0)) for k in range(TOP_K)]
    in_specs += [pl.BlockSpec((tm, TOP_K), lambda i: (i, 0)), rspec,
                 pl.BlockSpec((d, sf), lambda i: (0, 0)), pl.BlockSpec((d, sf), lambda i: (0, 0)),
                 pl.BlockSpec((sf, d), lambda i: (0, 0)), rspec, _mod_spec(rows, d, 5), _vec_spec(d), _vec_spec(d)]
    args = [yg] * TOP_K + [gate, h2, ws1, ws3, ws2, x1, mod_l, ln_g, ln_b]
    emit_h = mod_next is not None
    if emit_h:
        in_specs += [_mod_spec(rows, d, 1), _mod_spec(rows, d, 0)]
        args += [mod_next, mod_next]
        out_specs = [rspec, rspec]
        out_shape = [jax.ShapeDtypeStruct((n_rows, d), F32), jax.ShapeDtypeStruct((n_rows, d), BF16)]
    else:
        out_specs = rspec
        out_shape = jax.ShapeDtypeStruct((n_rows, d), F32)
    return pl.pallas_call(
        functools.partial(_combine_kernel, alpha=alpha, emit_h=emit_h),
        grid=(n_rows // tm,),
        in_specs=in_specs,
        out_specs=out_specs,
        out_shape=out_shape,
        compiler_params=_cparams(("parallel",)),
        name="moe_combine_norm",
    )(*args)


def _permute_w_in(w):
    a_end = A_Q + 2 * A_KV
    b_end = a_end + B_Q + 2 * B_KV
    c_end = b_end + C_CONV + C_V
    ab_end = c_end + 4 * C_V_HEADS
    main = jnp.concatenate([w[:, b_end:c_end], w[:, :b_end]], axis=1).astype(BF16)
    ab = jnp.pad(w[:, c_end:ab_end], ((0, 0), (0, AB_W - 4 * C_V_HEADS))).astype(BF16)
    gates = w[:, ab_end:].astype(BF16)
    return main, gates, ab


def kernel(x, c, ctx, c_ctx, w_mod, b_mod, w_in, q_norm_a, k_norm_a, sink_b, conv_c, a_log_c, dt_bias_c, norm_c,
           w_br_a, w_br_b, w_br_c, w_out, ln1_g, ln1_b, w_router, router_bias, w1, w3, w2, ws1, ws3, ws2,
           ln2_g, ln2_b):
    b, s, d = x.shape
    n_ctx = ctx.shape[1]
    depth = w_mod.shape[0]
    n_exp = w_router.shape[-1]
    alpha = (2.0 * depth) ** 0.25
    rows = _Rows(b, s, n_ctx)
    tm = rows.tm

    c_all = jnp.concatenate([c, c_ctx[None, :], jnp.zeros((8 - b - 1, d), F32)], axis=0)
    mod = _modulation(c_all, w_mod, b_mod).reshape(depth, 8, N_MOD, 1, d)
    tables = _rope_tables(s, tm)

    x_all = jnp.concatenate([x.reshape(b * s, d), ctx.reshape(b * n_ctx, d)], axis=0)
    x_cur, h = _entry(rows, x_all, mod[0])

    for l in range(depth):
        need_ctx = l < depth - 1
        n_rows = rows.r if need_ctx else rows.n_lat
        mod_l = mod[l]
        w_main, w_gates, w_ab = _permute_w_in(w_in[l])
        proj_main = _matmul(h, w_main, F32)
        gates = _matmul(h, w_gates, F32)
        proj_ab = _matmul(h, w_ab, F32)

        qkv = _attn_prep(rows, proj_main, tables, q_norm_a[l].reshape(1, -1), k_norm_a[l].reshape(1, -1))
        a_k, a_v = A_Q, A_Q + A_KV
        o_a = _attn_full(qkv, batch=b, n_q=s, q_base=0, sources=[(rows.n_lat, n_ctx), (0, s)],
                         q_col=0, k_col=a_k, v_col=a_v, heads=A_HEADS, kv_heads=A_KV_HEADS)
        o_b = _attn_window(qkv, sink_b[l], rows)
        if need_ctx:
            b_q = A_Q + 2 * A_KV
            o_a_c = _attn_full(qkv, batch=b, n_q=n_ctx, q_base=rows.n_lat, sources=[(rows.n_lat, n_ctx)],
                               q_col=0, k_col=a_k, v_col=a_v, heads=A_HEADS, kv_heads=A_KV_HEADS)
            o_b_c = _attn_full(qkv, batch=b, n_q=n_ctx, q_base=rows.n_lat, sources=[(rows.n_lat, n_ctx)],
                               q_col=b_q, k_col=b_q + B_Q, v_col=b_q + B_Q + B_KV, heads=B_HEADS,
                               kv_heads=B_KV_HEADS, sink=sink_b[l])
            o_a = jnp.concatenate([o_a, o_a_c], axis=0)
            o_b = jnp.concatenate([o_b, o_b_c], axis=0)

        dqkv, gbeta = _delta_prep(rows, proj_main, proj_ab, conv_c[l], a_log_c[l], dt_bias_c[l])
        o_f, o_r = _delta_scan(rows, dqkv, gbeta)

        y = _merge(n_rows, tm, o_a, o_b, o_f, o_r, proj_main, gates, norm_c[l].reshape(1, -1),
                   w_br_a[l].astype(BF16), w_br_b[l].astype(BF16), w_br_c[l].astype(BF16))
        x1, h2, h2p = _outproj(rows, n_rows, y, w_out[l].astype(BF16), x_cur, mod_l,
                          ln1_g[l].reshape(1, -1), ln1_b[l].reshape(1, -1), alpha)

        idx, gate, rank, counts = _router(rows, n_rows, x1, mod_l, w_router[l], router_bias[l])
        counts = counts[0].astype(jnp.int32)
        padded = (counts + MOE_ROWS - 1) // MOE_ROWS * MOE_ROWS
        padded_end = jnp.cumsum(padded)
        group_start = padded_end - padded
        experts_iota = jnp.arange(n_exp, dtype=jnp.int32)
        dest = rank + jnp.sum(jnp.where(idx[:, :, None] == experts_iota, group_start, 0), axis=-1)
        n_blocks = -(-(n_rows * TOP_K) // MOE_ROWS) + n_exp
        block_start = jnp.arange(n_blocks, dtype=jnp.int32)[:, None] * MOE_ROWS
        block_expert = jnp.minimum(jnp.sum((padded_end[None, :] <= block_start).astype(jnp.int32), axis=-1), n_exp - 1)
        n_used = (padded_end[-1:] // MOE_ROWS).astype(jnp.int32)
        n_slots = n_blocks * MOE_ROWS
        code = (jnp.arange(n_rows, dtype=jnp.int32)[:, None] * TOP_K + jnp.arange(TOP_K, dtype=jnp.int32)[None, :])
        slot_code = jnp.full((n_slots,), -1, jnp.int32).at[dest.reshape(-1)].set(
            code.reshape(-1), unique_indices=True, mode="promise_in_bounds")
        filled = slot_code >= 0
        slot_iota = jnp.arange(n_slots, dtype=jnp.int32)
        slot_src = jnp.where(filled, slot_code // TOP_K, 0)
        slot_dst = jnp.where(filled, (slot_code % TOP_K) * n_rows + slot_code // TOP_K,
                             TOP_K * n_rows + slot_iota % (2 * MOE_ROWS))
        yg = _experts(h2p, slot_src.reshape(n_blocks, 1, MOE_ROWS), slot_dst.reshape(n_blocks, 1, MOE_ROWS),
                      block_expert, n_used, w1, w3, w2, l, TOP_K * n_rows + 2 * MOE_ROWS)

        mod_next = mod[l + 1] if need_ctx else None
        out = _combine(rows, n_rows, yg, gate, h2, ws1[l].astype(BF16), ws3[l].astype(BF16), ws2[l].astype(BF16),
                       x1, mod_l, ln2_g[l].reshape(1, -1), ln2_b[l].reshape(1, -1), alpha, mod_next)
        if need_ctx:
            x_cur, h = out
        else:
            x_cur = out
    return x_cur[:rows.n_lat].reshape(b, s, d)
```

```python
import functools
import math

import jax
import jax.numpy as jnp
from jax import lax
from jax.experimental import pallas as pl
from jax.experimental.pallas import tpu as pltpu

F32 = jnp.float32
BF16 = jnp.bfloat16
HIGHEST = lax.Precision.HIGHEST

HEAD_DIM = 128
GRID_W = 64
ROPE_THETA = 10000.0
A_HEADS, A_KV_HEADS = 8, 2
B_HEADS, B_KV_HEADS = 8, 2
WINDOW = 128
Q_BLOCK = 128
C_QK_HEADS, C_V_HEADS = 4, 8
CONV_K = 5
CHUNK = 64
TOP_K = 8
ROUTED_SCALE = 2.5
LN_EPS = 1e-5
RMS_EPS = 1e-6
LOG2_E = 1.4426950408889634
MOE_ROWS = 256
N_MOD = 6
HALO = 8
VMEM_LIMIT = 56 * 2**20

A_Q, A_KV = A_HEADS * HEAD_DIM, A_KV_HEADS * HEAD_DIM
B_Q, B_KV = B_HEADS * HEAD_DIM, B_KV_HEADS * HEAD_DIM
C_QK, C_V = C_QK_HEADS * HEAD_DIM, C_V_HEADS * HEAD_DIM
C_CONV = 2 * C_QK + C_V
ATT_W = A_Q + 2 * A_KV + B_Q + 2 * B_KV
MAIN_W = C_CONV + C_V + ATT_W
AB_W = 128


def _cparams(sem):
    return pltpu.CompilerParams(dimension_semantics=sem, vmem_limit_bytes=VMEM_LIMIT)


def _pick(n, prefs):
    for p in prefs:
        if n % p == 0:
            return p
    return n


def _dot(a, b, **kw):
    return jnp.dot(a, b, preferred_element_type=F32, **kw)


def _dot_nt(a, b):
    return lax.dot_general(a, b, (((1,), (1,)), ((), ())), preferred_element_type=F32)


def _dot_tn(a, b):
    return lax.dot_general(a, b, (((0,), (0,)), ((), ())), preferred_element_type=F32)


def _silu(x):
    return x * jax.nn.sigmoid(x)


def _mod_kernel(c_ref, w_ref, b_ref, o_ref):
    o_ref[0] = _dot(_silu(c_ref[...]), w_ref[0], precision=HIGHEST) + b_ref[0]


def _modulation(c_all, w_mod, b_mod):
    depth, d, width = w_mod.shape
    tn = _pick(width, (1024, 512, 256, 128))
    return pl.pallas_call(
        _mod_kernel,
        grid=(depth, width // tn),
        in_specs=[pl.BlockSpec((8, d), lambda l, j: (0, 0)),
                  pl.BlockSpec((1, d, tn), lambda l, j: (l, 0, j)),
                  pl.BlockSpec((1, 1, tn), lambda l, j: (l, 0, j))],
        out_specs=pl.BlockSpec((1, 8, tn), lambda l, j: (l, 0, j)),
        out_shape=jax.ShapeDtypeStruct((depth, 8, width), F32),
        compiler_params=_cparams(("parallel", "parallel")),
        name="modulation",
    )(c_all, w_mod, b_mod.reshape(depth, 1, width))


class _Rows:
    def __init__(self, b, s, c):
        self.b, self.s, self.c = b, s, c
        self.tm = 256 if c % 256 == 0 else 128
        self.n_lat = b * s
        self.r = b * s + b * c
        self.lat_blocks = self.n_lat // self.tm
        self.blocks = self.r // self.tm

    def group(self, i):
        return jnp.where(i < self.lat_blocks, i // (self.s // self.tm), self.b)


def _mod_spec(rows, d, which):
    return pl.BlockSpec((1, 1, 1, d), lambda i: (rows.group(i), which, 0, 0))


def _vec_spec(d):
    return pl.BlockSpec((1, d), lambda i: (0, 0))


def _layernorm(x):
    mu = jnp.mean(x, axis=-1, keepdims=True)
    xc = x - mu
    var = jnp.mean(xc * xc, axis=-1, keepdims=True)
    return xc * lax.rsqrt(var + LN_EPS)


def _entry_kernel(x_ref, sc_ref, sh_ref, x0_ref, h_ref):
    xn = _layernorm(x_ref[...])
    x0_ref[...] = xn
    h_ref[...] = (xn * (1.0 + sc_ref[0, 0]) + sh_ref[0, 0]).astype(BF16)


def _entry(rows, x_all, mod_l):
    r, d = x_all.shape
    tm = rows.tm
    row_spec = pl.BlockSpec((tm, d), lambda i: (i, 0))
    return pl.pallas_call(
        _entry_kernel,
        grid=(r // tm,),
        in_specs=[row_spec, _mod_spec(rows, d, 1), _mod_spec(rows, d, 0)],
        out_specs=[row_spec, row_spec],
        out_shape=[jax.ShapeDtypeStruct((r, d), F32), jax.ShapeDtypeStruct((r, d), BF16)],
        compiler_params=_cparams(("parallel",)),
        name="entry_norm",
    )(x_all, mod_l, mod_l)


def _mm_kernel(a_ref, w_ref, o_ref):
    o_ref[...] = _dot(a_ref[...], w_ref[...]).astype(o_ref.dtype)


def _matmul(a, w, out_dtype):
    m, k = a.shape
    n = w.shape[1]
    tm = _pick(m, (1024, 512, 256, 128))
    tn = _pick(n, (1024, 512, 256, 128))
    return pl.pallas_call(
        _mm_kernel,
        grid=(n // tn, m // tm),
        in_specs=[pl.BlockSpec((tm, k), lambda j, i: (i, 0)),
                  pl.BlockSpec((k, tn), lambda j, i: (0, j))],
        out_specs=pl.BlockSpec((tm, tn), lambda j, i: (i, j)),
        out_shape=jax.ShapeDtypeStruct((m, n), out_dtype),
        compiler_params=_cparams(("parallel", "parallel")),
        name="in_proj",
    )(a, w)


def _rope(t, cos, sin_lo, sin_hi):
    return t * cos + pltpu.roll(t, HEAD_DIM - 32, 1) * sin_lo + pltpu.roll(t, 32, 1) * sin_hi


def _rms(t, g):
    return t * lax.rsqrt(jnp.mean(t * t, axis=-1, keepdims=True) + RMS_EPS) * g


def _attn_prep_kernel(x_ref, cos_ref, slo_ref, shi_ref, qn_ref, kn_ref, o_ref):
    cos, slo, shi = cos_ref[...], slo_ref[...], shi_ref[...]
    scale = HEAD_DIM ** -0.5 * LOG2_E

    def head(j):
        return x_ref[:, j * HEAD_DIM:(j + 1) * HEAD_DIM]

    def put(j, v):
        o_ref[:, j * HEAD_DIM:(j + 1) * HEAD_DIM] = v.astype(BF16)

    j = 0
    for _ in range(A_HEADS):
        put(j, _rope(_rms(head(j), qn_ref[...]), cos, slo, shi) * scale)
        j += 1
    for _ in range(A_KV_HEADS):
        put(j, _rope(_rms(head(j), kn_ref[...]), cos, slo, shi))
        j += 1
    for _ in range(A_KV_HEADS):
        put(j, head(j))
        j += 1
    for _ in range(B_HEADS):
        put(j, _rope(head(j), cos, slo, shi) * scale)
        j += 1
    for _ in range(B_KV_HEADS):
        put(j, _rope(head(j), cos, slo, shi))
        j += 1
    for _ in range(B_KV_HEADS):
        put(j, head(j))
        j += 1


def _attn_prep(rows, proj_main, tables, q_norm, k_norm):
    r = proj_main.shape[0]
    tm = rows.tm
    pos_blocks = rows.s // tm
    col_blk = (C_CONV + C_V) // ATT_W
    assert col_blk * ATT_W == C_CONV + C_V

    def tab_map(i):
        return (jnp.where(i < rows.lat_blocks, i % pos_blocks, pos_blocks), 0)

    tab_spec = pl.BlockSpec((tm, HEAD_DIM), tab_map)
    return pl.pallas_call(
        _attn_prep_kernel,
        grid=(r // tm,),
        in_specs=[pl.BlockSpec((tm, ATT_W), lambda i: (i, col_blk)), tab_spec, tab_spec, tab_spec,
                  _vec_spec(HEAD_DIM), _vec_spec(HEAD_DIM)],
        out_specs=pl.BlockSpec((tm, ATT_W), lambda i: (i, 0)),
        out_shape=jax.ShapeDtypeStruct((r, ATT_W), BF16),
        compiler_params=_cparams(("parallel",)),
        name="attn_prep",
    )(proj_main, *tables, q_norm, k_norm)


def _rope_tables(s, tm):
    row = jnp.repeat(jnp.arange(s // GRID_W, dtype=F32), GRID_W)
    col = jnp.tile(jnp.arange(GRID_W, dtype=F32), s // GRID_W)
    axis_dim = HEAD_DIM // 2
    inv_freq = ROPE_THETA ** (-jnp.arange(0, axis_dim, 2, dtype=F32) / axis_dim)
    ang_r = row[:, None] * inv_freq[None, :]
    ang_c = col[:, None] * inv_freq[None, :]
    ang = jnp.concatenate([ang_r, ang_r, ang_c, ang_c], axis=-1)
    cos, sin = jnp.cos(ang), jnp.sin(ang)
    lo = (jnp.arange(HEAD_DIM) % axis_dim) < (axis_dim // 2)
    sin_lo = jnp.where(lo, -sin, 0.0)
    sin_hi = jnp.where(lo, 0.0, sin)
    one, zero = jnp.ones((tm, HEAD_DIM), F32), jnp.zeros((tm, HEAD_DIM), F32)
    return (jnp.concatenate([cos, one]), jnp.concatenate([sin_lo, zero]), jnp.concatenate([sin_hi, zero]))


def _attn_full_kernel(*refs, n_src, has_sink, groups):
    if has_sink:
        sink_ref, refs = refs[0], refs[1:]
    q_ref, o_ref = refs[0], refs[1 + 2 * n_src]
    kv = refs[1:1 + 2 * n_src]
    kvh = pl.program_id(1)
    for g in range(groups):
        q = q_ref[:, g * HEAD_DIM:(g + 1) * HEAD_DIM]
        scores = [_dot_nt(q, kv[2 * j][...]) for j in range(n_src)]
        m = functools.reduce(jnp.maximum, [jnp.max(s, axis=-1, keepdims=True) for s in scores])
        if has_sink:
            sink = sink_ref[kvh * groups + g] * LOG2_E
            m = jnp.maximum(m, sink)
        es = [jnp.exp2(s - m) for s in scores]
        den = functools.reduce(jnp.add, [jnp.sum(e, axis=-1, keepdims=True) for e in es])
        if has_sink:
            den = den + jnp.exp2(sink - m)
        acc = functools.reduce(jnp.add, [_dot(es[j].astype(BF16), kv[2 * j + 1][...]) for j in range(n_src)])
        o_ref[:, g * HEAD_DIM:(g + 1) * HEAD_DIM] = (acc / den).astype(o_ref.dtype)


def _attn_full(qkv, *, batch, n_q, q_base, sources, q_col, k_col, v_col, heads, kv_heads, sink=None):
    groups = heads // kv_heads
    tq = _pick(n_q, (256, 128))
    qw = groups * HEAD_DIM
    nq = n_q // tq
    in_specs = []
    args = []
    if sink is not None:
        in_specs.append(pl.BlockSpec(memory_space=pltpu.SMEM))
        args.append(sink)
    in_specs.append(pl.BlockSpec((tq, qw), lambda b, h, i: (q_base // tq + b * nq + i, q_col // qw + h)))
    args.append(qkv)
    for base, n in sources:
        for col in (k_col, v_col):
            in_specs.append(pl.BlockSpec(
                (n, HEAD_DIM), lambda b, h, i, base=base, n=n, col=col: (base // n + b, col // HEAD_DIM + h)))
            args.append(qkv)
    kern = functools.partial(_attn_full_kernel, n_src=len(sources), has_sink=sink is not None, groups=groups)
    return pl.pallas_call(
        kern,
        grid=(batch, kv_heads, nq),
        in_specs=in_specs,
        out_specs=pl.BlockSpec((tq, qw), lambda b, h, i: (b * nq + i, h)),
        out_shape=jax.ShapeDtypeStruct((batch * n_q, heads * HEAD_DIM), BF16),
        compiler_params=_cparams(("parallel", "parallel", "arbitrary")),
        name="attn_full",
    )(*args)


def _attn_win_kernel(sink_ref, q_ref, k_ref, v_ref, kc_ref, vc_ref, o_ref, *, groups, seq):
    kvh = pl.program_id(1)
    n = pl.program_id(2)
    span = 3 * Q_BLOCK
    start = jnp.clip((n - 1) * Q_BLOCK, 0, seq - span)
    start = pl.multiple_of(start, Q_BLOCK)
    kb = k_ref[pl.ds(start, span), :]
    vb = v_ref[pl.ds(start, span), :]
    q_pos = n * Q_BLOCK + lax.broadcasted_iota(jnp.int32, (Q_BLOCK, span), 0)
    k_pos = start + lax.broadcasted_iota(jnp.int32, (Q_BLOCK, span), 1)
    valid = jnp.abs(k_pos - q_pos) <= WINDOW
    for g in range(groups):
        q = q_ref[:, g * HEAD_DIM:(g + 1) * HEAD_DIM]
        s_win = jnp.where(valid, _dot_nt(q, kb), -jnp.inf)
        s_ctx = _dot_nt(q, kc_ref[...])
        sink = sink_ref[kvh * groups + g] * LOG2_E
        m = jnp.maximum(jnp.maximum(jnp.max(s_win, axis=-1, keepdims=True),
                                    jnp.max(s_ctx, axis=-1, keepdims=True)), sink)
        e_win = jnp.exp2(s_win - m)
        e_ctx = jnp.exp2(s_ctx - m)
        den = (jnp.sum(e_win, axis=-1, keepdims=True) + jnp.sum(e_ctx, axis=-1, keepdims=True)
               + jnp.exp2(sink - m))
        acc = _dot(e_win.astype(BF16), vb) + _dot(e_ctx.astype(BF16), vc_ref[...])
        o_ref[:, g * HEAD_DIM:(g + 1) * HEAD_DIM] = (acc / den).astype(o_ref.dtype)


def _attn_window(qkv, sink, rows):
    b, s, c = rows.b, rows.s, rows.c
    groups = B_HEADS // B_KV_HEADS
    qw = groups * HEAD_DIM
    nq = s // Q_BLOCK
    q_col = A_Q + 2 * A_KV
    k_col = q_col + B_Q
    v_col = k_col + B_KV
    ctx_base = rows.n_lat // c

    def kv_spec(n, base, col):
        return pl.BlockSpec((n, HEAD_DIM), lambda bb, h, i: (base + bb, col // HEAD_DIM + h))

    return pl.pallas_call(
        functools.partial(_attn_win_kernel, groups=groups, seq=s),
        grid=(b, B_KV_HEADS, nq),
        in_specs=[pl.BlockSpec(memory_space=pltpu.SMEM),
                  pl.BlockSpec((Q_BLOCK, qw), lambda bb, h, i: (bb * nq + i, q_col // qw + h)),
                  kv_spec(s, 0, k_col), kv_spec(s, 0, v_col),
                  kv_spec(c, ctx_base, k_col), kv_spec(c, ctx_base, v_col)],
        out_specs=pl.BlockSpec((Q_BLOCK, qw), lambda bb, h, i: (bb * nq + i, h)),
        out_shape=jax.ShapeDtypeStruct((b * s, B_Q), BF16),
        compiler_params=_cparams(("parallel", "parallel", "arbitrary")),
        name="attn_window",
    )(sink, qkv, qkv, qkv, qkv, qkv)


def _delta_prep_kernel(prev_ref, cur_ref, next_ref, ab_ref, w_ref, alog_ref, dtb_ref, qkv_ref, gb_ref,
                       *, lat_blocks, seq_blocks, ctx_blocks):
    i = pl.program_id(0)
    tm = cur_ref.shape[0]
    is_lat = i < lat_blocks
    pos = jnp.where(is_lat, i % seq_blocks, (i - lat_blocks) % ctx_blocks)
    n_pos = jnp.where(is_lat, seq_blocks, ctx_blocks)
    first = pos == 0
    last = pos == n_pos - 1
    keep_prev = jnp.where(first, 0.0, 1.0)
    keep_next = jnp.where(last, 0.0, 1.0)
    ext = tm + 2 * HALO
    n_qk = 2 * C_QK_HEADS
    for j in range(C_CONV // HEAD_DIM):
        sl = slice(j * HEAD_DIM, (j + 1) * HEAD_DIM)
        xt = jnp.concatenate([prev_ref[:, sl] * keep_prev, cur_ref[:, sl], next_ref[:, sl] * keep_next], axis=0)
        acc = None
        for tap in range(CONV_K):
            shift = (CONV_K // 2 - tap) % ext
            xs = xt if shift == 0 else pltpu.roll(xt, shift, 0)
            term = xs[HALO:HALO + tm] * w_ref[tap:tap + 1, sl]
            acc = term if acc is None else acc + term
        y = _silu(acc)
        if j < n_qk:
            y = y * lax.rsqrt(jnp.sum(y * y, axis=-1, keepdims=True) + RMS_EPS)
            if j < C_QK_HEADS:
                y = y * (HEAD_DIM ** -0.5)
        qkv_ref[:, sl] = y
    ab = ab_ref[...]
    lane = lax.broadcasted_iota(jnp.int32, ab.shape, 1)
    n_g = 2 * C_V_HEADS
    z = ab + dtb_ref[...]
    softplus = jnp.maximum(z, 0.0) + jnp.log(1.0 + jnp.exp(-jnp.abs(z)))
    g = -jnp.exp(alog_ref[...]) * softplus
    gb_ref[...] = jnp.where(lane < n_g, g, jax.nn.sigmoid(ab))


def _delta_prep(rows, proj_main, proj_ab, conv_w, a_log, dt_bias):
    r = proj_main.shape[0]
    tm = rows.tm
    hb = tm // HALO
    last_halo = r // HALO - 1
    pad = AB_W - a_log.size
    alog = jnp.pad(a_log.reshape(1, -1), ((0, 0), (0, pad)))
    dtb = jnp.pad(dt_bias.reshape(1, -1), ((0, 0), (0, pad)))
    kern = functools.partial(_delta_prep_kernel, lat_blocks=rows.lat_blocks, seq_blocks=rows.s // tm,
                             ctx_blocks=rows.c // tm)
    return pl.pallas_call(
        kern,
        grid=(r // tm,),
        in_specs=[pl.BlockSpec((HALO, C_CONV), lambda i: (jnp.maximum(i * hb - 1, 0), 0)),
                  pl.BlockSpec((tm, C_CONV), lambda i: (i, 0)),
                  pl.BlockSpec((HALO, C_CONV), lambda i: (jnp.minimum((i + 1) * hb, last_halo), 0)),
                  pl.BlockSpec((tm, AB_W), lambda i: (i, 0)),
                  pl.BlockSpec((CONV_K, C_CONV), lambda i: (0, 0)),
                  _vec_spec(AB_W), _vec_spec(AB_W)],
        out_specs=[pl.BlockSpec((tm, C_CONV), lambda i: (i, 0)), pl.BlockSpec((tm, AB_W), lambda i: (i, 0))],
        out_shape=[jax.ShapeDtypeStruct((r, C_CONV), F32), jax.ShapeDtypeStruct((r, AB_W), F32)],
        compiler_params=_cparams(("parallel",)),
        name="delta_prep",
    )(proj_main, proj_main, proj_main, proj_ab, conv_w, alog, dtb)


def _delta_kernel(qkv_f_ref, gb_f_ref, qkv_r_ref, gb_r_ref, o_f_ref, o_r_ref, state_ref):
    @pl.when(pl.program_id(1) == 0)
    def _():
        state_ref[...] = jnp.zeros_like(state_ref)

    row = lax.broadcasted_iota(jnp.int32, (CHUNK, CHUNK), 0)
    col = lax.broadcasted_iota(jnp.int32, (CHUNK, CHUNK), 1)
    eye = jnp.where(row == col, 1.0, 0.0)
    rep = C_V_HEADS // C_QK_HEADS
    n_sq = int(math.log2(CHUNK)) - 1
    chains = []
    for direction, (qkv_ref, gb_ref, o_ref) in enumerate(((qkv_f_ref, gb_f_ref, o_f_ref),
                                                          (qkv_r_ref, gb_r_ref, o_r_ref))):
        if direction == 0:
            incl, strict, last = row >= col, row > col, CHUNK - 1
        else:
            incl, strict, last = row <= col, row < col, 0
        gb = gb_ref[...]
        gcum = _dot(jnp.where(incl, 1.0, 0.0), gb, precision=HIGHEST)
        gcum_t = gcum.T
        kq = []
        for qh in range(C_QK_HEADS):
            q = qkv_ref[:, qh * HEAD_DIM:(qh + 1) * HEAD_DIM]
            k = qkv_ref[:, C_QK + qh * HEAD_DIM:C_QK + (qh + 1) * HEAD_DIM]
            prod = _dot_nt(jnp.concatenate([k, q], axis=0).astype(BF16), k.astype(BF16))
            kq.append((q, k, prod[:CHUNK], prod[CHUNK:]))
        for h in range(C_V_HEADS):
            lane = direction * C_V_HEADS + h
            q, k, kk, qk = kq[h // rep]
            g_col = gcum[:, lane:lane + 1]
            beta = gb[:, 2 * C_V_HEADS + lane:2 * C_V_HEADS + lane + 1]
            decay = jnp.where(incl, jnp.exp(jnp.minimum(g_col - gcum_t[lane:lane + 1, :], 0.0)), 0.0)
            v = qkv_ref[:, 2 * C_QK + h * HEAD_DIM:2 * C_QK + (h + 1) * HEAD_DIM]
            p = jnp.where(strict, -(kk * beta) * decay, 0.0)
            chains.append(dict(
                h=h, slot=lane, o_ref=o_ref, q=q, k=k, g_col=g_col, e_g=jnp.exp(g_col),
                g_last=gcum[last:last + 1, lane:lane + 1], beta=beta, v=v, qk=qk * decay, p=p, t=eye + p))
    for _ in range(n_sq):
        for ch in chains:
            pb = ch["p"].astype(BF16)
            ch["p"] = _dot(pb, pb)
        for ch in chains:
            ch["t"] = ch["t"] + _dot(ch["t"].astype(BF16), ch["p"].astype(BF16))
    for ch in chains:
        rhs = jnp.concatenate([ch["v"] * ch["beta"], ch["k"] * (ch["beta"] * ch["e_g"])], axis=1)
        uw = _dot(ch["t"].astype(BF16), rhs.astype(BF16))
        ch["u"], ch["w"] = uw[:, :HEAD_DIM], uw[:, HEAD_DIM:]
    for ch in chains:
        ch["state"] = state_ref[ch["slot"]]
        lhs = jnp.concatenate([ch["w"], ch["q"] * ch["e_g"]], axis=0)
        ch["ws_qs"] = _dot(lhs.astype(BF16), ch["state"].astype(BF16))
    for ch in chains:
        ch["v_new"] = (ch["u"] - ch["ws_qs"][:CHUNK]).astype(BF16)
    for ch in chains:
        h = ch["h"]
        ch["o_ref"][:, h * HEAD_DIM:(h + 1) * HEAD_DIM] = ch["ws_qs"][CHUNK:] + _dot(ch["qk"].astype(BF16), ch["v_new"])
    for ch in chains:
        k_dec = ch["k"] * jnp.exp(ch["g_last"] - ch["g_col"])
        state_ref[ch["slot"]] = ch["state"] * jnp.exp(ch["g_last"]) + _dot_tn(k_dec.astype(BF16), ch["v_new"])


def _delta_scan(rows, dqkv, gb):
    b, s, c = rows.b, rows.s, rows.c
    ctx_chunks, lat_chunks = c // CHUNK, s // CHUNK
    ctx_base = rows.n_lat // CHUNK

    def fwd_map(bb, t):
        return (jnp.where(t < ctx_chunks, ctx_base + bb * ctx_chunks + t, bb * lat_chunks + t - ctx_chunks), 0)

    def rev_map(bb, t):
        ctx_i, lat_i = ctx_chunks - 1 - t, lat_chunks - 1 - (t - ctx_chunks)
        return (jnp.where(t < ctx_chunks, ctx_base + bb * ctx_chunks + ctx_i, bb * lat_chunks + lat_i), 0)

    out = jax.ShapeDtypeStruct((rows.r, C_V), F32)
    return pl.pallas_call(
        _delta_kernel,
        grid=(b, ctx_chunks + lat_chunks),
        in_specs=[pl.BlockSpec((CHUNK, C_CONV), fwd_map), pl.BlockSpec((CHUNK, AB_W), fwd_map),
                  pl.BlockSpec((CHUNK, C_CONV), rev_map), pl.BlockSpec((CHUNK, AB_W), rev_map)],
        out_specs=[pl.BlockSpec((CHUNK, C_V), fwd_map), pl.BlockSpec((CHUNK, C_V), rev_map)],
        out_shape=[out, out],
        scratch_shapes=[pltpu.VMEM((2 * C_V_HEADS, HEAD_DIM, HEAD_DIM), F32)],
        compiler_params=_cparams(("parallel", "arbitrary")),
        name="delta_scan",
    )(dqkv, gb, dqkv, gb)


def _merge_kernel(oa_ref, ob_ref, of_ref, or_ref, z_ref, nc_ref, ga_ref, gb_ref, gc_ref,
                  wa_ref, wb_ref, wc_ref, y_ref, oc_ref):
    for h in range(C_V_HEADS):
        sl = slice(h * HEAD_DIM, (h + 1) * HEAD_DIM)
        o = of_ref[:, sl] + or_ref[:, sl]
        oc_ref[:, sl] = (_rms(o, nc_ref[...]) * _silu(z_ref[:, sl])).astype(BF16)
    y = jax.nn.sigmoid(ga_ref[...]) * _dot(oa_ref[...], wa_ref[...])
    y = y + jax.nn.sigmoid(gb_ref[...]) * _dot(ob_ref[...], wb_ref[...])
    y = y + jax.nn.sigmoid(gc_ref[...]) * _dot(oc_ref[...], wc_ref[...])
    y_ref[...] = y.astype(BF16)


def _merge(n_rows, tm, o_a, o_b, o_f, o_r, proj_main, gates, norm_c, w_a, w_b, w_c):
    d = w_a.shape[1]
    z_blk = C_CONV // C_V

    def rspec(w, col=0):
        return pl.BlockSpec((tm, w), lambda i: (i, col))

    def wspec(kdim):
        return pl.BlockSpec((kdim, d), lambda i: (0, 0))

    return pl.pallas_call(
        _merge_kernel,
        grid=(n_rows // tm,),
        in_specs=[rspec(A_Q), rspec(B_Q), rspec(C_V), rspec(C_V), rspec(C_V, z_blk), _vec_spec(HEAD_DIM),
                  rspec(d, 0), rspec(d, 1), rspec(d, 2), wspec(A_Q), wspec(B_Q), wspec(C_V)],
        out_specs=rspec(d),
        out_shape=jax.ShapeDtypeStruct((n_rows, d), BF16),
        scratch_shapes=[pltpu.VMEM((tm, C_V), BF16)],
        compiler_params=_cparams(("parallel",)),
        name="branch_merge",
    )(o_a, o_b, o_f, o_r, proj_main, norm_c, gates, gates, gates, w_a, w_b, w_c)


LANES = 128


def _token_rows(ref_rows, tokens, s, per_token):
    return pl.ds(s, tokens, stride=per_token) if per_token > 1 else pl.ds(s, tokens)


def _store_packed(ref, lead, x):
    m, d = x.shape
    half = d // 2
    per_token = half // LANES
    for s in range(per_token):
        lo = lax.bitcast_convert_type(x[:, s * LANES:(s + 1) * LANES].astype(BF16).astype(F32), jnp.uint32)
        hi = lax.bitcast_convert_type(x[:, half + s * LANES:half + (s + 1) * LANES].astype(BF16).astype(F32),
                                      jnp.uint32)
        ref[(*lead, _token_rows(ref, m, s, per_token), slice(None))] = (lo >> 16) | hi


def _load_packed(ref, lead, m, d):
    per_token = d // 2 // LANES
    out = []
    for s in range(per_token):
        p = ref[(*lead, _token_rows(ref, m, s, per_token), slice(None))]
        out.append((lax.bitcast_convert_type(p << 16, F32),
                    lax.bitcast_convert_type(p & jnp.uint32(0xFFFF0000), F32)))
    return out


def _outproj_kernel(y_ref, w_ref, x_ref, gt_ref, g_ref, b_ref, sc_ref, sh_ref, x1_ref, h_ref, *, alpha):
    y = _dot(y_ref[...], w_ref[...])
    x1 = _layernorm(alpha * x_ref[...] + gt_ref[0, 0] * y) * g_ref[...] + b_ref[...]
    x1_ref[...] = x1
    h_ref[...] = (x1 * (1.0 + sc_ref[0, 0]) + sh_ref[0, 0]).astype(BF16)


def _outproj(rows, n_rows, y, w_out, x, mod_l, ln_g, ln_b, alpha):
    d = x.shape[1]
    tm = rows.tm
    rspec = pl.BlockSpec((tm, d), lambda i: (i, 0))
    return pl.pallas_call(
        functools.partial(_outproj_kernel, alpha=alpha),
        grid=(n_rows // tm,),
        in_specs=[rspec, pl.BlockSpec((d, d), lambda i: (0, 0)), rspec, _mod_spec(rows, d, 2),
                  _vec_spec(d), _vec_spec(d), _mod_spec(rows, d, 4), _mod_spec(rows, d, 3)],
        out_specs=[rspec, rspec],
        out_shape=[jax.ShapeDtypeStruct((n_rows, d), F32), jax.ShapeDtypeStruct((n_rows, d), BF16)],
        compiler_params=_cparams(("parallel",)),
        name="out_proj_norm",
    )(y, w_out, x, mod_l, ln_g, ln_b, mod_l, mod_l)


def _router_kernel(x_ref, sc_ref, sh_ref, wr_ref, rb_ref, gate_ref, pos_ref, cnt_ref, before_ref, lstart_ref,
                   total_ref, carry_ref):
    @pl.when(pl.program_id(0) == 0)
    def _():
        carry_ref[...] = jnp.zeros_like(carry_ref)

    h = x_ref[...] * (1.0 + sc_ref[0, 0]) + sh_ref[0, 0]
    scores = jax.nn.sigmoid(_dot(h, wr_ref[...], precision=HIGHEST))
    tm, n_exp = scores.shape
    sel = scores + rb_ref[...]
    lane = lax.broadcasted_iota(jnp.int32, (tm, n_exp), 1)
    slot = lax.broadcasted_iota(jnp.int32, (tm, TOP_K), 1)
    gate = jnp.zeros((tm, TOP_K), F32)
    mask = jnp.zeros((tm, n_exp), F32)
    hits = []
    for k in range(TOP_K):
        best = jnp.max(sel, axis=-1, keepdims=True)
        choice = jnp.min(jnp.where(sel == best, lane, n_exp), axis=-1, keepdims=True)
        hit = lane == choice
        hits.append(hit)
        gate = jnp.where(slot == k, jnp.sum(jnp.where(hit, scores, 0.0), axis=-1, keepdims=True), gate)
        sel = jnp.where(hit, -jnp.inf, sel)
        mask = jnp.where(hit, 1.0, mask)
    gate_ref[...] = gate / jnp.sum(gate, axis=-1, keepdims=True) * ROUTED_SCALE
    r_i = lax.broadcasted_iota(jnp.int32, (tm, tm), 0)
    c_i = lax.broadcasted_iota(jnp.int32, (tm, tm), 1)
    earlier = jnp.where(c_i < r_i, 1.0, 0.0).astype(BF16)
    within = _dot(earlier, mask.astype(BF16))
    cnt = jnp.sum(mask, axis=0, keepdims=True)
    run_rows = jnp.floor((cnt + (RUN - 1)) * (1.0 / RUN)) * RUN
    e_r = lax.broadcasted_iota(jnp.int32, (n_exp, n_exp), 0)
    e_c = lax.broadcasted_iota(jnp.int32, (n_exp, n_exp), 1)
    lstart = _dot(jnp.broadcast_to(run_rows, (8, n_exp)), jnp.where(e_r < e_c, 1.0, 0.0), precision=HIGHEST)[0:1]
    local = within + lstart
    pos = jnp.zeros((tm, TOP_K), F32)
    for k in range(TOP_K):
        pos = jnp.where(slot == k, jnp.sum(jnp.where(hits[k], local, 0.0), axis=-1, keepdims=True), pos)
    pos_ref[...] = pos.astype(jnp.int32)
    cnt_ref[0] = cnt.astype(jnp.int32)
    before_ref[0] = carry_ref[...].astype(jnp.int32)
    lstart_ref[0] = lstart.astype(jnp.int32)
    carry_ref[...] = carry_ref[...] + cnt
    total_ref[...] = carry_ref[...]


def _router(rows, n_rows, x1, mod_l, w_router, router_bias):
    d, n_exp = w_router.shape
    tm = rows.tm
    n_blk = n_rows // tm
    kspec = pl.BlockSpec((tm, TOP_K), lambda i: (i, 0))
    espec = pl.BlockSpec((1, 1, n_exp), lambda i: (i, 0, 0))
    per_block = jax.ShapeDtypeStruct((n_blk, 1, n_exp), jnp.int32)
    return pl.pallas_call(
        _router_kernel,
        grid=(n_blk,),
        in_specs=[pl.BlockSpec((tm, d), lambda i: (i, 0)), _mod_spec(rows, d, 4), _mod_spec(rows, d, 3),
                  pl.BlockSpec((d, n_exp), lambda i: (0, 0)), _vec_spec(n_exp)],
        out_specs=[kspec, kspec, espec, espec, espec, _vec_spec(n_exp)],
        out_shape=[jax.ShapeDtypeStruct((n_rows, TOP_K), F32), jax.ShapeDtypeStruct((n_rows, TOP_K), jnp.int32),
                   per_block, per_block, per_block, jax.ShapeDtypeStruct((1, n_exp), F32)],
        scratch_shapes=[pltpu.VMEM((1, n_exp), F32)],
        compiler_params=_cparams(("arbitrary",)),
        name="router",
    )(x1, mod_l, mod_l, w_router, router_bias.reshape(1, n_exp))


def _expert_kernel(be_ref, nu_ref, src_ref, dst_ref, hp_ref, w1_ref, w3_ref, w2_ref, yg_ref,
                   xbuf, obuf, xs_ref, gsem, ssem, w1b, w3b, w2b, prev_ref, *, per_token):
    blk = pl.program_id(0)
    n_used = nu_ref[0]
    g_slot = blk % 2
    c_slot = 1 - g_slot
    block_rows = MOE_ROWS * per_token
    d = xs_ref.shape[1]

    def token_rows(start):
        return pl.ds(pl.multiple_of(start, per_token), per_token)

    PROBE_G, PROBE_S = 1, 0

    def gather(buf_slot):
        for j in range(MOE_ROWS * PROBE_G):
            pltpu.make_async_copy(hp_ref.at[token_rows(src_ref[0, 0, j])],
                                  xbuf.at[buf_slot, pl.ds(j * per_token, per_token)],
                                  gsem.at[buf_slot]).start(priority=j % 2)

    def wait_gather(buf_slot):
        if not PROBE_G:
            return
        pltpu.make_async_copy(hp_ref.at[pl.ds(0, block_rows)], xbuf.at[buf_slot], gsem.at[buf_slot]).wait()

    def wait_scatter(buf_slot):
        if not PROBE_S:
            return
        pltpu.make_async_copy(obuf.at[buf_slot], yg_ref.at[pl.ds(0, block_rows)], ssem.at[buf_slot]).wait()

    @pl.when(blk == 0)
    def _():
        prev_ref[0] = -1
        obuf[...] = jnp.zeros_like(obuf)
        n_real = yg_ref.shape[0] - 2 * block_rows
        for s in range(2):
            spare = pltpu.make_async_copy(obuf.at[s], yg_ref.at[pl.ds(n_real + s * block_rows, block_rows)], ssem.at[s])
            spare.start()
            spare.wait()

    def compute(with_gather):
        wait_gather(c_slot)
        for s, (lo, hi) in enumerate(_load_packed(xbuf, (c_slot,), MOE_ROWS, d)):
            xs_ref[:, s * LANES:(s + 1) * LANES] = lo.astype(BF16)
            xs_ref[:, d // 2 + s * LANES:d // 2 + (s + 1) * LANES] = hi.astype(BF16)
        if with_gather:
            gather(g_slot)
        h1 = _dot(xs_ref[...], w1b[...])
        h3 = _dot(xs_ref[...], w3b[...])
        y = _dot((_silu(h1) * h3).astype(BF16), w2b[...])
        _store_packed(obuf, (c_slot,), y)
        for j in range(MOE_ROWS * PROBE_S):
            pltpu.make_async_copy(obuf.at[c_slot, pl.ds(j * per_token, per_token)],
                                  yg_ref.at[token_rows(dst_ref[0, 0, j])], ssem.at[c_slot]).start(priority=j % 2)

        @pl.when(blk >= 2)
        def _():
            wait_scatter(g_slot)

    @pl.when(jnp.logical_and(blk == 0, n_used > 0))
    def _():
        gather(g_slot)

    @pl.when(jnp.logical_and(blk >= 1, blk <= n_used))
    def _():
        e = be_ref[blk]

        @pl.when(e != prev_ref[0])
        def _():
            w1b[...] = w1_ref[0, 0].astype(BF16)
            w3b[...] = w3_ref[0, 0].astype(BF16)
            w2b[...] = w2_ref[0, 0].astype(BF16)
            prev_ref[0] = e

        @pl.when(blk < n_used)
        def _():
            compute(True)

        @pl.when(blk == n_used)
        def _():
            compute(False)
            wait_scatter(c_slot)


def _experts(hp, slot_src, slot_dst, block_expert, n_used, w1, w3, w2, layer, n_out_tokens):
    d, ff = w1.shape[-2:]
    per_token = d // 2 // LANES
    n_blocks = slot_src.shape[0]
    last = n_blocks - 1

    def wmap(i, be, nu):
        return (layer, be[i], 0, 0)

    buf = pltpu.VMEM((2, MOE_ROWS * per_token, LANES), jnp.uint32)
    grid_spec = pltpu.PrefetchScalarGridSpec(
        num_scalar_prefetch=2,
        grid=(n_blocks + 1,),
        in_specs=[pl.BlockSpec((1, 1, MOE_ROWS), lambda i, be, nu: (jnp.minimum(i, last), 0, 0),
                               memory_space=pltpu.SMEM),
                  pl.BlockSpec((1, 1, MOE_ROWS), lambda i, be, nu: (jnp.maximum(i - 1, 0), 0, 0),
                               memory_space=pltpu.SMEM),
                  pl.BlockSpec(memory_space=pl.ANY),
                  pl.BlockSpec((1, 1, d, ff), wmap), pl.BlockSpec((1, 1, d, ff), wmap),
                  pl.BlockSpec((1, 1, ff, d), wmap)],
        out_specs=pl.BlockSpec(memory_space=pl.ANY),
        scratch_shapes=[buf, buf, pltpu.VMEM((MOE_ROWS, d), BF16),
                        pltpu.SemaphoreType.DMA((2,)), pltpu.SemaphoreType.DMA((2,)),
                        pltpu.VMEM((d, ff), BF16), pltpu.VMEM((d, ff), BF16), pltpu.VMEM((ff, d), BF16),
                        pltpu.SMEM((1,), jnp.int32)],
    )
    be_steps = jnp.concatenate([block_expert[:1], block_expert])
    return pl.pallas_call(
        functools.partial(_expert_kernel, per_token=per_token),
        grid_spec=grid_spec,
        out_shape=jax.ShapeDtypeStruct((n_out_tokens * per_token, LANES), jnp.uint32),
        compiler_params=_cparams(("arbitrary",)),
        name="experts",
    )(be_steps, n_used, slot_src * per_token, slot_dst * per_token, hp, w1, w3, w2)


def _combine_kernel(*refs, alpha, emit_h):
    ys_refs, refs = refs[:TOP_K], refs[TOP_K:]
    (gate_ref, h_ref, ws1_ref, ws3_ref, ws2_ref, x_ref, gt_ref, g_ref, b_ref) = refs[:9]
    hidden = _silu(_dot(h_ref[...], ws1_ref[...])) * _dot(h_ref[...], ws3_ref[...])
    f = _dot(hidden.astype(BF16), ws2_ref[...])
    gate = gate_ref[...]
    tm, d = f.shape
    routed = None
    for k in range(TOP_K):
        chunks = _load_packed(ys_refs[k], (), tm, d)
        term = jnp.concatenate([lo for lo, _ in chunks] + [hi for _, hi in chunks], axis=1) * gate[:, k:k + 1]
        routed = term if routed is None else routed + term
    f = f + routed
    x2 = _layernorm(alpha * x_ref[...] + gt_ref[0, 0] * f) * g_ref[...] + b_ref[...]
    if emit_h:
        sc_ref, sh_ref, x2_ref, hn_ref = refs[9:]
        x2_ref[...] = x2
        hn_ref[...] = (x2 * (1.0 + sc_ref[0, 0]) + sh_ref[0, 0]).astype(BF16)
    else:
        refs[9][...] = x2


def _combine(rows, n_rows, yg, gate, h2, ws1, ws3, ws2, x1, mod_l, ln_g, ln_b, alpha, mod_next):
    d = x1.shape[1]
    sf = ws1.shape[1]
    tm = rows.tm
    k_blocks = n_rows // tm
    rspec = pl.BlockSpec((tm, d), lambda i: (i, 0))
    per_token = d // 2 // LANES
    in_specs = [pl.BlockSpec((tm * per_token, LANES), lambda i, k=k: (k * k_blocks + i, 0)) for k in range(TOP_K)]
    in_specs += [pl.BlockSpec((tm, TOP_K), lambda i: (i, 0)), rspec,
                 pl.BlockSpec((d, sf), lambda i: (0, 0)), pl.BlockSpec((d, sf), lambda i: (0, 0)),
                 pl.BlockSpec((sf, d), lambda i: (0, 0)), rspec, _mod_spec(rows, d, 5), _vec_spec(d), _vec_spec(d)]
    args = [yg] * TOP_K + [gate, h2, ws1, ws3, ws2, x1, mod_l, ln_g, ln_b]
    emit_h = mod_next is not None
    if emit_h:
        in_specs += [_mod_spec(rows, d, 1), _mod_spec(rows, d, 0)]
        args += [mod_next, mod_next]
        out_specs = [rspec, rspec]
        out_shape = [jax.ShapeDtypeStruct((n_rows, d), F32), jax.ShapeDtypeStruct((n_rows, d), BF16)]
    else:
        out_specs = rspec
        out_shape = jax.ShapeDtypeStruct((n_rows, d), F32)
    return pl.pallas_call(
        functools.partial(_combine_kernel, alpha=alpha, emit_h=emit_h),
        grid=(n_rows // tm,),
        in_specs=in_specs,
        out_specs=out_specs,
        out_shape=out_shape,
        compiler_params=_cparams(("parallel",)),
        name="moe_combine_norm",
    )(*args)


RUN = 16


def _local_rows(tm, n_exp):
    return tm * TOP_K + n_exp * RUN


def _run_dmas(n_exp, cnt_ref, src_start, dst_start, src_ref, dst_ref, sem, per_token):
    total = jnp.int32(0)
    rows = RUN * per_token
    for e in range(n_exp):
        n_run = (cnt_ref[0, 0, e] + (RUN - 1)) // RUN
        s0, d0 = src_start(e), dst_start(e)

        def body(c, carry, s0=s0, d0=d0):
            pltpu.make_async_copy(src_ref.at[pl.ds(pl.multiple_of((s0 + c * RUN) * per_token, per_token), rows)],
                                  dst_ref.at[pl.ds(pl.multiple_of((d0 + c * RUN) * per_token, per_token), rows)],
                                  sem).start()
            return carry

        lax.fori_loop(0, n_run, body, 0)
        total = total + n_run
    return total


def _wait_runs(n, src_ref, dst_ref, sem, per_token):
    rows = RUN * per_token

    def body(c, carry):
        pltpu.make_async_copy(src_ref.at[pl.ds(0, rows)], dst_ref.at[pl.ds(0, rows)], sem).wait()
        return carry

    lax.fori_loop(0, n, body, 0)


def _dispatch_kernel(gs_ref, count_ref, padded_ref, post_ref, cnt_ref, carry_ref, ls_ref, h_ref, xs_ref,
                     loc, zbuf, sem, pending_ref, *, per_token, n_exp):
    i = pl.program_id(0)
    tm, d = h_ref.shape
    half = d // 2
    n_local = loc.shape[1] // per_token
    chunk = 512
    buf = i % 2
    for r0 in range(0, n_local, chunk):
        p_iota = r0 + lax.broadcasted_iota(jnp.int32, (chunk, tm), 0)
        onehot = jnp.zeros((chunk, tm), F32)
        for k in range(TOP_K):
            onehot = onehot + jnp.where(p_iota == post_ref[k:k + 1, :], 1.0, 0.0)
        onehot = onehot.astype(BF16)
        lo = lax.bitcast_convert_type(_dot(onehot, h_ref[:, :half]), jnp.uint32)
        hi = lax.bitcast_convert_type(_dot(onehot, h_ref[:, half:]), jnp.uint32)
        words = (lo >> 16) | (hi & jnp.uint32(0xFFFF0000))
        for s in range(per_token):
            loc[buf, _token_rows(loc, chunk, r0 * per_token + s, per_token), :] = words[:, s * LANES:(s + 1) * LANES]

    @pl.when(i > 0)
    def _():
        _wait_runs(pending_ref[0], loc.at[0], xs_ref, sem, per_token)

    pending_ref[0] = _run_dmas(n_exp, cnt_ref, lambda e: ls_ref[0, 0, e], lambda e: gs_ref[e] + carry_ref[0, 0, e],
                               loc.at[buf], xs_ref, sem, per_token)

    @pl.when(i == pl.num_programs(0) - 1)
    def _():
        _wait_runs(pending_ref[0], loc.at[0], xs_ref, sem, per_token)
        zbuf[...] = jnp.zeros_like(zbuf)
        rows = RUN * per_token
        n_zero = jnp.int32(0)
        for e in range(n_exp):
            first = gs_ref[e] + count_ref[e]
            n_piece = (padded_ref[e] - count_ref[e]) // RUN

            def body(c, carry, first=first):
                pltpu.make_async_copy(zbuf, xs_ref.at[pl.ds(pl.multiple_of((first + c * RUN) * per_token, per_token),
                                                            rows)], sem).start()
                return carry

            lax.fori_loop(0, n_piece, body, 0)
            n_zero = n_zero + n_piece
        used = gs_ref[n_exp - 1] + padded_ref[n_exp - 1]
        n_tail = (xs_ref.shape[0] // per_token - used) // RUN

        def tail_body(c, carry):
            pltpu.make_async_copy(zbuf, xs_ref.at[pl.ds(pl.multiple_of((used + c * RUN) * per_token, per_token),
                                                        rows)], sem).start()
            return carry

        lax.fori_loop(0, n_tail, tail_body, 0)
        _wait_runs(n_zero + n_tail, loc.at[0], xs_ref, sem, per_token)
        for e in range(n_exp):
            end = gs_ref[e] + padded_ref[e]
            pltpu.make_async_copy(zbuf, xs_ref.at[pl.ds(pl.multiple_of((end - RUN) * per_token, per_token), rows)],
                                  sem).start()
        _wait_runs(n_exp, loc.at[0], xs_ref, sem, per_token)


def _dispatch(rows, n_rows, h2, pos_t, cnt, carry, lstart, group_start, count, padded, n_slots):
    d = h2.shape[1]
    tm = rows.tm
    n_exp = count.shape[0]
    per_token = d // 2 // LANES
    n_local = _local_rows(tm, n_exp)
    smem_blk = pl.BlockSpec((1, 1, n_exp), lambda i, *_: (i, 0, 0), memory_space=pltpu.SMEM)
    grid_spec = pltpu.PrefetchScalarGridSpec(
        num_scalar_prefetch=3,
        grid=(n_rows // tm,),
        in_specs=[pl.BlockSpec((TOP_K, tm), lambda i, *_: (0, i)), smem_blk, smem_blk, smem_blk,
                  pl.BlockSpec((tm, d), lambda i, *_: (i, 0))],
        out_specs=pl.BlockSpec(memory_space=pl.ANY),
        scratch_shapes=[pltpu.VMEM((2, n_local * per_token, LANES), jnp.uint32),
                        pltpu.VMEM((RUN * per_token, LANES), jnp.uint32),
                        pltpu.SemaphoreType.DMA(()), pltpu.SMEM((1,), jnp.int32)],
    )
    return pl.pallas_call(
        functools.partial(_dispatch_kernel, per_token=per_token, n_exp=n_exp),
        grid_spec=grid_spec,
        out_shape=jax.ShapeDtypeStruct((n_slots * per_token, LANES), jnp.uint32),
        compiler_params=_cparams(("arbitrary",)),
        name="moe_dispatch",
    )(group_start, count, padded, pos_t, cnt, carry, lstart, h2)


def _expert_block_kernel(be_ref, nu_ref, x_ref, w1_ref, w3_ref, w2_ref, o_ref, xs_ref, w1b, w3b, w2b, prev_ref):
    blk = pl.program_id(0)
    d = xs_ref.shape[1]

    @pl.when(blk == 0)
    def _():
        prev_ref[0] = -1

    @pl.when(blk < nu_ref[0])
    def _():
        e = be_ref[blk]

        @pl.when(e != prev_ref[0])
        def _():
            w1b[...] = w1_ref[0, 0].astype(BF16)
            w3b[...] = w3_ref[0, 0].astype(BF16)
            w2b[...] = w2_ref[0, 0].astype(BF16)
            prev_ref[0] = e

        for s, (lo, hi) in enumerate(_load_packed(x_ref, (), MOE_ROWS, d)):
            xs_ref[:, s * LANES:(s + 1) * LANES] = lo.astype(BF16)
            xs_ref[:, d // 2 + s * LANES:d // 2 + (s + 1) * LANES] = hi.astype(BF16)
        hidden = _silu(_dot(xs_ref[...], w1b[...])) * _dot(xs_ref[...], w3b[...])
        _store_packed(o_ref, (), _dot(hidden.astype(BF16), w2b[...]))

    @pl.when(blk >= nu_ref[0])
    def _():
        o_ref[...] = jnp.zeros_like(o_ref)


def _expert_blocks(xs, block_expert, n_used, w1, w3, w2, layer):
    d, ff = w1.shape[-2:]
    per_token = d // 2 // LANES
    n_blocks = block_expert.shape[0]
    blk_spec = pl.BlockSpec((MOE_ROWS * per_token, LANES), lambda i, be, nu: (i, 0))
    in_spec = pl.BlockSpec((MOE_ROWS * per_token, LANES), lambda i, be, nu: (jnp.minimum(i, nu[0] - 1), 0))

    def wmap(i, be, nu):
        return (layer, be[i], 0, 0)

    grid_spec = pltpu.PrefetchScalarGridSpec(
        num_scalar_prefetch=2,
        grid=(n_blocks,),
        in_specs=[in_spec, pl.BlockSpec((1, 1, d, ff), wmap), pl.BlockSpec((1, 1, d, ff), wmap),
                  pl.BlockSpec((1, 1, ff, d), wmap)],
        out_specs=blk_spec,
        scratch_shapes=[pltpu.VMEM((MOE_ROWS, d), BF16),
                        pltpu.VMEM((d, ff), BF16), pltpu.VMEM((d, ff), BF16), pltpu.VMEM((ff, d), BF16),
                        pltpu.SMEM((1,), jnp.int32)],
    )
    return pl.pallas_call(
        _expert_block_kernel,
        grid_spec=grid_spec,
        out_shape=jax.ShapeDtypeStruct(xs.shape, jnp.uint32),
        compiler_params=_cparams(("arbitrary",)),
        name="experts",
    )(block_expert, n_used, xs, w1, w3, w2)


def _gather_combine_kernel(*refs, alpha, emit_h, per_token, n_exp):
    gs_ref, pos_ref, gate_ref, cnt_ref, carry_ref, ls_ref, ys_ref, h_ref, ws1_ref, ws3_ref, ws2_ref = refs[:11]
    x_ref, gt_ref, g_ref, b_ref = refs[11:15]
    n_out = 2 if emit_h else 1
    out_refs = refs[-(n_out + 2):-2]
    loc, sem = refs[-2:]
    tm, d = h_ref.shape
    n_local = loc.shape[0] // per_token

    @pl.when(pl.program_id(0) == 0)
    def _():
        loc[...] = jnp.zeros_like(loc)

    n_dma = _run_dmas(n_exp, cnt_ref, lambda e: gs_ref[e] + carry_ref[0, 0, e], lambda e: ls_ref[0, 0, e],
                      ys_ref, loc, sem, per_token)
    hidden = _silu(_dot(h_ref[...], ws1_ref[...])) * _dot(h_ref[...], ws3_ref[...])
    f = _dot(hidden.astype(BF16), ws2_ref[...])
    _wait_runs(n_dma, ys_ref, loc, sem, per_token)
    p_iota = lax.broadcasted_iota(jnp.int32, (tm, n_local), 1)
    weights = jnp.zeros((tm, n_local), F32)
    for k in range(TOP_K):
        weights = weights + jnp.where(p_iota == pos_ref[:, k:k + 1], gate_ref[:, k:k + 1], 0.0)
    weights = weights.astype(BF16)
    group = 2 if per_token % 2 == 0 else 1
    lo_cols, hi_cols = [], []
    for s0 in range(0, per_token, group):
        los, his = [], []
        for s in range(s0, s0 + group):
            p = loc[_token_rows(loc, n_local, s, per_token), :]
            los.append(lax.bitcast_convert_type(p << 16, F32).astype(BF16))
            his.append(lax.bitcast_convert_type(p & jnp.uint32(0xFFFF0000), F32).astype(BF16))
        lo_cols.append(_dot(weights, jnp.concatenate(los, axis=1)))
        hi_cols.append(_dot(weights, jnp.concatenate(his, axis=1)))
    f = f + jnp.concatenate(lo_cols + hi_cols, axis=1)
    x2 = _layernorm(alpha * x_ref[...] + gt_ref[0, 0] * f) * g_ref[...] + b_ref[...]
    out_refs[0][...] = x2
    if emit_h:
        sc_ref, sh_ref = refs[15:17]
        out_refs[1][...] = (x2 * (1.0 + sc_ref[0, 0]) + sh_ref[0, 0]).astype(BF16)


def _gather_combine(rows, n_rows, ys, pos, gate, cnt, carry, lstart, group_start, h2, ws1, ws3, ws2, x1, mod_l,
                    ln_g, ln_b, alpha, mod_next):
    d = x1.shape[1]
    sf = ws1.shape[1]
    tm = rows.tm
    n_exp = group_start.shape[0]
    per_token = d // 2 // LANES
    n_local = _local_rows(tm, n_exp)

    def rmap(i, *_):
        return (i, 0)

    def const(i, *_):
        return (0, 0)

    def mod_spec(which):
        return pl.BlockSpec((1, 1, 1, d), lambda i, *_: (rows.group(i), which, 0, 0))

    rspec = pl.BlockSpec((tm, d), rmap)
    kspec = pl.BlockSpec((tm, TOP_K), rmap)
    smem_blk = pl.BlockSpec((1, 1, n_exp), lambda i, *_: (i, 0, 0), memory_space=pltpu.SMEM)
    in_specs = [kspec, kspec, smem_blk, smem_blk, smem_blk, pl.BlockSpec(memory_space=pl.ANY), rspec,
                pl.BlockSpec((d, sf), const), pl.BlockSpec((d, sf), const), pl.BlockSpec((sf, d), const),
                rspec, mod_spec(5), pl.BlockSpec((1, d), const), pl.BlockSpec((1, d), const)]
    args = [pos, gate, cnt, carry, lstart, ys, h2, ws1, ws3, ws2, x1, mod_l, ln_g, ln_b]
    emit_h = mod_next is not None
    if emit_h:
        in_specs += [mod_spec(1), mod_spec(0)]
        args += [mod_next, mod_next]
        out_specs = [rspec, rspec]
        out_shape = [jax.ShapeDtypeStruct((n_rows, d), F32), jax.ShapeDtypeStruct((n_rows, d), BF16)]
    else:
        out_specs = rspec
        out_shape = jax.ShapeDtypeStruct((n_rows, d), F32)
    grid_spec = pltpu.PrefetchScalarGridSpec(
        num_scalar_prefetch=1,
        grid=(n_rows // tm,),
        in_specs=in_specs,
        out_specs=out_specs,
        scratch_shapes=[pltpu.VMEM((n_local * per_token, LANES), jnp.uint32), pltpu.SemaphoreType.DMA(())],
    )
    return pl.pallas_call(
        functools.partial(_gather_combine_kernel, alpha=alpha, emit_h=emit_h, per_token=per_token, n_exp=n_exp),
        grid_spec=grid_spec,
        out_shape=out_shape,
        compiler_params=_cparams(("arbitrary",)),
        name="moe_combine_norm",
    )(group_start, *args)


def _permute_w_in(w):
    a_end = A_Q + 2 * A_KV
    b_end = a_end + B_Q + 2 * B_KV
    c_end = b_end + C_CONV + C_V
    ab_end = c_end + 4 * C_V_HEADS
    main = jnp.concatenate([w[:, b_end:c_end], w[:, :b_end]], axis=1).astype(BF16)
    ab = jnp.pad(w[:, c_end:ab_end], ((0, 0), (0, AB_W - 4 * C_V_HEADS))).astype(BF16)
    gates = w[:, ab_end:].astype(BF16)
    return main, gates, ab


def kernel(x, c, ctx, c_ctx, w_mod, b_mod, w_in, q_norm_a, k_norm_a, sink_b, conv_c, a_log_c, dt_bias_c, norm_c,
           w_br_a, w_br_b, w_br_c, w_out, ln1_g, ln1_b, w_router, router_bias, w1, w3, w2, ws1, ws3, ws2,
           ln2_g, ln2_b):
    b, s, d = x.shape
    n_ctx = ctx.shape[1]
    depth = w_mod.shape[0]
    n_exp = w_router.shape[-1]
    alpha = (2.0 * depth) ** 0.25
    rows = _Rows(b, s, n_ctx)
    tm = rows.tm

    c_all = jnp.concatenate([c, c_ctx[None, :], jnp.zeros((8 - b - 1, d), F32)], axis=0)
    mod = _modulation(c_all, w_mod, b_mod).reshape(depth, 8, N_MOD, 1, d)
    tables = _rope_tables(s, tm)

    x_all = jnp.concatenate([x.reshape(b * s, d), ctx.reshape(b * n_ctx, d)], axis=0)
    x_cur, h = _entry(rows, x_all, mod[0])

    for l in range(depth):
        need_ctx = l < depth - 1
        n_rows = rows.r if need_ctx else rows.n_lat
        mod_l = mod[l]
        w_main, w_gates, w_ab = _permute_w_in(w_in[l])
        proj_main = _matmul(h, w_main, F32)
        gates = _matmul(h, w_gates, F32)
        proj_ab = _matmul(h, w_ab, F32)

        qkv = _attn_prep(rows, proj_main, tables, q_norm_a[l].reshape(1, -1), k_norm_a[l].reshape(1, -1))
        a_k, a_v = A_Q, A_Q + A_KV
        o_a = _attn_full(qkv, batch=b, n_q=s, q_base=0, sources=[(rows.n_lat, n_ctx), (0, s)],
                         q_col=0, k_col=a_k, v_col=a_v, heads=A_HEADS, kv_heads=A_KV_HEADS)
        o_b = _attn_window(qkv, sink_b[l], rows)
        if need_ctx:
            b_q = A_Q + 2 * A_KV
            o_a_c = _attn_full(qkv, batch=b, n_q=n_ctx, q_base=rows.n_lat, sources=[(rows.n_lat, n_ctx)],
                               q_col=0, k_col=a_k, v_col=a_v, heads=A_HEADS, kv_heads=A_KV_HEADS)
            o_b_c = _attn_full(qkv, batch=b, n_q=n_ctx, q_base=rows.n_lat, sources=[(rows.n_lat, n_ctx)],
                               q_col=b_q, k_col=b_q + B_Q, v_col=b_q + B_Q + B_KV, heads=B_HEADS,
                               kv_heads=B_KV_HEADS, sink=sink_b[l])
            o_a = jnp.concatenate([o_a, o_a_c], axis=0)
            o_b = jnp.concatenate([o_b, o_b_c], axis=0)

        dqkv, gbeta = _delta_prep(rows, proj_main, proj_ab, conv_c[l], a_log_c[l], dt_bias_c[l])
        o_f, o_r = _delta_scan(rows, dqkv, gbeta)

        y = _merge(n_rows, tm, o_a, o_b, o_f, o_r, proj_main, gates, norm_c[l].reshape(1, -1),
                   w_br_a[l].astype(BF16), w_br_b[l].astype(BF16), w_br_c[l].astype(BF16))
        x1, h2 = _outproj(rows, n_rows, y, w_out[l].astype(BF16), x_cur, mod_l,
                          ln1_g[l].reshape(1, -1), ln1_b[l].reshape(1, -1), alpha)

        gate, pos, cnt, before, lstart, counts = _router(rows, n_rows, x1, mod_l, w_router[l], router_bias[l])
        counts = counts[0].astype(jnp.int32)
        padded = (counts + RUN + MOE_ROWS - 1) // MOE_ROWS * MOE_ROWS
        padded_end = jnp.cumsum(padded)
        group_start = padded_end - padded
        n_blocks = -(-(n_rows * TOP_K + n_exp * RUN) // MOE_ROWS) + n_exp
        block_start = jnp.arange(n_blocks, dtype=jnp.int32)[:, None] * MOE_ROWS
        block_expert = jnp.minimum(jnp.sum((padded_end[None, :] <= block_start).astype(jnp.int32), axis=-1), n_exp - 1)
        n_used = (padded_end[-1:] // MOE_ROWS).astype(jnp.int32)
        xs = _dispatch(rows, n_rows, h2, pos.T, cnt, before, lstart, group_start, counts, padded,
                       n_blocks * MOE_ROWS)
        ys = _expert_blocks(xs, block_expert, n_used, w1, w3, w2, l)

        mod_next = mod[l + 1] if need_ctx else None
        out = _gather_combine(rows, n_rows, ys, pos, gate, cnt, before, lstart, group_start, h2,
                              ws1[l].astype(BF16), ws3[l].astype(BF16), ws2[l].astype(BF16),
                              x1, mod_l, ln2_g[l].reshape(1, -1), ln2_b[l].reshape(1, -1), alpha, mod_next)
        if need_ctx:
            x_cur, h = out
        else:
            x_cur = out
    return x_cur[:rows.n_lat].reshape(b, s, d)
```

```python
import functools
import math

import jax
import jax.numpy as jnp
from jax import lax
from jax.experimental import pallas as pl
from jax.experimental.pallas import tpu as pltpu

F32 = jnp.float32
BF16 = jnp.bfloat16
HIGHEST = lax.Precision.HIGHEST

HEAD_DIM = 128
GRID_W = 64
ROPE_THETA = 10000.0
A_HEADS, A_KV_HEADS = 8, 2
B_HEADS, B_KV_HEADS = 8, 2
WINDOW = 128
Q_BLOCK = 128
C_QK_HEADS, C_V_HEADS = 4, 8
CONV_K = 5
CHUNK = 64
TOP_K = 8
ROUTED_SCALE = 2.5
LN_EPS = 1e-5
RMS_EPS = 1e-6
LOG2_E = 1.4426950408889634
MOE_ROWS = 256
RUN = 16
LANES = 128
N_MOD = 6
HALO = 8
VMEM_LIMIT = 56 * 2**20

A_Q, A_KV = A_HEADS * HEAD_DIM, A_KV_HEADS * HEAD_DIM
B_Q, B_KV = B_HEADS * HEAD_DIM, B_KV_HEADS * HEAD_DIM
C_QK, C_V = C_QK_HEADS * HEAD_DIM, C_V_HEADS * HEAD_DIM
C_CONV = 2 * C_QK + C_V
ATT_W = A_Q + 2 * A_KV + B_Q + 2 * B_KV
MAIN_W = C_CONV + C_V + ATT_W
AB_W = 128


def _cparams(sem):
    return pltpu.CompilerParams(dimension_semantics=sem, vmem_limit_bytes=VMEM_LIMIT)


def _pick(n, prefs):
    for p in prefs:
        if n % p == 0:
            return p
    return n


def _dot(a, b, **kw):
    return jnp.dot(a, b, preferred_element_type=F32, **kw)


def _dot_nt(a, b):
    return lax.dot_general(a, b, (((1,), (1,)), ((), ())), preferred_element_type=F32)


def _dot_tn(a, b):
    return lax.dot_general(a, b, (((0,), (0,)), ((), ())), preferred_element_type=F32)


def _silu(x):
    return x * jax.nn.sigmoid(x)


def _mod_kernel(c_ref, w_ref, b_ref, o_ref):
    o_ref[0] = _dot(_silu(c_ref[...]), w_ref[0], precision=HIGHEST) + b_ref[0]


def _modulation(c_all, w_mod, b_mod):
    depth, d, width = w_mod.shape
    tn = _pick(width, (1024, 512, 256, 128))
    return pl.pallas_call(
        _mod_kernel,
        grid=(depth, width // tn),
        in_specs=[pl.BlockSpec((8, d), lambda l, j: (0, 0)),
                  pl.BlockSpec((1, d, tn), lambda l, j: (l, 0, j)),
                  pl.BlockSpec((1, 1, tn), lambda l, j: (l, 0, j))],
        out_specs=pl.BlockSpec((1, 8, tn), lambda l, j: (l, 0, j)),
        out_shape=jax.ShapeDtypeStruct((depth, 8, width), F32),
        compiler_params=_cparams(("parallel", "parallel")),
        name="modulation",
    )(c_all, w_mod, b_mod.reshape(depth, 1, width))


class _Rows:
    def __init__(self, b, s, c):
        self.b, self.s, self.c = b, s, c
        self.tm = 256 if c % 256 == 0 else 128
        self.n_lat = b * s
        self.r = b * s + b * c
        self.lat_blocks = self.n_lat // self.tm
        self.blocks = self.r // self.tm

    def group(self, i):
        return jnp.where(i < self.lat_blocks, i // (self.s // self.tm), self.b)


def _mod_spec(rows, d, which):
    return pl.BlockSpec((1, 1, 1, d), lambda i: (rows.group(i), which, 0, 0))


def _vec_spec(d):
    return pl.BlockSpec((1, d), lambda i: (0, 0))


def _layernorm(x):
    mu = jnp.mean(x, axis=-1, keepdims=True)
    xc = x - mu
    var = jnp.mean(xc * xc, axis=-1, keepdims=True)
    return xc * lax.rsqrt(var + LN_EPS)


def _entry_kernel(x_ref, sc_ref, sh_ref, x0_ref, h_ref):
    xn = _layernorm(x_ref[...])
    x0_ref[...] = xn
    h_ref[...] = (xn * (1.0 + sc_ref[0, 0]) + sh_ref[0, 0]).astype(BF16)


def _entry(rows, x_all, mod_l):
    r, d = x_all.shape
    tm = rows.tm
    row_spec = pl.BlockSpec((tm, d), lambda i: (i, 0))
    return pl.pallas_call(
        _entry_kernel,
        grid=(r // tm,),
        in_specs=[row_spec, _mod_spec(rows, d, 1), _mod_spec(rows, d, 0)],
        out_specs=[row_spec, row_spec],
        out_shape=[jax.ShapeDtypeStruct((r, d), F32), jax.ShapeDtypeStruct((r, d), BF16)],
        compiler_params=_cparams(("parallel",)),
        name="entry_norm",
    )(x_all, mod_l, mod_l)


def _mm_kernel(a_ref, w_ref, o_ref):
    o_ref[...] = _dot(a_ref[...], w_ref[...]).astype(o_ref.dtype)


def _matmul(a, w, out_dtype):
    m, k = a.shape
    n = w.shape[1]
    tm = _pick(m, (1024, 512, 256, 128))
    tn = _pick(n, (1024, 512, 256, 128))
    return pl.pallas_call(
        _mm_kernel,
        grid=(n // tn, m // tm),
        in_specs=[pl.BlockSpec((tm, k), lambda j, i: (i, 0)),
                  pl.BlockSpec((k, tn), lambda j, i: (0, j))],
        out_specs=pl.BlockSpec((tm, tn), lambda j, i: (i, j)),
        out_shape=jax.ShapeDtypeStruct((m, n), out_dtype),
        compiler_params=_cparams(("parallel", "parallel")),
        name="in_proj",
    )(a, w)


def _rope(t, cos, sin_lo, sin_hi):
    return t * cos + pltpu.roll(t, HEAD_DIM - 32, 1) * sin_lo + pltpu.roll(t, 32, 1) * sin_hi


def _rms(t, g):
    return t * lax.rsqrt(jnp.mean(t * t, axis=-1, keepdims=True) + RMS_EPS) * g


def _attn_prep_kernel(x_ref, cos_ref, slo_ref, shi_ref, qn_ref, kn_ref, o_ref):
    cos, slo, shi = cos_ref[...], slo_ref[...], shi_ref[...]
    scale = HEAD_DIM ** -0.5 * LOG2_E

    def head(j):
        return x_ref[:, j * HEAD_DIM:(j + 1) * HEAD_DIM]

    def put(j, v):
        o_ref[:, j * HEAD_DIM:(j + 1) * HEAD_DIM] = v.astype(BF16)

    j = 0
    for _ in range(A_HEADS):
        put(j, _rope(_rms(head(j), qn_ref[...]), cos, slo, shi) * scale)
        j += 1
    for _ in range(A_KV_HEADS):
        put(j, _rope(_rms(head(j), kn_ref[...]), cos, slo, shi))
        j += 1
    for _ in range(A_KV_HEADS):
        put(j, head(j))
        j += 1
    for _ in range(B_HEADS):
        put(j, _rope(head(j), cos, slo, shi) * scale)
        j += 1
    for _ in range(B_KV_HEADS):
        put(j, _rope(head(j), cos, slo, shi))
        j += 1
    for _ in range(B_KV_HEADS):
        put(j, head(j))
        j += 1


def _attn_prep(rows, proj_main, tables, q_norm, k_norm):
    r = proj_main.shape[0]
    tm = rows.tm
    pos_blocks = rows.s // tm
    col_blk = (C_CONV + C_V) // ATT_W
    assert col_blk * ATT_W == C_CONV + C_V

    def tab_map(i):
        return (jnp.where(i < rows.lat_blocks, i % pos_blocks, pos_blocks), 0)

    tab_spec = pl.BlockSpec((tm, HEAD_DIM), tab_map)
    return pl.pallas_call(
        _attn_prep_kernel,
        grid=(r // tm,),
        in_specs=[pl.BlockSpec((tm, ATT_W), lambda i: (i, col_blk)), tab_spec, tab_spec, tab_spec,
                  _vec_spec(HEAD_DIM), _vec_spec(HEAD_DIM)],
        out_specs=pl.BlockSpec((tm, ATT_W), lambda i: (i, 0)),
        out_shape=jax.ShapeDtypeStruct((r, ATT_W), BF16),
        compiler_params=_cparams(("parallel",)),
        name="attn_prep",
    )(proj_main, *tables, q_norm, k_norm)


def _rope_tables(s, tm):
    row = jnp.repeat(jnp.arange(s // GRID_W, dtype=F32), GRID_W)
    col = jnp.tile(jnp.arange(GRID_W, dtype=F32), s // GRID_W)
    axis_dim = HEAD_DIM // 2
    inv_freq = ROPE_THETA ** (-jnp.arange(0, axis_dim, 2, dtype=F32) / axis_dim)
    ang_r = row[:, None] * inv_freq[None, :]
    ang_c = col[:, None] * inv_freq[None, :]
    ang = jnp.concatenate([ang_r, ang_r, ang_c, ang_c], axis=-1)
    cos, sin = jnp.cos(ang), jnp.sin(ang)
    lo = (jnp.arange(HEAD_DIM) % axis_dim) < (axis_dim // 2)
    sin_lo = jnp.where(lo, -sin, 0.0)
    sin_hi = jnp.where(lo, 0.0, sin)
    one, zero = jnp.ones((tm, HEAD_DIM), F32), jnp.zeros((tm, HEAD_DIM), F32)
    return (jnp.concatenate([cos, one]), jnp.concatenate([sin_lo, zero]), jnp.concatenate([sin_hi, zero]))


def _attn_full_kernel(*refs, n_src, has_sink, groups):
    if has_sink:
        sink_ref, refs = refs[0], refs[1:]
    q_ref, o_ref = refs[0], refs[1 + 2 * n_src]
    kv = refs[1:1 + 2 * n_src]
    kvh = pl.program_id(1)
    for g in range(groups):
        q = q_ref[:, g * HEAD_DIM:(g + 1) * HEAD_DIM]
        scores = [_dot_nt(q, kv[2 * j][...]) for j in range(n_src)]
        m = functools.reduce(jnp.maximum, [jnp.max(s, axis=-1, keepdims=True) for s in scores])
        if has_sink:
            sink = sink_ref[kvh * groups + g] * LOG2_E
            m = jnp.maximum(m, sink)
        es = [jnp.exp2(s - m) for s in scores]
        den = functools.reduce(jnp.add, [jnp.sum(e, axis=-1, keepdims=True) for e in es])
        if has_sink:
            den = den + jnp.exp2(sink - m)
        acc = functools.reduce(jnp.add, [_dot(es[j].astype(BF16), kv[2 * j + 1][...]) for j in range(n_src)])
        o_ref[:, g * HEAD_DIM:(g + 1) * HEAD_DIM] = (acc / den).astype(o_ref.dtype)


def _attn_full(qkv, *, batch, n_q, q_base, sources, q_col, k_col, v_col, heads, kv_heads, sink=None):
    groups = heads // kv_heads
    tq = _pick(n_q, (256, 128))
    qw = groups * HEAD_DIM
    nq = n_q // tq
    in_specs = []
    args = []
    if sink is not None:
        in_specs.append(pl.BlockSpec(memory_space=pltpu.SMEM))
        args.append(sink)
    in_specs.append(pl.BlockSpec((tq, qw), lambda b, h, i: (q_base // tq + b * nq + i, q_col // qw + h)))
    args.append(qkv)
    for base, n in sources:
        for col in (k_col, v_col):
            in_specs.append(pl.BlockSpec(
                (n, HEAD_DIM), lambda b, h, i, base=base, n=n, col=col: (base // n + b, col // HEAD_DIM + h)))
            args.append(qkv)
    kern = functools.partial(_attn_full_kernel, n_src=len(sources), has_sink=sink is not None, groups=groups)
    return pl.pallas_call(
        kern,
        grid=(batch, kv_heads, nq),
        in_specs=in_specs,
        out_specs=pl.BlockSpec((tq, qw), lambda b, h, i: (b * nq + i, h)),
        out_shape=jax.ShapeDtypeStruct((batch * n_q, heads * HEAD_DIM), BF16),
        compiler_params=_cparams(("parallel", "parallel", "arbitrary")),
        name="attn_full",
    )(*args)


def _attn_win_kernel(sink_ref, q_ref, k_ref, v_ref, kc_ref, vc_ref, o_ref, *, groups, seq):
    kvh = pl.program_id(1)
    n = pl.program_id(2)
    span = 3 * Q_BLOCK
    start = jnp.clip((n - 1) * Q_BLOCK, 0, seq - span)
    start = pl.multiple_of(start, Q_BLOCK)
    kb = k_ref[pl.ds(start, span), :]
    vb = v_ref[pl.ds(start, span), :]
    q_pos = n * Q_BLOCK + lax.broadcasted_iota(jnp.int32, (Q_BLOCK, span), 0)
    k_pos = start + lax.broadcasted_iota(jnp.int32, (Q_BLOCK, span), 1)
    valid = jnp.abs(k_pos - q_pos) <= WINDOW
    for g in range(groups):
        q = q_ref[:, g * HEAD_DIM:(g + 1) * HEAD_DIM]
        s_win = jnp.where(valid, _dot_nt(q, kb), -jnp.inf)
        s_ctx = _dot_nt(q, kc_ref[...])
        sink = sink_ref[kvh * groups + g] * LOG2_E
        m = jnp.maximum(jnp.maximum(jnp.max(s_win, axis=-1, keepdims=True),
                                    jnp.max(s_ctx, axis=-1, keepdims=True)), sink)
        e_win = jnp.exp2(s_win - m)
        e_ctx = jnp.exp2(s_ctx - m)
        den = (jnp.sum(e_win, axis=-1, keepdims=True) + jnp.sum(e_ctx, axis=-1, keepdims=True)
               + jnp.exp2(sink - m))
        acc = _dot(e_win.astype(BF16), vb) + _dot(e_ctx.astype(BF16), vc_ref[...])
        o_ref[:, g * HEAD_DIM:(g + 1) * HEAD_DIM] = (acc / den).astype(o_ref.dtype)


def _attn_window(qkv, sink, rows):
    b, s, c = rows.b, rows.s, rows.c
    groups = B_HEADS // B_KV_HEADS
    qw = groups * HEAD_DIM
    nq = s // Q_BLOCK
    q_col = A_Q + 2 * A_KV
    k_col = q_col + B_Q
    v_col = k_col + B_KV
    ctx_base = rows.n_lat // c

    def kv_spec(n, base, col):
        return pl.BlockSpec((n, HEAD_DIM), lambda bb, h, i: (base + bb, col // HEAD_DIM + h))

    return pl.pallas_call(
        functools.partial(_attn_win_kernel, groups=groups, seq=s),
        grid=(b, B_KV_HEADS, nq),
        in_specs=[pl.BlockSpec(memory_space=pltpu.SMEM),
                  pl.BlockSpec((Q_BLOCK, qw), lambda bb, h, i: (bb * nq + i, q_col // qw + h)),
                  kv_spec(s, 0, k_col), kv_spec(s, 0, v_col),
                  kv_spec(c, ctx_base, k_col), kv_spec(c, ctx_base, v_col)],
        out_specs=pl.BlockSpec((Q_BLOCK, qw), lambda bb, h, i: (bb * nq + i, h)),
        out_shape=jax.ShapeDtypeStruct((b * s, B_Q), BF16),
        compiler_params=_cparams(("parallel", "parallel", "arbitrary")),
        name="attn_window",
    )(sink, qkv, qkv, qkv, qkv, qkv)


def _delta_prep_kernel(prev_ref, cur_ref, next_ref, ab_ref, w_ref, alog_ref, dtb_ref, qkv_ref, gb_ref,
                       *, lat_blocks, seq_blocks, ctx_blocks):
    i = pl.program_id(0)
    tm = cur_ref.shape[0]
    is_lat = i < lat_blocks
    pos = jnp.where(is_lat, i % seq_blocks, (i - lat_blocks) % ctx_blocks)
    n_pos = jnp.where(is_lat, seq_blocks, ctx_blocks)
    first = pos == 0
    last = pos == n_pos - 1
    keep_prev = jnp.where(first, 0.0, 1.0)
    keep_next = jnp.where(last, 0.0, 1.0)
    ext = tm + 2 * HALO
    n_qk = 2 * C_QK_HEADS
    for j in range(C_CONV // HEAD_DIM):
        sl = slice(j * HEAD_DIM, (j + 1) * HEAD_DIM)
        xt = jnp.concatenate([prev_ref[:, sl] * keep_prev, cur_ref[:, sl], next_ref[:, sl] * keep_next], axis=0)
        acc = None
        for tap in range(CONV_K):
            shift = (CONV_K // 2 - tap) % ext
            xs = xt if shift == 0 else pltpu.roll(xt, shift, 0)
            term = xs[HALO:HALO + tm] * w_ref[tap:tap + 1, sl]
            acc = term if acc is None else acc + term
        y = _silu(acc)
        if j < n_qk:
            y = y * lax.rsqrt(jnp.sum(y * y, axis=-1, keepdims=True) + RMS_EPS)
            if j < C_QK_HEADS:
                y = y * (HEAD_DIM ** -0.5)
        qkv_ref[:, sl] = y
    ab = ab_ref[...]
    lane = lax.broadcasted_iota(jnp.int32, ab.shape, 1)
    n_g = 2 * C_V_HEADS
    z = ab + dtb_ref[...]
    softplus = jnp.maximum(z, 0.0) + jnp.log(1.0 + jnp.exp(-jnp.abs(z)))
    g = -jnp.exp(alog_ref[...]) * softplus
    gb_ref[...] = jnp.where(lane < n_g, g, jax.nn.sigmoid(ab))


def _delta_prep(rows, proj_main, proj_ab, conv_w, a_log, dt_bias):
    r = proj_main.shape[0]
    tm = rows.tm
    hb = tm // HALO
    last_halo = r // HALO - 1
    pad = AB_W - a_log.size
    alog = jnp.pad(a_log.reshape(1, -1), ((0, 0), (0, pad)))
    dtb = jnp.pad(dt_bias.reshape(1, -1), ((0, 0), (0, pad)))
    kern = functools.partial(_delta_prep_kernel, lat_blocks=rows.lat_blocks, seq_blocks=rows.s // tm,
                             ctx_blocks=rows.c // tm)
    return pl.pallas_call(
        kern,
        grid=(r // tm,),
        in_specs=[pl.BlockSpec((HALO, C_CONV), lambda i: (jnp.maximum(i * hb - 1, 0), 0)),
                  pl.BlockSpec((tm, C_CONV), lambda i: (i, 0)),
                  pl.BlockSpec((HALO, C_CONV), lambda i: (jnp.minimum((i + 1) * hb, last_halo), 0)),
                  pl.BlockSpec((tm, AB_W), lambda i: (i, 0)),
                  pl.BlockSpec((CONV_K, C_CONV), lambda i: (0, 0)),
                  _vec_spec(AB_W), _vec_spec(AB_W)],
        out_specs=[pl.BlockSpec((tm, C_CONV), lambda i: (i, 0)), pl.BlockSpec((tm, AB_W), lambda i: (i, 0))],
        out_shape=[jax.ShapeDtypeStruct((r, C_CONV), F32), jax.ShapeDtypeStruct((r, AB_W), F32)],
        compiler_params=_cparams(("parallel",)),
        name="delta_prep",
    )(proj_main, proj_main, proj_main, proj_ab, conv_w, alog, dtb)


def _delta_kernel(qkv_f_ref, gb_f_ref, qkv_r_ref, gb_r_ref, o_f_ref, o_r_ref, state_ref):
    @pl.when(pl.program_id(1) == 0)
    def _():
        state_ref[...] = jnp.zeros_like(state_ref)

    row = lax.broadcasted_iota(jnp.int32, (CHUNK, CHUNK), 0)
    col = lax.broadcasted_iota(jnp.int32, (CHUNK, CHUNK), 1)
    eye = jnp.where(row == col, 1.0, 0.0)
    rep = C_V_HEADS // C_QK_HEADS
    n_sq = int(math.log2(CHUNK)) - 1
    chains = []
    for direction, (qkv_ref, gb_ref, o_ref) in enumerate(((qkv_f_ref, gb_f_ref, o_f_ref),
                                                          (qkv_r_ref, gb_r_ref, o_r_ref))):
        if direction == 0:
            incl, strict, last = row >= col, row > col, CHUNK - 1
        else:
            incl, strict, last = row <= col, row < col, 0
        gb = gb_ref[...]
        gcum = _dot(jnp.where(incl, 1.0, 0.0), gb, precision=HIGHEST)
        gcum_t = gcum.T
        kq = []
        for qh in range(C_QK_HEADS):
            q = qkv_ref[:, qh * HEAD_DIM:(qh + 1) * HEAD_DIM]
            k = qkv_ref[:, C_QK + qh * HEAD_DIM:C_QK + (qh + 1) * HEAD_DIM]
            prod = _dot_nt(jnp.concatenate([k, q], axis=0).astype(BF16), k.astype(BF16))
            kq.append((q, k, prod[:CHUNK], prod[CHUNK:]))
        for h in range(C_V_HEADS):
            lane = direction * C_V_HEADS + h
            q, k, kk, qk = kq[h // rep]
            g_col = gcum[:, lane:lane + 1]
            beta = gb[:, 2 * C_V_HEADS + lane:2 * C_V_HEADS + lane + 1]
            decay = jnp.where(incl, jnp.exp(jnp.minimum(g_col - gcum_t[lane:lane + 1, :], 0.0)), 0.0)
            v = qkv_ref[:, 2 * C_QK + h * HEAD_DIM:2 * C_QK + (h + 1) * HEAD_DIM]
            p = jnp.where(strict, -(kk * beta) * decay, 0.0)
            chains.append(dict(
                h=h, slot=lane, o_ref=o_ref, q=q, k=k, g_col=g_col, e_g=jnp.exp(g_col),
                g_last=gcum[last:last + 1, lane:lane + 1], beta=beta, v=v, qk=qk * decay, p=p, t=eye + p))
    for _ in range(n_sq):
        for ch in chains:
            pb = ch["p"].astype(BF16)
            ch["p"] = _dot(pb, pb)
        for ch in chains:
            ch["t"] = ch["t"] + _dot(ch["t"].astype(BF16), ch["p"].astype(BF16))
    for ch in chains:
        rhs = jnp.concatenate([ch["v"] * ch["beta"], ch["k"] * (ch["beta"] * ch["e_g"])], axis=1)
        uw = _dot(ch["t"].astype(BF16), rhs.astype(BF16))
        ch["u"], ch["w"] = uw[:, :HEAD_DIM], uw[:, HEAD_DIM:]
    for ch in chains:
        ch["state"] = state_ref[ch["slot"]]
        lhs = jnp.concatenate([ch["w"], ch["q"] * ch["e_g"]], axis=0)
        ch["ws_qs"] = _dot(lhs.astype(BF16), ch["state"].astype(BF16))
    for ch in chains:
        ch["v_new"] = (ch["u"] - ch["ws_qs"][:CHUNK]).astype(BF16)
    for ch in chains:
        h = ch["h"]
        ch["o_ref"][:, h * HEAD_DIM:(h + 1) * HEAD_DIM] = ch["ws_qs"][CHUNK:] + _dot(ch["qk"].astype(BF16), ch["v_new"])
    for ch in chains:
        k_dec = ch["k"] * jnp.exp(ch["g_last"] - ch["g_col"])
        state_ref[ch["slot"]] = ch["state"] * jnp.exp(ch["g_last"]) + _dot_tn(k_dec.astype(BF16), ch["v_new"])


def _delta_scan(rows, dqkv, gb):
    b, s, c = rows.b, rows.s, rows.c
    ctx_chunks, lat_chunks = c // CHUNK, s // CHUNK
    ctx_base = rows.n_lat // CHUNK

    def fwd_map(bb, t):
        return (jnp.where(t < ctx_chunks, ctx_base + bb * ctx_chunks + t, bb * lat_chunks + t - ctx_chunks), 0)

    def rev_map(bb, t):
        ctx_i, lat_i = ctx_chunks - 1 - t, lat_chunks - 1 - (t - ctx_chunks)
        return (jnp.where(t < ctx_chunks, ctx_base + bb * ctx_chunks + ctx_i, bb * lat_chunks + lat_i), 0)

    out = jax.ShapeDtypeStruct((rows.r, C_V), F32)
    return pl.pallas_call(
        _delta_kernel,
        grid=(b, ctx_chunks + lat_chunks),
        in_specs=[pl.BlockSpec((CHUNK, C_CONV), fwd_map), pl.BlockSpec((CHUNK, AB_W), fwd_map),
                  pl.BlockSpec((CHUNK, C_CONV), rev_map), pl.BlockSpec((CHUNK, AB_W), rev_map)],
        out_specs=[pl.BlockSpec((CHUNK, C_V), fwd_map), pl.BlockSpec((CHUNK, C_V), rev_map)],
        out_shape=[out, out],
        scratch_shapes=[pltpu.VMEM((2 * C_V_HEADS, HEAD_DIM, HEAD_DIM), F32)],
        compiler_params=_cparams(("parallel", "arbitrary")),
        name="delta_scan",
    )(dqkv, gb, dqkv, gb)


def _merge_kernel(oa_ref, ob_ref, of_ref, or_ref, z_ref, nc_ref, ga_ref, gb_ref, gc_ref,
                  wa_ref, wb_ref, wc_ref, y_ref, oc_ref):
    for h in range(C_V_HEADS):
        sl = slice(h * HEAD_DIM, (h + 1) * HEAD_DIM)
        o = of_ref[:, sl] + or_ref[:, sl]
        oc_ref[:, sl] = (_rms(o, nc_ref[...]) * _silu(z_ref[:, sl])).astype(BF16)
    y = jax.nn.sigmoid(ga_ref[...]) * _dot(oa_ref[...], wa_ref[...])
    y = y + jax.nn.sigmoid(gb_ref[...]) * _dot(ob_ref[...], wb_ref[...])
    y = y + jax.nn.sigmoid(gc_ref[...]) * _dot(oc_ref[...], wc_ref[...])
    y_ref[...] = y.astype(BF16)


def _merge(n_rows, tm, o_a, o_b, o_f, o_r, proj_main, gates, norm_c, w_a, w_b, w_c):
    d = w_a.shape[1]
    z_blk = C_CONV // C_V

    def rspec(w, col=0):
        return pl.BlockSpec((tm, w), lambda i: (i, col))

    def wspec(kdim):
        return pl.BlockSpec((kdim, d), lambda i: (0, 0))

    return pl.pallas_call(
        _merge_kernel,
        grid=(n_rows // tm,),
        in_specs=[rspec(A_Q), rspec(B_Q), rspec(C_V), rspec(C_V), rspec(C_V, z_blk), _vec_spec(HEAD_DIM),
                  rspec(d, 0), rspec(d, 1), rspec(d, 2), wspec(A_Q), wspec(B_Q), wspec(C_V)],
        out_specs=rspec(d),
        out_shape=jax.ShapeDtypeStruct((n_rows, d), BF16),
        scratch_shapes=[pltpu.VMEM((tm, C_V), BF16)],
        compiler_params=_cparams(("parallel",)),
        name="branch_merge",
    )(o_a, o_b, o_f, o_r, proj_main, norm_c, gates, gates, gates, w_a, w_b, w_c)


def _outproj_kernel(y_ref, w_ref, x_ref, gt_ref, g_ref, b_ref, sc_ref, sh_ref, x1_ref, h_ref, *, alpha):
    y = _dot(y_ref[...], w_ref[...])
    x1 = _layernorm(alpha * x_ref[...] + gt_ref[0, 0] * y) * g_ref[...] + b_ref[...]
    x1_ref[...] = x1
    h_ref[...] = (x1 * (1.0 + sc_ref[0, 0]) + sh_ref[0, 0]).astype(BF16)


def _outproj(rows, n_rows, y, w_out, x, mod_l, ln_g, ln_b, alpha):
    d = x.shape[1]
    tm = rows.tm
    rspec = pl.BlockSpec((tm, d), lambda i: (i, 0))
    return pl.pallas_call(
        functools.partial(_outproj_kernel, alpha=alpha),
        grid=(n_rows // tm,),
        in_specs=[rspec, pl.BlockSpec((d, d), lambda i: (0, 0)), rspec, _mod_spec(rows, d, 2),
                  _vec_spec(d), _vec_spec(d), _mod_spec(rows, d, 4), _mod_spec(rows, d, 3)],
        out_specs=[rspec, rspec],
        out_shape=[jax.ShapeDtypeStruct((n_rows, d), F32), jax.ShapeDtypeStruct((n_rows, d), BF16)],
        compiler_params=_cparams(("parallel",)),
        name="out_proj_norm",
    )(y, w_out, x, mod_l, ln_g, ln_b, mod_l, mod_l)


def _router_kernel(x_ref, sc_ref, sh_ref, wr_ref, rb_ref, gate_ref, pos_ref, cnt_ref, before_ref, lstart_ref,
                   total_ref, carry_ref):
    @pl.when(pl.program_id(0) == 0)
    def _():
        carry_ref[...] = jnp.zeros_like(carry_ref)

    h = x_ref[...] * (1.0 + sc_ref[0, 0]) + sh_ref[0, 0]
    scores = jax.nn.sigmoid(_dot(h, wr_ref[...], precision=HIGHEST))
    tm, n_exp = scores.shape
    sel = scores + rb_ref[...]
    lane = lax.broadcasted_iota(jnp.int32, (tm, n_exp), 1)
    slot = lax.broadcasted_iota(jnp.int32, (tm, TOP_K), 1)
    gate = jnp.zeros((tm, TOP_K), F32)
    mask = jnp.zeros((tm, n_exp), F32)
    hits = []
    for k in range(TOP_K):
        best = jnp.max(sel, axis=-1, keepdims=True)
        choice = jnp.min(jnp.where(sel == best, lane, n_exp), axis=-1, keepdims=True)
        hit = lane == choice
        hits.append(hit)
        gate = jnp.where(slot == k, jnp.sum(jnp.where(hit, scores, 0.0), axis=-1, keepdims=True), gate)
        sel = jnp.where(hit, -jnp.inf, sel)
        mask = jnp.where(hit, 1.0, mask)
    gate_ref[...] = gate / jnp.sum(gate, axis=-1, keepdims=True) * ROUTED_SCALE
    r_i = lax.broadcasted_iota(jnp.int32, (tm, tm), 0)
    c_i = lax.broadcasted_iota(jnp.int32, (tm, tm), 1)
    earlier = jnp.where(c_i < r_i, 1.0, 0.0).astype(BF16)
    within = _dot(earlier, mask.astype(BF16))
    cnt = jnp.sum(mask, axis=0, keepdims=True)
    run_rows = jnp.floor((cnt + (RUN - 1)) * (1.0 / RUN)) * RUN
    e_r = lax.broadcasted_iota(jnp.int32, (n_exp, n_exp), 0)
    e_c = lax.broadcasted_iota(jnp.int32, (n_exp, n_exp), 1)
    lstart = _dot(jnp.broadcast_to(run_rows, (8, n_exp)), jnp.where(e_r < e_c, 1.0, 0.0), precision=HIGHEST)[0:1]
    local = within + lstart
    pos = jnp.zeros((tm, TOP_K), F32)
    for k in range(TOP_K):
        pos = jnp.where(slot == k, jnp.sum(jnp.where(hits[k], local, 0.0), axis=-1, keepdims=True), pos)
    pos_ref[...] = pos.astype(jnp.int32)
    cnt_ref[0] = cnt.astype(jnp.int32)
    before_ref[0] = carry_ref[...].astype(jnp.int32)
    lstart_ref[0] = lstart.astype(jnp.int32)
    carry_ref[...] = carry_ref[...] + cnt
    total_ref[...] = carry_ref[...]


def _router(rows, n_rows, x1, mod_l, w_router, router_bias):
    d, n_exp = w_router.shape
    tm = rows.tm
    n_blk = n_rows // tm
    kspec = pl.BlockSpec((tm, TOP_K), lambda i: (i, 0))
    espec = pl.BlockSpec((1, 1, n_exp), lambda i: (i, 0, 0))
    per_block = jax.ShapeDtypeStruct((n_blk, 1, n_exp), jnp.int32)
    return pl.pallas_call(
        _router_kernel,
        grid=(n_blk,),
        in_specs=[pl.BlockSpec((tm, d), lambda i: (i, 0)), _mod_spec(rows, d, 4), _mod_spec(rows, d, 3),
                  pl.BlockSpec((d, n_exp), lambda i: (0, 0)), _vec_spec(n_exp)],
        out_specs=[kspec, kspec, espec, espec, espec, _vec_spec(n_exp)],
        out_shape=[jax.ShapeDtypeStruct((n_rows, TOP_K), F32), jax.ShapeDtypeStruct((n_rows, TOP_K), jnp.int32),
                   per_block, per_block, per_block, jax.ShapeDtypeStruct((1, n_exp), F32)],
        scratch_shapes=[pltpu.VMEM((1, n_exp), F32)],
        compiler_params=_cparams(("arbitrary",)),
        name="router",
    )(x1, mod_l, mod_l, w_router, router_bias.reshape(1, n_exp))


def _local_rows(tm, n_exp):
    return tm * TOP_K + n_exp * RUN


def _token_rows(tokens, s, per_token):
    return pl.ds(s, tokens, stride=per_token) if per_token > 1 else pl.ds(s, tokens)


def _pack_words(lo, hi):
    return (lax.bitcast_convert_type(lo, jnp.uint32) >> 16) | (lax.bitcast_convert_type(hi, jnp.uint32)
                                                               & jnp.uint32(0xFFFF0000))


def _unpack_words(p):
    return (lax.bitcast_convert_type(p << 16, F32).astype(BF16),
            lax.bitcast_convert_type(p & jnp.uint32(0xFFFF0000), F32).astype(BF16))


def _run_dmas(n_exp, cnt_ref, src_start, dst_start, src_ref, dst_ref, sem, per_token):
    total = jnp.int32(0)
    rows = RUN * per_token
    for e in range(n_exp):
        n_run = (cnt_ref[0, 0, e] + (RUN - 1)) // RUN
        s0, d0 = src_start(e), dst_start(e)

        def body(c, carry, s0=s0, d0=d0):
            pltpu.make_async_copy(src_ref.at[pl.ds(pl.multiple_of((s0 + c * RUN) * per_token, per_token), rows)],
                                  dst_ref.at[pl.ds(pl.multiple_of((d0 + c * RUN) * per_token, per_token), rows)],
                                  sem).start()
            return carry

        lax.fori_loop(0, n_run, body, 0)
        total = total + n_run
    return total


def _wait_runs(n, src_ref, dst_ref, sem, per_token):
    rows = RUN * per_token

    def body(c, carry):
        pltpu.make_async_copy(src_ref.at[pl.ds(0, rows)], dst_ref.at[pl.ds(0, rows)], sem).wait()
        return carry

    lax.fori_loop(0, n, body, 0)


def _dispatch_kernel(gs_ref, count_ref, padded_ref, post_ref, cnt_ref, carry_ref, ls_ref, h_ref, xs_ref,
                     loc, zbuf, sem, pending_ref, *, per_token, n_exp):
    i = pl.program_id(0)
    tm, d = h_ref.shape
    half = d // 2
    n_local = loc.shape[1] // per_token
    chunk = 512
    buf = i % 2
    for r0 in range(0, n_local, chunk):
        p_iota = r0 + lax.broadcasted_iota(jnp.int32, (chunk, tm), 0)
        onehot = jnp.zeros((chunk, tm), F32)
        for k in range(TOP_K):
            onehot = jnp.where(p_iota == post_ref[k:k + 1, :], 1.0, onehot)
        onehot = onehot.astype(BF16)
        words = _pack_words(_dot(onehot, h_ref[:, :half]), _dot(onehot, h_ref[:, half:]))
        for s in range(per_token):
            loc[buf, _token_rows(chunk, r0 * per_token + s, per_token), :] = words[:, s * LANES:(s + 1) * LANES]

    @pl.when(i > 0)
    def _():
        _wait_runs(pending_ref[0], loc.at[0], xs_ref, sem, per_token)

    pending_ref[0] = _run_dmas(n_exp, cnt_ref, lambda e: ls_ref[0, 0, e], lambda e: gs_ref[e] + carry_ref[0, 0, e],
                               loc.at[buf], xs_ref, sem, per_token)

    @pl.when(i == pl.num_programs(0) - 1)
    def _():
        _wait_runs(pending_ref[0], loc.at[0], xs_ref, sem, per_token)
        zbuf[...] = jnp.zeros_like(zbuf)
        rows = RUN * per_token
        n_zero = jnp.int32(0)
        for e in range(n_exp):
            first = gs_ref[e] + count_ref[e]
            n_piece = (padded_ref[e] - count_ref[e]) // RUN

            def body(c, carry, first=first):
                pltpu.make_async_copy(zbuf, xs_ref.at[pl.ds(pl.multiple_of((first + c * RUN) * per_token, per_token),
                                                            rows)], sem).start()
                return carry

            lax.fori_loop(0, n_piece, body, 0)
            n_zero = n_zero + n_piece
        used = gs_ref[n_exp - 1] + padded_ref[n_exp - 1]
        n_tail = (xs_ref.shape[0] // per_token - used) // RUN

        def tail_body(c, carry):
            pltpu.make_async_copy(zbuf, xs_ref.at[pl.ds(pl.multiple_of((used + c * RUN) * per_token, per_token),
                                                        rows)], sem).start()
            return carry

        lax.fori_loop(0, n_tail, tail_body, 0)
        _wait_runs(n_zero + n_tail, loc.at[0], xs_ref, sem, per_token)
        for e in range(n_exp):
            end = gs_ref[e] + padded_ref[e]
            pltpu.make_async_copy(zbuf, xs_ref.at[pl.ds(pl.multiple_of((end - RUN) * per_token, per_token), rows)],
                                  sem).start()
        _wait_runs(n_exp, loc.at[0], xs_ref, sem, per_token)


def _dispatch(rows, n_rows, h2, pos_t, cnt, carry, lstart, group_start, count, padded, n_slots):
    d = h2.shape[1]
    tm = rows.tm
    n_exp = count.shape[0]
    per_token = d // 2 // LANES
    n_local = _local_rows(tm, n_exp)
    smem_blk = pl.BlockSpec((1, 1, n_exp), lambda i, *_: (i, 0, 0), memory_space=pltpu.SMEM)
    grid_spec = pltpu.PrefetchScalarGridSpec(
        num_scalar_prefetch=3,
        grid=(n_rows // tm,),
        in_specs=[pl.BlockSpec((TOP_K, tm), lambda i, *_: (0, i)), smem_blk, smem_blk, smem_blk,
                  pl.BlockSpec((tm, d), lambda i, *_: (i, 0))],
        out_specs=pl.BlockSpec(memory_space=pl.ANY),
        scratch_shapes=[pltpu.VMEM((2, n_local * per_token, LANES), jnp.uint32),
                        pltpu.VMEM((RUN * per_token, LANES), jnp.uint32),
                        pltpu.SemaphoreType.DMA(()), pltpu.SMEM((1,), jnp.int32)],
    )
    return pl.pallas_call(
        functools.partial(_dispatch_kernel, per_token=per_token, n_exp=n_exp),
        grid_spec=grid_spec,
        out_shape=jax.ShapeDtypeStruct((n_slots * per_token, LANES), jnp.uint32),
        compiler_params=_cparams(("arbitrary",)),
        name="moe_dispatch",
    )(group_start, count, padded, pos_t, cnt, carry, lstart, h2)


def _expert_block_kernel(be_ref, nu_ref, x_ref, w1_ref, w3_ref, w2_ref, o_ref, xs_ref, w1b, w3b, w2b, prev_ref):
    blk = pl.program_id(0)
    d = xs_ref.shape[1]

    @pl.when(blk == 0)
    def _():
        prev_ref[0] = -1

    @pl.when(blk < nu_ref[0])
    def _():
        e = be_ref[blk]

        @pl.when(e != prev_ref[0])
        def _():
            w1b[...] = w1_ref[0, 0].astype(BF16)
            w3b[...] = w3_ref[0, 0].astype(BF16)
            w2b[...] = w2_ref[0, 0].astype(BF16)
            prev_ref[0] = e

        half = d // 2
        per_token = half // LANES
        for s in range(per_token):
            lo, hi = _unpack_words(x_ref[_token_rows(MOE_ROWS, s, per_token), :])
            xs_ref[:, s * LANES:(s + 1) * LANES] = lo
            xs_ref[:, half + s * LANES:half + (s + 1) * LANES] = hi
        hidden = _silu(_dot(xs_ref[...], w1b[...])) * _dot(xs_ref[...], w3b[...])
        y = _dot(hidden.astype(BF16), w2b[...])
        for s in range(per_token):
            o_ref[_token_rows(MOE_ROWS, s, per_token), :] = _pack_words(
                y[:, s * LANES:(s + 1) * LANES].astype(BF16).astype(F32),
                y[:, half + s * LANES:half + (s + 1) * LANES].astype(BF16).astype(F32))

    @pl.when(blk >= nu_ref[0])
    def _():
        o_ref[...] = jnp.zeros_like(o_ref)


def _expert_blocks(xs, block_expert, n_used, w1, w3, w2, layer):
    d, ff = w1.shape[-2:]
    per_token = d // 2 // LANES
    n_blocks = block_expert.shape[0]
    blk_spec = pl.BlockSpec((MOE_ROWS * per_token, LANES), lambda i, be, nu: (i, 0))
    in_spec = pl.BlockSpec((MOE_ROWS * per_token, LANES), lambda i, be, nu: (jnp.minimum(i, nu[0] - 1), 0))

    def wmap(i, be, nu):
        return (layer, be[i], 0, 0)

    grid_spec = pltpu.PrefetchScalarGridSpec(
        num_scalar_prefetch=2,
        grid=(n_blocks,),
        in_specs=[in_spec, pl.BlockSpec((1, 1, d, ff), wmap), pl.BlockSpec((1, 1, d, ff), wmap),
                  pl.BlockSpec((1, 1, ff, d), wmap)],
        out_specs=blk_spec,
        scratch_shapes=[pltpu.VMEM((MOE_ROWS, d), BF16),
                        pltpu.VMEM((d, ff), BF16), pltpu.VMEM((d, ff), BF16), pltpu.VMEM((ff, d), BF16),
                        pltpu.SMEM((1,), jnp.int32)],
    )
    return pl.pallas_call(
        _expert_block_kernel,
        grid_spec=grid_spec,
        out_shape=jax.ShapeDtypeStruct(xs.shape, jnp.uint32),
        compiler_params=_cparams(("arbitrary",)),
        name="experts",
    )(block_expert, n_used, xs, w1, w3, w2)


def _gather_combine_kernel(*refs, alpha, emit_h, per_token, n_exp):
    gs_ref, pos_ref, gate_ref, cnt_ref, carry_ref, ls_ref, ys_ref, h_ref, ws1_ref, ws3_ref, ws2_ref = refs[:11]
    x_ref, gt_ref, g_ref, b_ref = refs[11:15]
    n_out = 2 if emit_h else 1
    out_refs = refs[-(n_out + 3):-3]
    loc, wts_ref, sem = refs[-3:]
    tm, d = h_ref.shape
    n_local = loc.shape[0] // per_token

    @pl.when(pl.program_id(0) == 0)
    def _():
        loc[...] = jnp.zeros_like(loc)

    n_dma = _run_dmas(n_exp, cnt_ref, lambda e: gs_ref[e] + carry_ref[0, 0, e], lambda e: ls_ref[0, 0, e],
                      ys_ref, loc, sem, per_token)
    hidden = _silu(_dot(h_ref[...], ws1_ref[...])) * _dot(h_ref[...], ws3_ref[...])
    f = _dot(hidden.astype(BF16), ws2_ref[...])
    p_iota = lax.broadcasted_iota(jnp.int32, (tm, n_local), 1)
    weights = jnp.zeros((tm, n_local), F32)
    for k in range(TOP_K):
        weights = jnp.where(p_iota == pos_ref[:, k:k + 1], gate_ref[:, k:k + 1], weights)
    wts_ref[...] = weights.astype(BF16)
    _wait_runs(n_dma, ys_ref, loc, sem, per_token)
    group = 2 if per_token % 2 == 0 else 1
    lo_cols, hi_cols = [], []
    for s0 in range(0, per_token, group):
        planes = [_unpack_words(loc[_token_rows(n_local, s, per_token), :]) for s in range(s0, s0 + group)]
        lo_cols.append(_dot(wts_ref[...], jnp.concatenate([lo for lo, _ in planes], axis=1)))
        hi_cols.append(_dot(wts_ref[...], jnp.concatenate([hi for _, hi in planes], axis=1)))
    f = f + jnp.concatenate(lo_cols + hi_cols, axis=1)
    x2 = _layernorm(alpha * x_ref[...] + gt_ref[0, 0] * f) * g_ref[...] + b_ref[...]
    out_refs[0][...] = x2
    if emit_h:
        sc_ref, sh_ref = refs[15:17]
        out_refs[1][...] = (x2 * (1.0 + sc_ref[0, 0]) + sh_ref[0, 0]).astype(BF16)


def _gather_combine(rows, n_rows, ys, pos, gate, cnt, carry, lstart, group_start, h2, ws1, ws3, ws2, x1, mod_l,
                    ln_g, ln_b, alpha, mod_next):
    d = x1.shape[1]
    sf = ws1.shape[1]
    tm = rows.tm
    n_exp = group_start.shape[0]
    per_token = d // 2 // LANES
    n_local = _local_rows(tm, n_exp)

    def rmap(i, *_):
        return (i, 0)

    def const(i, *_):
        return (0, 0)

    def mod_spec(which):
        return pl.BlockSpec((1, 1, 1, d), lambda i, *_: (rows.group(i), which, 0, 0))

    rspec = pl.BlockSpec((tm, d), rmap)
    kspec = pl.BlockSpec((tm, TOP_K), rmap)
    smem_blk = pl.BlockSpec((1, 1, n_exp), lambda i, *_: (i, 0, 0), memory_space=pltpu.SMEM)
    in_specs = [kspec, kspec, smem_blk, smem_blk, smem_blk, pl.BlockSpec(memory_space=pl.ANY), rspec,
                pl.BlockSpec((d, sf), const), pl.BlockSpec((d, sf), const), pl.BlockSpec((sf, d), const),
                rspec, mod_spec(5), pl.BlockSpec((1, d), const), pl.BlockSpec((1, d), const)]
    args = [pos, gate, cnt, carry, lstart, ys, h2, ws1, ws3, ws2, x1, mod_l, ln_g, ln_b]
    emit_h = mod_next is not None
    if emit_h:
        in_specs += [mod_spec(1), mod_spec(0)]
        args += [mod_next, mod_next]
        out_specs = [rspec, rspec]
        out_shape = [jax.ShapeDtypeStruct((n_rows, d), F32), jax.ShapeDtypeStruct((n_rows, d), BF16)]
    else:
        out_specs = rspec
        out_shape = jax.ShapeDtypeStruct((n_rows, d), F32)
    grid_spec = pltpu.PrefetchScalarGridSpec(
        num_scalar_prefetch=1,
        grid=(n_rows // tm,),
        in_specs=in_specs,
        out_specs=out_specs,
        scratch_shapes=[pltpu.VMEM((n_local * per_token, LANES), jnp.uint32), pltpu.VMEM((tm, n_local), BF16),
                        pltpu.SemaphoreType.DMA(())],
    )
    return pl.pallas_call(
        functools.partial(_gather_combine_kernel, alpha=alpha, emit_h=emit_h, per_token=per_token, n_exp=n_exp),
        grid_spec=grid_spec,
        out_shape=out_shape,
        compiler_params=_cparams(("arbitrary",)),
        name="moe_combine_norm",
    )(group_start, *args)


def _permute_w_in(w):
    a_end = A_Q + 2 * A_KV
    b_end = a_end + B_Q + 2 * B_KV
    c_end = b_end + C_CONV + C_V
    ab_end = c_end + 4 * C_V_HEADS
    main = jnp.concatenate([w[:, b_end:c_end], w[:, :b_end]], axis=1).astype(BF16)
    ab = jnp.pad(w[:, c_end:ab_end], ((0, 0), (0, AB_W - 4 * C_V_HEADS))).astype(BF16)
    gates = w[:, ab_end:].astype(BF16)
    return main, gates, ab


def kernel(x, c, ctx, c_ctx, w_mod, b_mod, w_in, q_norm_a, k_norm_a, sink_b, conv_c, a_log_c, dt_bias_c, norm_c,
           w_br_a, w_br_b, w_br_c, w_out, ln1_g, ln1_b, w_router, router_bias, w1, w3, w2, ws1, ws3, ws2,
           ln2_g, ln2_b):
    b, s, d = x.shape
    n_ctx = ctx.shape[1]
    depth = w_mod.shape[0]
    n_exp = w_router.shape[-1]
    alpha = (2.0 * depth) ** 0.25
    rows = _Rows(b, s, n_ctx)
    tm = rows.tm

    c_all = jnp.concatenate([c, c_ctx[None, :], jnp.zeros((8 - b - 1, d), F32)], axis=0)
    mod = _modulation(c_all, w_mod, b_mod).reshape(depth, 8, N_MOD, 1, d)
    tables = _rope_tables(s, tm)

    x_all = jnp.concatenate([x.reshape(b * s, d), ctx.reshape(b * n_ctx, d)], axis=0)
    x_cur, h = _entry(rows, x_all, mod[0])

    for l in range(depth):
        need_ctx = l < depth - 1
        n_rows = rows.r if need_ctx else rows.n_lat
        mod_l = mod[l]
        w_main, w_gates, w_ab = _permute_w_in(w_in[l])
        proj_main = _matmul(h, w_main, F32)
        gates = _matmul(h, w_gates, F32)
        proj_ab = _matmul(h, w_ab, F32)

        qkv = _attn_prep(rows, proj_main, tables, q_norm_a[l].reshape(1, -1), k_norm_a[l].reshape(1, -1))
        a_k, a_v = A_Q, A_Q + A_KV
        o_a = _attn_full(qkv, batch=b, n_q=s, q_base=0, sources=[(rows.n_lat, n_ctx), (0, s)],
                         q_col=0, k_col=a_k, v_col=a_v, heads=A_HEADS, kv_heads=A_KV_HEADS)
        o_b = _attn_window(qkv, sink_b[l], rows)
        if need_ctx:
            b_q = A_Q + 2 * A_KV
            o_a_c = _attn_full(qkv, batch=b, n_q=n_ctx, q_base=rows.n_lat, sources=[(rows.n_lat, n_ctx)],
                               q_col=0, k_col=a_k, v_col=a_v, heads=A_HEADS, kv_heads=A_KV_HEADS)
            o_b_c = _attn_full(qkv, batch=b, n_q=n_ctx, q_base=rows.n_lat, sources=[(rows.n_lat, n_ctx)],
                               q_col=b_q, k_col=b_q + B_Q, v_col=b_q + B_Q + B_KV, heads=B_HEADS,
                               kv_heads=B_KV_HEADS, sink=sink_b[l])
            o_a = jnp.concatenate([o_a, o_a_c], axis=0)
            o_b = jnp.concatenate([o_b, o_b_c], axis=0)

        dqkv, gbeta = _delta_prep(rows, proj_main, proj_ab, conv_c[l], a_log_c[l], dt_bias_c[l])
        o_f, o_r = _delta_scan(rows, dqkv, gbeta)

        y = _merge(n_rows, tm, o_a, o_b, o_f, o_r, proj_main, gates, norm_c[l].reshape(1, -1),
                   w_br_a[l].astype(BF16), w_br_b[l].astype(BF16), w_br_c[l].astype(BF16))
        x1, h2 = _outproj(rows, n_rows, y, w_out[l].astype(BF16), x_cur, mod_l,
                          ln1_g[l].reshape(1, -1), ln1_b[l].reshape(1, -1), alpha)

        gate, pos, cnt, before, lstart, counts = _router(rows, n_rows, x1, mod_l, w_router[l], router_bias[l])
        counts = counts[0].astype(jnp.int32)
        padded = (counts + RUN + MOE_ROWS - 1) // MOE_ROWS * MOE_ROWS
        padded_end = jnp.cumsum(padded)
        group_start = padded_end - padded
        n_blocks = -(-(n_rows * TOP_K + n_exp * RUN) // MOE_ROWS) + n_exp
        block_start = jnp.arange(n_blocks, dtype=jnp.int32)[:, None] * MOE_ROWS
        block_expert = jnp.minimum(jnp.sum((padded_end[None, :] <= block_start).astype(jnp.int32), axis=-1), n_exp - 1)
        n_used = (padded_end[-1:] // MOE_ROWS).astype(jnp.int32)
        xs = _dispatch(rows, n_rows, h2, pos.T, cnt, before, lstart, group_start, counts, padded,
                       n_blocks * MOE_ROWS)
        ys = _expert_blocks(xs, block_expert, n_used, w1, w3, w2, l)

        mod_next = mod[l + 1] if need_ctx else None
        out = _gather_combine(rows, n_rows, ys, pos, gate, cnt, before, lstart, group_start, h2,
                              ws1[l].astype(BF16), ws3[l].astype(BF16), ws2[l].astype(BF16),
                              x1, mod_l, ln2_g[l].reshape(1, -1), ln2_b[l].reshape(1, -1), alpha, mod_next)
        if need_ctx:
            x_cur, h = out
        else:
            x_cur = out
    return x_cur[:rows.n_lat].reshape(b, s, d)
```

```python
import functools
import math

import jax
import jax.numpy as jnp
from jax import lax
from jax.experimental import pallas as pl
from jax.experimental.pallas import tpu as pltpu

F32 = jnp.float32
BF16 = jnp.bfloat16
HIGHEST = lax.Precision.HIGHEST

HEAD_DIM = 128
GRID_W = 64
ROPE_THETA = 10000.0
A_HEADS, A_KV_HEADS = 8, 2
B_HEADS, B_KV_HEADS = 8, 2
WINDOW = 128
Q_BLOCK = 128
C_QK_HEADS, C_V_HEADS = 4, 8
CONV_K = 5
CHUNK = 64
TOP_K = 8
ROUTED_SCALE = 2.5
LN_EPS = 1e-5
RMS_EPS = 1e-6
LOG2_E = 1.4426950408889634
MOE_ROWS = 256
RUN = 16
LANES = 128
N_MOD = 6
HALO = 8
VMEM_LIMIT = 56 * 2**20

A_Q, A_KV = A_HEADS * HEAD_DIM, A_KV_HEADS * HEAD_DIM
B_Q, B_KV = B_HEADS * HEAD_DIM, B_KV_HEADS * HEAD_DIM
C_QK, C_V = C_QK_HEADS * HEAD_DIM, C_V_HEADS * HEAD_DIM
C_CONV = 2 * C_QK + C_V
ATT_W = A_Q + 2 * A_KV + B_Q + 2 * B_KV
MAIN_W = C_CONV + C_V + ATT_W
AB_W = 128


def _cparams(sem):
    return pltpu.CompilerParams(dimension_semantics=sem, vmem_limit_bytes=VMEM_LIMIT)


def _pick(n, prefs):
    for p in prefs:
        if n % p == 0:
            return p
    return n


def _dot(a, b, **kw):
    return jnp.dot(a, b, preferred_element_type=F32, **kw)


def _dot_nt(a, b):
    return lax.dot_general(a, b, (((1,), (1,)), ((), ())), preferred_element_type=F32)


def _dot_tn(a, b):
    return lax.dot_general(a, b, (((0,), (0,)), ((), ())), preferred_element_type=F32)


def _silu(x):
    return x * jax.nn.sigmoid(x)


def _mod_kernel(c_ref, w_ref, b_ref, o_ref):
    o_ref[0] = _dot(_silu(c_ref[...]), w_ref[0], precision=HIGHEST) + b_ref[0]


def _modulation(c_all, w_mod, b_mod):
    depth, d, width = w_mod.shape
    tn = _pick(width, (1024, 512, 256, 128))
    return pl.pallas_call(
        _mod_kernel,
        grid=(depth, width // tn),
        in_specs=[pl.BlockSpec((8, d), lambda l, j: (0, 0)),
                  pl.BlockSpec((1, d, tn), lambda l, j: (l, 0, j)),
                  pl.BlockSpec((1, 1, tn), lambda l, j: (l, 0, j))],
        out_specs=pl.BlockSpec((1, 8, tn), lambda l, j: (l, 0, j)),
        out_shape=jax.ShapeDtypeStruct((depth, 8, width), F32),
        compiler_params=_cparams(("parallel", "parallel")),
        name="modulation",
    )(c_all, w_mod, b_mod.reshape(depth, 1, width))


class _Rows:
    def __init__(self, b, s, c):
        self.b, self.s, self.c = b, s, c
        self.tm = 256 if c % 256 == 0 else 128
        self.n_lat = b * s
        self.r = b * s + b * c
        self.lat_blocks = self.n_lat // self.tm
        self.blocks = self.r // self.tm

    def group(self, i):
        return jnp.where(i < self.lat_blocks, i // (self.s // self.tm), self.b)


def _mod_spec(rows, d, which):
    return pl.BlockSpec((1, 1, 1, d), lambda i: (rows.group(i), which, 0, 0))


def _vec_spec(d):
    return pl.BlockSpec((1, d), lambda i: (0, 0))


def _layernorm(x):
    mu = jnp.mean(x, axis=-1, keepdims=True)
    xc = x - mu
    var = jnp.mean(xc * xc, axis=-1, keepdims=True)
    return xc * lax.rsqrt(var + LN_EPS)


def _entry_kernel(lat_ref, ctx_ref, sc_ref, sh_ref, x0_ref, h_ref, *, lat_blocks):
    def emit(src_ref):
        xn = _layernorm(src_ref[...])
        x0_ref[...] = xn
        h_ref[...] = (xn * (1.0 + sc_ref[0, 0]) + sh_ref[0, 0]).astype(BF16)

    is_lat = pl.program_id(0) < lat_blocks
    pl.when(is_lat)(functools.partial(emit, lat_ref))
    pl.when(jnp.logical_not(is_lat))(functools.partial(emit, ctx_ref))


def _entry(rows, x_lat, x_ctx, mod_l):
    d = x_lat.shape[1]
    tm = rows.tm
    lat_blocks = rows.lat_blocks
    row_spec = pl.BlockSpec((tm, d), lambda i: (i, 0))
    return pl.pallas_call(
        functools.partial(_entry_kernel, lat_blocks=lat_blocks),
        grid=(rows.r // tm,),
        in_specs=[pl.BlockSpec((tm, d), lambda i: (jnp.minimum(i, lat_blocks - 1), 0)),
                  pl.BlockSpec((tm, d), lambda i: (jnp.maximum(i - lat_blocks, 0), 0)),
                  _mod_spec(rows, d, 1), _mod_spec(rows, d, 0)],
        out_specs=[row_spec, row_spec],
        out_shape=[jax.ShapeDtypeStruct((rows.r, d), F32), jax.ShapeDtypeStruct((rows.r, d), BF16)],
        compiler_params=_cparams(("parallel",)),
        name="entry_norm",
    )(x_lat, x_ctx, mod_l, mod_l)


def _mm_kernel(a_ref, w_ref, o_ref):
    o_ref[...] = _dot(a_ref[...], w_ref[...]).astype(o_ref.dtype)


def _matmul(a, w, out_dtype):
    m, k = a.shape
    n = w.shape[1]
    tm = _pick(m, (1024, 512, 256, 128))
    tn = _pick(n, (1024, 512, 256, 128))
    return pl.pallas_call(
        _mm_kernel,
        grid=(n // tn, m // tm),
        in_specs=[pl.BlockSpec((tm, k), lambda j, i: (i, 0)),
                  pl.BlockSpec((k, tn), lambda j, i: (0, j))],
        out_specs=pl.BlockSpec((tm, tn), lambda j, i: (i, j)),
        out_shape=jax.ShapeDtypeStruct((m, n), out_dtype),
        compiler_params=_cparams(("parallel", "parallel")),
        name="in_proj",
    )(a, w)


def _rope(t, cos, sin_lo, sin_hi):
    return t * cos + pltpu.roll(t, HEAD_DIM - 32, 1) * sin_lo + pltpu.roll(t, 32, 1) * sin_hi


def _rms(t, g):
    return t * lax.rsqrt(jnp.mean(t * t, axis=-1, keepdims=True) + RMS_EPS) * g


def _attn_prep_kernel(x_ref, cos_ref, slo_ref, shi_ref, qn_ref, kn_ref, o_ref):
    cos, slo, shi = cos_ref[...], slo_ref[...], shi_ref[...]
    scale = HEAD_DIM ** -0.5 * LOG2_E

    def head(j):
        return x_ref[:, j * HEAD_DIM:(j + 1) * HEAD_DIM]

    def put(j, v):
        o_ref[:, j * HEAD_DIM:(j + 1) * HEAD_DIM] = v.astype(BF16)

    j = 0
    for _ in range(A_HEADS):
        put(j, _rope(_rms(head(j), qn_ref[...]), cos, slo, shi) * scale)
        j += 1
    for _ in range(A_KV_HEADS):
        put(j, _rope(_rms(head(j), kn_ref[...]), cos, slo, shi))
        j += 1
    for _ in range(A_KV_HEADS):
        put(j, head(j))
        j += 1
    for _ in range(B_HEADS):
        put(j, _rope(head(j), cos, slo, shi) * scale)
        j += 1
    for _ in range(B_KV_HEADS):
        put(j, _rope(head(j), cos, slo, shi))
        j += 1
    for _ in range(B_KV_HEADS):
        put(j, head(j))
        j += 1


def _attn_prep(rows, proj_main, tables, q_norm, k_norm):
    r = proj_main.shape[0]
    tm = rows.tm
    pos_blocks = rows.s // tm
    col_blk = (C_CONV + C_V) // ATT_W
    assert col_blk * ATT_W == C_CONV + C_V

    def tab_map(i):
        return (jnp.where(i < rows.lat_blocks, i % pos_blocks, pos_blocks), 0)

    tab_spec = pl.BlockSpec((tm, HEAD_DIM), tab_map)
    return pl.pallas_call(
        _attn_prep_kernel,
        grid=(r // tm,),
        in_specs=[pl.BlockSpec((tm, ATT_W), lambda i: (i, col_blk)), tab_spec, tab_spec, tab_spec,
                  _vec_spec(HEAD_DIM), _vec_spec(HEAD_DIM)],
        out_specs=pl.BlockSpec((tm, ATT_W), lambda i: (i, 0)),
        out_shape=jax.ShapeDtypeStruct((r, ATT_W), BF16),
        compiler_params=_cparams(("parallel",)),
        name="attn_prep",
    )(proj_main, *tables, q_norm, k_norm)


def _rope_tables(s, tm):
    row = jnp.repeat(jnp.arange(s // GRID_W, dtype=F32), GRID_W)
    col = jnp.tile(jnp.arange(GRID_W, dtype=F32), s // GRID_W)
    axis_dim = HEAD_DIM // 2
    inv_freq = ROPE_THETA ** (-jnp.arange(0, axis_dim, 2, dtype=F32) / axis_dim)
    ang_r = row[:, None] * inv_freq[None, :]
    ang_c = col[:, None] * inv_freq[None, :]
    ang = jnp.concatenate([ang_r, ang_r, ang_c, ang_c], axis=-1)
    cos, sin = jnp.cos(ang), jnp.sin(ang)
    lo = (jnp.arange(HEAD_DIM) % axis_dim) < (axis_dim // 2)
    sin_lo = jnp.where(lo, -sin, 0.0)
    sin_hi = jnp.where(lo, 0.0, sin)
    one, zero = jnp.ones((tm, HEAD_DIM), F32), jnp.zeros((tm, HEAD_DIM), F32)
    return (jnp.concatenate([cos, one]), jnp.concatenate([sin_lo, zero]), jnp.concatenate([sin_hi, zero]))


def _attn_full_kernel(*refs, n_src, has_sink, groups):
    if has_sink:
        sink_ref, refs = refs[0], refs[1:]
    q_ref, o_ref = refs[0], refs[1 + 2 * n_src]
    kv = refs[1:1 + 2 * n_src]
    kvh = pl.program_id(1)
    for g in range(groups):
        q = q_ref[:, g * HEAD_DIM:(g + 1) * HEAD_DIM]
        scores = [_dot_nt(q, kv[2 * j][...]) for j in range(n_src)]
        m = functools.reduce(jnp.maximum, [jnp.max(s, axis=-1, keepdims=True) for s in scores])
        if has_sink:
            sink = sink_ref[kvh * groups + g] * LOG2_E
            m = jnp.maximum(m, sink)
        es = [jnp.exp2(s - m) for s in scores]
        den = functools.reduce(jnp.add, [jnp.sum(e, axis=-1, keepdims=True) for e in es])
        if has_sink:
            den = den + jnp.exp2(sink - m)
        acc = functools.reduce(jnp.add, [_dot(es[j].astype(BF16), kv[2 * j + 1][...]) for j in range(n_src)])
        o_ref[:, g * HEAD_DIM:(g + 1) * HEAD_DIM] = (acc / den).astype(o_ref.dtype)


def _attn_full(qkv, *, batch, n_q, q_base, sources, q_col, k_col, v_col, heads, kv_heads, sink=None):
    groups = heads // kv_heads
    tq = _pick(n_q, (256, 128))
    qw = groups * HEAD_DIM
    nq = n_q // tq
    in_specs = []
    args = []
    if sink is not None:
        in_specs.append(pl.BlockSpec(memory_space=pltpu.SMEM))
        args.append(sink)
    in_specs.append(pl.BlockSpec((tq, qw), lambda b, h, i: (q_base // tq + b * nq + i, q_col // qw + h)))
    args.append(qkv)
    for base, n in sources:
        for col in (k_col, v_col):
            in_specs.append(pl.BlockSpec(
                (n, HEAD_DIM), lambda b, h, i, base=base, n=n, col=col: (base // n + b, col // HEAD_DIM + h)))
            args.append(qkv)
    kern = functools.partial(_attn_full_kernel, n_src=len(sources), has_sink=sink is not None, groups=groups)
    return pl.pallas_call(
        kern,
        grid=(batch, kv_heads, nq),
        in_specs=in_specs,
        out_specs=pl.BlockSpec((tq, qw), lambda b, h, i: (b * nq + i, h)),
        out_shape=jax.ShapeDtypeStruct((batch * n_q, heads * HEAD_DIM), BF16),
        compiler_params=_cparams(("parallel", "parallel", "arbitrary")),
        name="attn_full",
    )(*args)


def _attn_win_kernel(sink_ref, q_ref, k_ref, v_ref, kc_ref, vc_ref, o_ref, *, groups, seq):
    kvh = pl.program_id(1)
    n = pl.program_id(2)
    span = 3 * Q_BLOCK
    start = jnp.clip((n - 1) * Q_BLOCK, 0, seq - span)
    start = pl.multiple_of(start, Q_BLOCK)
    kb = k_ref[pl.ds(start, span), :]
    vb = v_ref[pl.ds(start, span), :]
    q_pos = n * Q_BLOCK + lax.broadcasted_iota(jnp.int32, (Q_BLOCK, span), 0)
    k_pos = start + lax.broadcasted_iota(jnp.int32, (Q_BLOCK, span), 1)
    valid = jnp.abs(k_pos - q_pos) <= WINDOW
    for g in range(groups):
        q = q_ref[:, g * HEAD_DIM:(g + 1) * HEAD_DIM]
        s_win = jnp.where(valid, _dot_nt(q, kb), -jnp.inf)
        s_ctx = _dot_nt(q, kc_ref[...])
        sink = sink_ref[kvh * groups + g] * LOG2_E
        m = jnp.maximum(jnp.maximum(jnp.max(s_win, axis=-1, keepdims=True),
                                    jnp.max(s_ctx, axis=-1, keepdims=True)), sink)
        e_win = jnp.exp2(s_win - m)
        e_ctx = jnp.exp2(s_ctx - m)
        den = (jnp.sum(e_win, axis=-1, keepdims=True) + jnp.sum(e_ctx, axis=-1, keepdims=True)
               + jnp.exp2(sink - m))
        acc = _dot(e_win.astype(BF16), vb) + _dot(e_ctx.astype(BF16), vc_ref[...])
        o_ref[:, g * HEAD_DIM:(g + 1) * HEAD_DIM] = (acc / den).astype(o_ref.dtype)


def _attn_window(qkv, sink, rows):
    b, s, c = rows.b, rows.s, rows.c
    groups = B_HEADS // B_KV_HEADS
    qw = groups * HEAD_DIM
    nq = s // Q_BLOCK
    q_col = A_Q + 2 * A_KV
    k_col = q_col + B_Q
    v_col = k_col + B_KV
    ctx_base = rows.n_lat // c

    def kv_spec(n, base, col):
        return pl.BlockSpec((n, HEAD_DIM), lambda bb, h, i: (base + bb, col // HEAD_DIM + h))

    return pl.pallas_call(
        functools.partial(_attn_win_kernel, groups=groups, seq=s),
        grid=(b, B_KV_HEADS, nq),
        in_specs=[pl.BlockSpec(memory_space=pltpu.SMEM),
                  pl.BlockSpec((Q_BLOCK, qw), lambda bb, h, i: (bb * nq + i, q_col // qw + h)),
                  kv_spec(s, 0, k_col), kv_spec(s, 0, v_col),
                  kv_spec(c, ctx_base, k_col), kv_spec(c, ctx_base, v_col)],
        out_specs=pl.BlockSpec((Q_BLOCK, qw), lambda bb, h, i: (bb * nq + i, h)),
        out_shape=jax.ShapeDtypeStruct((b * s, B_Q), BF16),
        compiler_params=_cparams(("parallel", "parallel", "arbitrary")),
        name="attn_window",
    )(sink, qkv, qkv, qkv, qkv, qkv)


def _delta_prep_kernel(prev_ref, cur_ref, next_ref, ab_ref, w_ref, alog_ref, dtb_ref, qkv_ref, gb_ref,
                       *, lat_blocks, seq_blocks, ctx_blocks):
    i = pl.program_id(0)
    tm = cur_ref.shape[0]
    is_lat = i < lat_blocks
    pos = jnp.where(is_lat, i % seq_blocks, (i - lat_blocks) % ctx_blocks)
    n_pos = jnp.where(is_lat, seq_blocks, ctx_blocks)
    first = pos == 0
    last = pos == n_pos - 1
    keep_prev = jnp.where(first, 0.0, 1.0)
    keep_next = jnp.where(last, 0.0, 1.0)
    ext = tm + 2 * HALO
    n_qk = 2 * C_QK_HEADS
    for j in range(C_CONV // HEAD_DIM):
        sl = slice(j * HEAD_DIM, (j + 1) * HEAD_DIM)
        xt = jnp.concatenate([prev_ref[:, sl] * keep_prev, cur_ref[:, sl], next_ref[:, sl] * keep_next], axis=0)
        acc = None
        for tap in range(CONV_K):
            shift = (CONV_K // 2 - tap) % ext
            xs = xt if shift == 0 else pltpu.roll(xt, shift, 0)
            term = xs[HALO:HALO + tm] * w_ref[tap:tap + 1, sl]
            acc = term if acc is None else acc + term
        y = _silu(acc)
        if j < n_qk:
            y = y * lax.rsqrt(jnp.sum(y * y, axis=-1, keepdims=True) + RMS_EPS)
            if j < C_QK_HEADS:
                y = y * (HEAD_DIM ** -0.5)
        qkv_ref[:, sl] = y
    ab = ab_ref[...]
    lane = lax.broadcasted_iota(jnp.int32, ab.shape, 1)
    n_g = 2 * C_V_HEADS
    z = ab + dtb_ref[...]
    softplus = jnp.maximum(z, 0.0) + jnp.log(1.0 + jnp.exp(-jnp.abs(z)))
    g = -jnp.exp(alog_ref[...]) * softplus
    gb_ref[...] = jnp.where(lane < n_g, g, jax.nn.sigmoid(ab))


def _delta_prep(rows, proj_main, proj_ab, conv_w, a_log, dt_bias):
    r = proj_main.shape[0]
    tm = rows.tm
    hb = tm // HALO
    last_halo = r // HALO - 1
    pad = AB_W - a_log.size
    alog = jnp.pad(a_log.reshape(1, -1), ((0, 0), (0, pad)))
    dtb = jnp.pad(dt_bias.reshape(1, -1), ((0, 0), (0, pad)))
    kern = functools.partial(_delta_prep_kernel, lat_blocks=rows.lat_blocks, seq_blocks=rows.s // tm,
                             ctx_blocks=rows.c // tm)
    return pl.pallas_call(
        kern,
        grid=(r // tm,),
        in_specs=[pl.BlockSpec((HALO, C_CONV), lambda i: (jnp.maximum(i * hb - 1, 0), 0)),
                  pl.BlockSpec((tm, C_CONV), lambda i: (i, 0)),
                  pl.BlockSpec((HALO, C_CONV), lambda i: (jnp.minimum((i + 1) * hb, last_halo), 0)),
                  pl.BlockSpec((tm, AB_W), lambda i: (i, 0)),
                  pl.BlockSpec((CONV_K, C_CONV), lambda i: (0, 0)),
                  _vec_spec(AB_W), _vec_spec(AB_W)],
        out_specs=[pl.BlockSpec((tm, C_CONV), lambda i: (i, 0)), pl.BlockSpec((tm, AB_W), lambda i: (i, 0))],
        out_shape=[jax.ShapeDtypeStruct((r, C_CONV), F32), jax.ShapeDtypeStruct((r, AB_W), F32)],
        compiler_params=_cparams(("parallel",)),
        name="delta_prep",
    )(proj_main, proj_main, proj_main, proj_ab, conv_w, alog, dtb)


def _delta_kernel(qkv_f_ref, gb_f_ref, qkv_r_ref, gb_r_ref, o_f_ref, o_r_ref, state_ref):
    @pl.when(pl.program_id(1) == 0)
    def _():
        state_ref[...] = jnp.zeros_like(state_ref)

    row = lax.broadcasted_iota(jnp.int32, (CHUNK, CHUNK), 0)
    col = lax.broadcasted_iota(jnp.int32, (CHUNK, CHUNK), 1)
    eye = jnp.where(row == col, 1.0, 0.0)
    rep = C_V_HEADS // C_QK_HEADS
    n_sq = int(math.log2(CHUNK)) - 1
    chains = []
    for direction, (qkv_ref, gb_ref, o_ref) in enumerate(((qkv_f_ref, gb_f_ref, o_f_ref),
                                                          (qkv_r_ref, gb_r_ref, o_r_ref))):
        if direction == 0:
            incl, strict, last = row >= col, row > col, CHUNK - 1
        else:
            incl, strict, last = row <= col, row < col, 0
        gb = gb_ref[...]
        gcum = _dot(jnp.where(incl, 1.0, 0.0), gb, precision=HIGHEST)
        gcum_t = gcum.T
        kq = []
        for qh in range(C_QK_HEADS):
            q = qkv_ref[:, qh * HEAD_DIM:(qh + 1) * HEAD_DIM]
            k = qkv_ref[:, C_QK + qh * HEAD_DIM:C_QK + (qh + 1) * HEAD_DIM]
            prod = _dot_nt(jnp.concatenate([k, q], axis=0).astype(BF16), k.astype(BF16))
            kq.append((q, k, prod[:CHUNK], prod[CHUNK:]))
        for h in range(C_V_HEADS):
            lane = direction * C_V_HEADS + h
            q, k, kk, qk = kq[h // rep]
            g_col = gcum[:, lane:lane + 1]
            beta = gb[:, 2 * C_V_HEADS + lane:2 * C_V_HEADS + lane + 1]
            decay = jnp.where(incl, jnp.exp(jnp.minimum(g_col - gcum_t[lane:lane + 1, :], 0.0)), 0.0)
            v = qkv_ref[:, 2 * C_QK + h * HEAD_DIM:2 * C_QK + (h + 1) * HEAD_DIM]
            p = jnp.where(strict, -(kk * beta) * decay, 0.0)
            chains.append(dict(
                h=h, slot=lane, o_ref=o_ref, q=q, k=k, g_col=g_col, e_g=jnp.exp(g_col),
                g_last=gcum[last:last + 1, lane:lane + 1], beta=beta, v=v, qk=qk * decay, p=p, t=eye + p))
    for _ in range(n_sq):
        for ch in chains:
            pb = ch["p"].astype(BF16)
            ch["p"] = _dot(pb, pb)
        for ch in chains:
            ch["t"] = ch["t"] + _dot(ch["t"].astype(BF16), ch["p"].astype(BF16))
    for ch in chains:
        rhs = jnp.concatenate([ch["v"] * ch["beta"], ch["k"] * (ch["beta"] * ch["e_g"])], axis=1)
        uw = _dot(ch["t"].astype(BF16), rhs.astype(BF16))
        ch["u"], ch["w"] = uw[:, :HEAD_DIM], uw[:, HEAD_DIM:]
    for ch in chains:
        ch["state"] = state_ref[ch["slot"]]
        lhs = jnp.concatenate([ch["w"], ch["q"] * ch["e_g"]], axis=0)
        ch["ws_qs"] = _dot(lhs.astype(BF16), ch["state"].astype(BF16))
    for ch in chains:
        ch["v_new"] = (ch["u"] - ch["ws_qs"][:CHUNK]).astype(BF16)
    for ch in chains:
        h = ch["h"]
        ch["o_ref"][:, h * HEAD_DIM:(h + 1) * HEAD_DIM] = ch["ws_qs"][CHUNK:] + _dot(ch["qk"].astype(BF16), ch["v_new"])
    for ch in chains:
        k_dec = ch["k"] * jnp.exp(ch["g_last"] - ch["g_col"])
        state_ref[ch["slot"]] = ch["state"] * jnp.exp(ch["g_last"]) + _dot_tn(k_dec.astype(BF16), ch["v_new"])


def _delta_scan(rows, dqkv, gb):
    b, s, c = rows.b, rows.s, rows.c
    ctx_chunks, lat_chunks = c // CHUNK, s // CHUNK
    ctx_base = rows.n_lat // CHUNK

    def fwd_map(bb, t):
        return (jnp.where(t < ctx_chunks, ctx_base + bb * ctx_chunks + t, bb * lat_chunks + t - ctx_chunks), 0)

    def rev_map(bb, t):
        ctx_i, lat_i = ctx_chunks - 1 - t, lat_chunks - 1 - (t - ctx_chunks)
        return (jnp.where(t < ctx_chunks, ctx_base + bb * ctx_chunks + ctx_i, bb * lat_chunks + lat_i), 0)

    out = jax.ShapeDtypeStruct((rows.r, C_V), F32)
    return pl.pallas_call(
        _delta_kernel,
        grid=(b, ctx_chunks + lat_chunks),
        in_specs=[pl.BlockSpec((CHUNK, C_CONV), fwd_map), pl.BlockSpec((CHUNK, AB_W), fwd_map),
                  pl.BlockSpec((CHUNK, C_CONV), rev_map), pl.BlockSpec((CHUNK, AB_W), rev_map)],
        out_specs=[pl.BlockSpec((CHUNK, C_V), fwd_map), pl.BlockSpec((CHUNK, C_V), rev_map)],
        out_shape=[out, out],
        scratch_shapes=[pltpu.VMEM((2 * C_V_HEADS, HEAD_DIM, HEAD_DIM), F32)],
        compiler_params=_cparams(("parallel", "arbitrary")),
        name="delta_scan",
    )(dqkv, gb, dqkv, gb)


def _merge_kernel(oa_ref, ob_ref, of_ref, or_ref, z_ref, nc_ref, ga_ref, gb_ref, gc_ref,
                  wa_ref, wb_ref, wc_ref, y_ref, oc_ref):
    for h in range(C_V_HEADS):
        sl = slice(h * HEAD_DIM, (h + 1) * HEAD_DIM)
        o = of_ref[:, sl] + or_ref[:, sl]
        oc_ref[:, sl] = (_rms(o, nc_ref[...]) * _silu(z_ref[:, sl])).astype(BF16)
    y = jax.nn.sigmoid(ga_ref[...].astype(F32)) * _dot(oa_ref[...], wa_ref[...])
    y = y + jax.nn.sigmoid(gb_ref[...].astype(F32)) * _dot(ob_ref[...], wb_ref[...])
    y = y + jax.nn.sigmoid(gc_ref[...].astype(F32)) * _dot(oc_ref[...], wc_ref[...])
    y_ref[...] = y.astype(BF16)


def _merge(n_rows, tm, o_a, o_b, o_f, o_r, proj_main, gates, norm_c, w_a, w_b, w_c):
    d = w_a.shape[1]
    z_blk = C_CONV // C_V

    def rspec(w, col=0):
        return pl.BlockSpec((tm, w), lambda i: (i, col))

    def wspec(kdim):
        return pl.BlockSpec((kdim, d), lambda i: (0, 0))

    return pl.pallas_call(
        _merge_kernel,
        grid=(n_rows // tm,),
        in_specs=[rspec(A_Q), rspec(B_Q), rspec(C_V), rspec(C_V), rspec(C_V, z_blk), _vec_spec(HEAD_DIM),
                  rspec(d, 0), rspec(d, 1), rspec(d, 2), wspec(A_Q), wspec(B_Q), wspec(C_V)],
        out_specs=rspec(d),
        out_shape=jax.ShapeDtypeStruct((n_rows, d), BF16),
        scratch_shapes=[pltpu.VMEM((tm, C_V), BF16)],
        compiler_params=_cparams(("parallel",)),
        name="branch_merge",
    )(o_a, o_b, o_f, o_r, proj_main, norm_c, gates, gates, gates, w_a, w_b, w_c)


def _outproj_kernel(y_ref, w_ref, x_ref, gt_ref, g_ref, b_ref, sc_ref, sh_ref, x1_ref, h_ref, *, alpha):
    y = _dot(y_ref[...], w_ref[...])
    x1 = _layernorm(alpha * x_ref[...] + gt_ref[0, 0] * y) * g_ref[...] + b_ref[...]
    x1_ref[...] = x1
    h_ref[...] = (x1 * (1.0 + sc_ref[0, 0]) + sh_ref[0, 0]).astype(BF16)


def _outproj(rows, n_rows, y, w_out, x, mod_l, ln_g, ln_b, alpha):
    d = x.shape[1]
    tm = rows.tm
    rspec = pl.BlockSpec((tm, d), lambda i: (i, 0))
    return pl.pallas_call(
        functools.partial(_outproj_kernel, alpha=alpha),
        grid=(n_rows // tm,),
        in_specs=[rspec, pl.BlockSpec((d, d), lambda i: (0, 0)), rspec, _mod_spec(rows, d, 2),
                  _vec_spec(d), _vec_spec(d), _mod_spec(rows, d, 4), _mod_spec(rows, d, 3)],
        out_specs=[rspec, rspec],
        out_shape=[jax.ShapeDtypeStruct((n_rows, d), F32), jax.ShapeDtypeStruct((n_rows, d), BF16)],
        compiler_params=_cparams(("parallel",)),
        name="out_proj_norm",
    )(y, w_out, x, mod_l, ln_g, ln_b, mod_l, mod_l)


def _router_kernel(x_ref, sc_ref, sh_ref, wr_ref, rb_ref, gate_ref, pos_ref, cnt_ref, before_ref, lstart_ref,
                   total_ref, carry_ref):
    @pl.when(pl.program_id(0) == 0)
    def _():
        carry_ref[...] = jnp.zeros_like(carry_ref)

    h = x_ref[...] * (1.0 + sc_ref[0, 0]) + sh_ref[0, 0]
    scores = jax.nn.sigmoid(_dot(h, wr_ref[...], precision=HIGHEST))
    tm, n_exp = scores.shape
    sel = scores + rb_ref[...]
    lane = lax.broadcasted_iota(jnp.int32, (tm, n_exp), 1)
    slot = lax.broadcasted_iota(jnp.int32, (tm, TOP_K), 1)
    gate = jnp.zeros((tm, TOP_K), F32)
    mask = jnp.zeros((tm, n_exp), F32)
    hits = []
    for k in range(TOP_K):
        best = jnp.max(sel, axis=-1, keepdims=True)
        choice = jnp.min(jnp.where(sel == best, lane, n_exp), axis=-1, keepdims=True)
        hit = lane == choice
        hits.append(hit)
        gate = jnp.where(slot == k, jnp.sum(jnp.where(hit, scores, 0.0), axis=-1, keepdims=True), gate)
        sel = jnp.where(hit, -jnp.inf, sel)
        mask = jnp.where(hit, 1.0, mask)
    gate_ref[...] = gate / jnp.sum(gate, axis=-1, keepdims=True) * ROUTED_SCALE
    r_i = lax.broadcasted_iota(jnp.int32, (tm, tm), 0)
    c_i = lax.broadcasted_iota(jnp.int32, (tm, tm), 1)
    earlier = jnp.where(c_i < r_i, 1.0, 0.0).astype(BF16)
    within = _dot(earlier, mask.astype(BF16))
    cnt = jnp.sum(mask, axis=0, keepdims=True)
    run_rows = jnp.floor((cnt + (RUN - 1)) * (1.0 / RUN)) * RUN
    e_r = lax.broadcasted_iota(jnp.int32, (n_exp, n_exp), 0)
    e_c = lax.broadcasted_iota(jnp.int32, (n_exp, n_exp), 1)
    lstart = _dot(jnp.broadcast_to(run_rows, (8, n_exp)), jnp.where(e_r < e_c, 1.0, 0.0), precision=HIGHEST)[0:1]
    local = within + lstart
    pos = jnp.zeros((tm, TOP_K), F32)
    for k in range(TOP_K):
        pos = jnp.where(slot == k, jnp.sum(jnp.where(hits[k], local, 0.0), axis=-1, keepdims=True), pos)
    pos_ref[...] = pos.astype(jnp.int32)
    cnt_ref[0] = cnt.astype(jnp.int32)
    before_ref[0] = carry_ref[...].astype(jnp.int32)
    lstart_ref[0] = lstart.astype(jnp.int32)
    carry_ref[...] = carry_ref[...] + cnt
    total_ref[...] = carry_ref[...]


def _router(rows, n_rows, x1, mod_l, w_router, router_bias):
    d, n_exp = w_router.shape
    tm = rows.tm
    n_blk = n_rows // tm
    kspec = pl.BlockSpec((tm, TOP_K), lambda i: (i, 0))
    espec = pl.BlockSpec((1, 1, n_exp), lambda i: (i, 0, 0))
    per_block = jax.ShapeDtypeStruct((n_blk, 1, n_exp), jnp.int32)
    return pl.pallas_call(
        _router_kernel,
        grid=(n_blk,),
        in_specs=[pl.BlockSpec((tm, d), lambda i: (i, 0)), _mod_spec(rows, d, 4), _mod_spec(rows, d, 3),
                  pl.BlockSpec((d, n_exp), lambda i: (0, 0)), _vec_spec(n_exp)],
        out_specs=[kspec, kspec, espec, espec, espec, _vec_spec(n_exp)],
        out_shape=[jax.ShapeDtypeStruct((n_rows, TOP_K), F32), jax.ShapeDtypeStruct((n_rows, TOP_K), jnp.int32),
                   per_block, per_block, per_block, jax.ShapeDtypeStruct((1, n_exp), F32)],
        scratch_shapes=[pltpu.VMEM((1, n_exp), F32)],
        compiler_params=_cparams(("arbitrary",)),
        name="router",
    )(x1, mod_l, mod_l, w_router, router_bias.reshape(1, n_exp))


def _local_rows(tm, n_exp):
    return tm * TOP_K + n_exp * RUN


def _token_rows(tokens, s, per_token):
    return pl.ds(s, tokens, stride=per_token) if per_token > 1 else pl.ds(s, tokens)


def _pack_words(lo, hi):
    return (lax.bitcast_convert_type(lo, jnp.uint32) >> 16) | (lax.bitcast_convert_type(hi, jnp.uint32)
                                                               & jnp.uint32(0xFFFF0000))


def _unpack_words(p):
    return (lax.bitcast_convert_type(p << 16, F32).astype(BF16),
            lax.bitcast_convert_type(p & jnp.uint32(0xFFFF0000), F32).astype(BF16))


def _run_dmas(n_exp, cnt_ref, src_start, dst_start, src_ref, dst_ref, sem, per_token):
    total = jnp.int32(0)
    rows = RUN * per_token
    for e in range(n_exp):
        n_run = (cnt_ref[0, 0, e] + (RUN - 1)) // RUN
        s0, d0 = src_start(e), dst_start(e)

        def body(c, carry, s0=s0, d0=d0):
            pltpu.make_async_copy(src_ref.at[pl.ds(pl.multiple_of((s0 + c * RUN) * per_token, per_token), rows)],
                                  dst_ref.at[pl.ds(pl.multiple_of((d0 + c * RUN) * per_token, per_token), rows)],
                                  sem).start()
            return carry

        lax.fori_loop(0, n_run, body, 0)
        total = total + n_run
    return total


def _wait_runs(n, src_ref, dst_ref, sem, per_token):
    rows = RUN * per_token

    def body(c, carry):
        pltpu.make_async_copy(src_ref.at[pl.ds(0, rows)], dst_ref.at[pl.ds(0, rows)], sem).wait()
        return carry

    lax.fori_loop(0, n, body, 0)


def _dispatch_kernel(gs_ref, count_ref, padded_ref, post_ref, cnt_ref, carry_ref, ls_ref, h_ref, xs_ref,
                     loc, zbuf, sem, pending_ref, *, per_token, n_exp):
    i = pl.program_id(0)
    tm, d = h_ref.shape
    half = d // 2
    n_local = loc.shape[1] // per_token
    chunk = 512
    buf = i % 2
    for r0 in range(0, n_local, chunk):
        p_iota = r0 + lax.broadcasted_iota(jnp.int32, (chunk, tm), 0)
        onehot = jnp.zeros((chunk, tm), F32)
        for k in range(TOP_K):
            onehot = jnp.where(p_iota == post_ref[k:k + 1, :], 1.0, onehot)
        onehot = onehot.astype(BF16)
        words = _pack_words(_dot(onehot, h_ref[:, :half]), _dot(onehot, h_ref[:, half:]))
        for s in range(per_token):
            loc[buf, _token_rows(chunk, r0 * per_token + s, per_token), :] = words[:, s * LANES:(s + 1) * LANES]

    @pl.when(i > 0)
    def _():
        _wait_runs(pending_ref[0], loc.at[0], xs_ref, sem, per_token)

    pending_ref[0] = _run_dmas(n_exp, cnt_ref, lambda e: ls_ref[0, 0, e], lambda e: gs_ref[e] + carry_ref[0, 0, e],
                               loc.at[buf], xs_ref, sem, per_token)

    @pl.when(i == pl.num_programs(0) - 1)
    def _():
        _wait_runs(pending_ref[0], loc.at[0], xs_ref, sem, per_token)
        zbuf[...] = jnp.zeros_like(zbuf)
        rows = RUN * per_token
        n_zero = jnp.int32(0)
        for e in range(n_exp):
            first = gs_ref[e] + count_ref[e]
            n_piece = (padded_ref[e] - count_ref[e]) // RUN

            def body(c, carry, first=first):
                pltpu.make_async_copy(zbuf, xs_ref.at[pl.ds(pl.multiple_of((first + c * RUN) * per_token, per_token),
                                                            rows)], sem).start()
                return carry

            lax.fori_loop(0, n_piece, body, 0)
            n_zero = n_zero + n_piece
        used = gs_ref[n_exp - 1] + padded_ref[n_exp - 1]
        n_tail = (xs_ref.shape[0] // per_token - used) // RUN

        def tail_body(c, carry):
            pltpu.make_async_copy(zbuf, xs_ref.at[pl.ds(pl.multiple_of((used + c * RUN) * per_token, per_token),
                                                        rows)], sem).start()
            return carry

        lax.fori_loop(0, n_tail, tail_body, 0)
        _wait_runs(n_zero + n_tail, loc.at[0], xs_ref, sem, per_token)
        for e in range(n_exp):
            end = gs_ref[e] + padded_ref[e]
            pltpu.make_async_copy(zbuf, xs_ref.at[pl.ds(pl.multiple_of((end - RUN) * per_token, per_token), rows)],
                                  sem).start()
        _wait_runs(n_exp, loc.at[0], xs_ref, sem, per_token)


def _dispatch(rows, n_rows, h2, pos_t, cnt, carry, lstart, group_start, count, padded, n_slots):
    d = h2.shape[1]
    tm = rows.tm
    n_exp = count.shape[0]
    per_token = d // 2 // LANES
    n_local = _local_rows(tm, n_exp)
    smem_blk = pl.BlockSpec((1, 1, n_exp), lambda i, *_: (i, 0, 0), memory_space=pltpu.SMEM)
    grid_spec = pltpu.PrefetchScalarGridSpec(
        num_scalar_prefetch=3,
        grid=(n_rows // tm,),
        in_specs=[pl.BlockSpec((TOP_K, tm), lambda i, *_: (0, i)), smem_blk, smem_blk, smem_blk,
                  pl.BlockSpec((tm, d), lambda i, *_: (i, 0))],
        out_specs=pl.BlockSpec(memory_space=pl.ANY),
        scratch_shapes=[pltpu.VMEM((2, n_local * per_token, LANES), jnp.uint32),
                        pltpu.VMEM((RUN * per_token, LANES), jnp.uint32),
                        pltpu.SemaphoreType.DMA(()), pltpu.SMEM((1,), jnp.int32)],
    )
    return pl.pallas_call(
        functools.partial(_dispatch_kernel, per_token=per_token, n_exp=n_exp),
        grid_spec=grid_spec,
        out_shape=jax.ShapeDtypeStruct((n_slots * per_token, LANES), jnp.uint32),
        compiler_params=_cparams(("arbitrary",)),
        name="moe_dispatch",
    )(group_start, count, padded, pos_t, cnt, carry, lstart, h2)


def _expert_block_kernel(be_ref, nu_ref, x_ref, w1_ref, w3_ref, w2_ref, o_ref, xs_ref, w1b, w3b, w2b, prev_ref):
    blk = pl.program_id(0)
    d = xs_ref.shape[1]

    @pl.when(blk == 0)
    def _():
        prev_ref[0] = -1

    @pl.when(blk < nu_ref[0])
    def _():
        e = be_ref[blk]

        @pl.when(e != prev_ref[0])
        def _():
            w1b[...] = w1_ref[0, 0].astype(BF16)
            w3b[...] = w3_ref[0, 0].astype(BF16)
            w2b[...] = w2_ref[0, 0].astype(BF16)
            prev_ref[0] = e

        half = d // 2
        per_token = half // LANES
        for s in range(per_token):
            lo, hi = _unpack_words(x_ref[_token_rows(MOE_ROWS, s, per_token), :])
            xs_ref[:, s * LANES:(s + 1) * LANES] = lo
            xs_ref[:, half + s * LANES:half + (s + 1) * LANES] = hi
        hidden = _silu(_dot(xs_ref[...], w1b[...])) * _dot(xs_ref[...], w3b[...])
        y = _dot(hidden.astype(BF16), w2b[...])
        for s in range(per_token):
            o_ref[_token_rows(MOE_ROWS, s, per_token), :] = _pack_words(
                y[:, s * LANES:(s + 1) * LANES].astype(BF16).astype(F32),
                y[:, half + s * LANES:half + (s + 1) * LANES].astype(BF16).astype(F32))

    @pl.when(blk >= nu_ref[0])
    def _():
        o_ref[...] = jnp.zeros_like(o_ref)


def _expert_blocks(xs, block_expert, n_used, w1, w3, w2, layer):
    d, ff = w1.shape[-2:]
    per_token = d // 2 // LANES
    n_blocks = block_expert.shape[0]
    blk_spec = pl.BlockSpec((MOE_ROWS * per_token, LANES), lambda i, be, nu: (i, 0))
    in_spec = pl.BlockSpec((MOE_ROWS * per_token, LANES), lambda i, be, nu: (jnp.minimum(i, nu[0] - 1), 0))

    def wmap(i, be, nu):
        return (layer, be[i], 0, 0)

    grid_spec = pltpu.PrefetchScalarGridSpec(
        num_scalar_prefetch=2,
        grid=(n_blocks,),
        in_specs=[in_spec, pl.BlockSpec((1, 1, d, ff), wmap), pl.BlockSpec((1, 1, d, ff), wmap),
                  pl.BlockSpec((1, 1, ff, d), wmap)],
        out_specs=blk_spec,
        scratch_shapes=[pltpu.VMEM((MOE_ROWS, d), BF16),
                        pltpu.VMEM((d, ff), BF16), pltpu.VMEM((d, ff), BF16), pltpu.VMEM((ff, d), BF16),
                        pltpu.SMEM((1,), jnp.int32)],
    )
    return pl.pallas_call(
        _expert_block_kernel,
        grid_spec=grid_spec,
        out_shape=jax.ShapeDtypeStruct(xs.shape, jnp.uint32),
        compiler_params=_cparams(("arbitrary",)),
        name="experts",
    )(block_expert, n_used, xs, w1, w3, w2)


def _gather_combine_kernel(*refs, alpha, emit_h, per_token, n_exp):
    gs_ref, pos_ref, gate_ref, cnt_ref, carry_ref, ls_ref, ys_ref, h_ref, ws1_ref, ws3_ref, ws2_ref = refs[:11]
    x_ref, gt_ref, g_ref, b_ref = refs[11:15]
    n_out = 2 if emit_h else 1
    out_refs = refs[-(n_out + 3):-3]
    loc, wts_ref, sem = refs[-3:]
    tm, d = h_ref.shape
    n_local = loc.shape[0] // per_token

    @pl.when(pl.program_id(0) == 0)
    def _():
        loc[...] = jnp.zeros_like(loc)

    n_dma = _run_dmas(n_exp, cnt_ref, lambda e: gs_ref[e] + carry_ref[0, 0, e], lambda e: ls_ref[0, 0, e],
                      ys_ref, loc, sem, per_token)
    hidden = _silu(_dot(h_ref[...], ws1_ref[...])) * _dot(h_ref[...], ws3_ref[...])
    f = _dot(hidden.astype(BF16), ws2_ref[...])
    p_iota = lax.broadcasted_iota(jnp.int32, (tm, n_local), 1)
    weights = jnp.zeros((tm, n_local), F32)
    for k in range(TOP_K):
        weights = jnp.where(p_iota == pos_ref[:, k:k + 1], gate_ref[:, k:k + 1], weights)
    wts_ref[...] = weights.astype(BF16)
    _wait_runs(n_dma, ys_ref, loc, sem, per_token)
    group = 2 if per_token % 2 == 0 else 1
    lo_cols, hi_cols = [], []
    for s0 in range(0, per_token, group):
        planes = [_unpack_words(loc[_token_rows(n_local, s, per_token), :]) for s in range(s0, s0 + group)]
        lo_cols.append(_dot(wts_ref[...], jnp.concatenate([lo for lo, _ in planes], axis=1)))
        hi_cols.append(_dot(wts_ref[...], jnp.concatenate([hi for _, hi in planes], axis=1)))
    f = f + jnp.concatenate(lo_cols + hi_cols, axis=1)
    x2 = _layernorm(alpha * x_ref[...] + gt_ref[0, 0] * f) * g_ref[...] + b_ref[...]
    out_refs[0][...] = x2
    if emit_h:
        sc_ref, sh_ref = refs[15:17]
        out_refs[1][...] = (x2 * (1.0 + sc_ref[0, 0]) + sh_ref[0, 0]).astype(BF16)


def _gather_combine(rows, n_rows, ys, pos, gate, cnt, carry, lstart, group_start, h2, ws1, ws3, ws2, x1, mod_l,
                    ln_g, ln_b, alpha, mod_next):
    d = x1.shape[1]
    sf = ws1.shape[1]
    tm = rows.tm
    n_exp = group_start.shape[0]
    per_token = d // 2 // LANES
    n_local = _local_rows(tm, n_exp)

    def rmap(i, *_):
        return (i, 0)

    def const(i, *_):
        return (0, 0)

    def mod_spec(which):
        return pl.BlockSpec((1, 1, 1, d), lambda i, *_: (rows.group(i), which, 0, 0))

    rspec = pl.BlockSpec((tm, d), rmap)
    kspec = pl.BlockSpec((tm, TOP_K), rmap)
    smem_blk = pl.BlockSpec((1, 1, n_exp), lambda i, *_: (i, 0, 0), memory_space=pltpu.SMEM)
    in_specs = [kspec, kspec, smem_blk, smem_blk, smem_blk, pl.BlockSpec(memory_space=pl.ANY), rspec,
                pl.BlockSpec((d, sf), const), pl.BlockSpec((d, sf), const), pl.BlockSpec((sf, d), const),
                rspec, mod_spec(5), pl.BlockSpec((1, d), const), pl.BlockSpec((1, d), const)]
    args = [pos, gate, cnt, carry, lstart, ys, h2, ws1, ws3, ws2, x1, mod_l, ln_g, ln_b]
    emit_h = mod_next is not None
    if emit_h:
        in_specs += [mod_spec(1), mod_spec(0)]
        args += [mod_next, mod_next]
        out_specs = [rspec, rspec]
        out_shape = [jax.ShapeDtypeStruct((n_rows, d), F32), jax.ShapeDtypeStruct((n_rows, d), BF16)]
    else:
        out_specs = rspec
        out_shape = jax.ShapeDtypeStruct((n_rows, d), F32)
    grid_spec = pltpu.PrefetchScalarGridSpec(
        num_scalar_prefetch=1,
        grid=(n_rows // tm,),
        in_specs=in_specs,
        out_specs=out_specs,
        scratch_shapes=[pltpu.VMEM((n_local * per_token, LANES), jnp.uint32), pltpu.VMEM((tm, n_local), BF16),
                        pltpu.SemaphoreType.DMA(())],
    )
    return pl.pallas_call(
        functools.partial(_gather_combine_kernel, alpha=alpha, emit_h=emit_h, per_token=per_token, n_exp=n_exp),
        grid_spec=grid_spec,
        out_shape=out_shape,
        compiler_params=_cparams(("arbitrary",)),
        name="moe_combine_norm",
    )(group_start, *args)


def _permute_w_in(w):
    a_end = A_Q + 2 * A_KV
    b_end = a_end + B_Q + 2 * B_KV
    c_end = b_end + C_CONV + C_V
    ab_end = c_end + 4 * C_V_HEADS
    main = jnp.concatenate([w[:, b_end:c_end], w[:, :b_end]], axis=1).astype(BF16)
    ab = jnp.pad(w[:, c_end:ab_end], ((0, 0), (0, AB_W - 4 * C_V_HEADS))).astype(BF16)
    gates = w[:, ab_end:].astype(BF16)
    return main, gates, ab


def kernel(x, c, ctx, c_ctx, w_mod, b_mod, w_in, q_norm_a, k_norm_a, sink_b, conv_c, a_log_c, dt_bias_c, norm_c,
           w_br_a, w_br_b, w_br_c, w_out, ln1_g, ln1_b, w_router, router_bias, w1, w3, w2, ws1, ws3, ws2,
           ln2_g, ln2_b):
    b, s, d = x.shape
    n_ctx = ctx.shape[1]
    depth = w_mod.shape[0]
    n_exp = w_router.shape[-1]
    alpha = (2.0 * depth) ** 0.25
    rows = _Rows(b, s, n_ctx)
    tm = rows.tm

    c_all = jnp.concatenate([c, c_ctx[None, :], jnp.zeros((8 - b - 1, d), F32)], axis=0)
    mod = _modulation(c_all, w_mod, b_mod).reshape(depth, 8, N_MOD, 1, d)
    tables = _rope_tables(s, tm)

    x_cur, h = _entry(rows, x.reshape(b * s, d), ctx.reshape(b * n_ctx, d), mod[0])

    for l in range(depth):
        need_ctx = l < depth - 1
        n_rows = rows.r if need_ctx else rows.n_lat
        mod_l = mod[l]
        w_main, w_gates, w_ab = _permute_w_in(w_in[l])
        proj_main = _matmul(h, w_main, F32)
        gates = _matmul(h, w_gates, BF16)
        proj_ab = _matmul(h, w_ab, F32)

        qkv = _attn_prep(rows, proj_main, tables, q_norm_a[l].reshape(1, -1), k_norm_a[l].reshape(1, -1))
        a_k, a_v = A_Q, A_Q + A_KV
        o_a = _attn_full(qkv, batch=b, n_q=s, q_base=0, sources=[(rows.n_lat, n_ctx), (0, s)],
                         q_col=0, k_col=a_k, v_col=a_v, heads=A_HEADS, kv_heads=A_KV_HEADS)
        o_b = _attn_window(qkv, sink_b[l], rows)
        if need_ctx:
            b_q = A_Q + 2 * A_KV
            o_a_c = _attn_full(qkv, batch=b, n_q=n_ctx, q_base=rows.n_lat, sources=[(rows.n_lat, n_ctx)],
                               q_col=0, k_col=a_k, v_col=a_v, heads=A_HEADS, kv_heads=A_KV_HEADS)
            o_b_c = _attn_full(qkv, batch=b, n_q=n_ctx, q_base=rows.n_lat, sources=[(rows.n_lat, n_ctx)],
                               q_col=b_q, k_col=b_q + B_Q, v_col=b_q + B_Q + B_KV, heads=B_HEADS,
                               kv_heads=B_KV_HEADS, sink=sink_b[l])
            o_a = jnp.concatenate([o_a, o_a_c], axis=0)
            o_b = jnp.concatenate([o_b, o_b_c], axis=0)

        dqkv, gbeta = _delta_prep(rows, proj_main, proj_ab, conv_c[l], a_log_c[l], dt_bias_c[l])
        o_f, o_r = _delta_scan(rows, dqkv, gbeta)

        y = _merge(n_rows, tm, o_a, o_b, o_f, o_r, proj_main, gates, norm_c[l].reshape(1, -1),
                   w_br_a[l].astype(BF16), w_br_b[l].astype(BF16), w_br_c[l].astype(BF16))
        x1, h2 = _outproj(rows, n_rows, y, w_out[l].astype(BF16), x_cur, mod_l,
                          ln1_g[l].reshape(1, -1), ln1_b[l].reshape(1, -1), alpha)

        gate, pos, cnt, before, lstart, counts = _router(rows, n_rows, x1, mod_l, w_router[l], router_bias[l])
        counts = counts[0].astype(jnp.int32)
        padded = (counts + RUN + MOE_ROWS - 1) // MOE_ROWS * MOE_ROWS
        padded_end = jnp.cumsum(padded)
        group_start = padded_end - padded
        n_blocks = -(-(n_rows * TOP_K + n_exp * RUN) // MOE_ROWS) + n_exp
        block_start = jnp.arange(n_blocks, dtype=jnp.int32)[:, None] * MOE_ROWS
        block_expert = jnp.minimum(jnp.sum((padded_end[None, :] <= block_start).astype(jnp.int32), axis=-1), n_exp - 1)
        n_used = (padded_end[-1:] // MOE_ROWS).astype(jnp.int32)
        xs = _dispatch(rows, n_rows, h2, pos.T, cnt, before, lstart, group_start, counts, padded,
                       n_blocks * MOE_ROWS)
        ys = _expert_blocks(xs, block_expert, n_used, w1, w3, w2, l)

        mod_next = mod[l + 1] if need_ctx else None
        out = _gather_combine(rows, n_rows, ys, pos, gate, cnt, before, lstart, group_start, h2,
                              ws1[l].astype(BF16), ws3[l].astype(BF16), ws2[l].astype(BF16),
                              x1, mod_l, ln2_g[l].reshape(1, -1), ln2_b[l].reshape(1, -1), alpha, mod_next)
        if need_ctx:
            x_cur, h = out
        else:
            x_cur = out
    return x_cur[:rows.n_lat].reshape(b, s, d)
```

```python
import functools
import math

import jax
import jax.numpy as jnp
from jax import lax
from jax.experimental import pallas as pl
from jax.experimental.pallas import tpu as pltpu

F32 = jnp.float32
BF16 = jnp.bfloat16
HIGHEST = lax.Precision.HIGHEST

HEAD_DIM = 128
GRID_W = 64
ROPE_THETA = 10000.0
A_HEADS, A_KV_HEADS = 8, 2
B_HEADS, B_KV_HEADS = 8, 2
WINDOW = 128
Q_BLOCK = 128
C_QK_HEADS, C_V_HEADS = 4, 8
CONV_K = 5
CHUNK = 64
TOP_K = 8
ROUTED_SCALE = 2.5
LN_EPS = 1e-5
RMS_EPS = 1e-6
LOG2_E = 1.4426950408889634
MOE_ROWS = 256
RUN = 16
DISPATCH_TAIL = 64
DISPATCH_CHUNKS = 4
LANES = 128
N_MOD = 6
HALO = 8
VMEM_LIMIT = 56 * 2**20

A_Q, A_KV = A_HEADS * HEAD_DIM, A_KV_HEADS * HEAD_DIM
B_Q, B_KV = B_HEADS * HEAD_DIM, B_KV_HEADS * HEAD_DIM
C_QK, C_V = C_QK_HEADS * HEAD_DIM, C_V_HEADS * HEAD_DIM
C_CONV = 2 * C_QK + C_V
ATT_W = A_Q + 2 * A_KV + B_Q + 2 * B_KV
MAIN_W = C_CONV + C_V + ATT_W
AB_W = 128


def _cparams(sem):
    return pltpu.CompilerParams(dimension_semantics=sem, vmem_limit_bytes=VMEM_LIMIT)


def _pick(n, prefs):
    for p in prefs:
        if n % p == 0:
            return p
    return n


def _dot(a, b, **kw):
    return jnp.dot(a, b, preferred_element_type=F32, **kw)


def _dot_nt(a, b):
    return lax.dot_general(a, b, (((1,), (1,)), ((), ())), preferred_element_type=F32)


def _dot_tn(a, b):
    return lax.dot_general(a, b, (((0,), (0,)), ((), ())), preferred_element_type=F32)


def _silu(x):
    return x * jax.nn.sigmoid(x)


def _mod_kernel(c_ref, w_ref, b_ref, o_ref):
    o_ref[0] = _dot(_silu(c_ref[...]), w_ref[0], precision=HIGHEST) + b_ref[0]


def _modulation(c_all, w_mod, b_mod):
    depth, d, width = w_mod.shape
    tn = _pick(width, (1024, 512, 256, 128))
    return pl.pallas_call(
        _mod_kernel,
        grid=(depth, width // tn),
        in_specs=[pl.BlockSpec((8, d), lambda l, j: (0, 0)),
                  pl.BlockSpec((1, d, tn), lambda l, j: (l, 0, j)),
                  pl.BlockSpec((1, 1, tn), lambda l, j: (l, 0, j))],
        out_specs=pl.BlockSpec((1, 8, tn), lambda l, j: (l, 0, j)),
        out_shape=jax.ShapeDtypeStruct((depth, 8, width), F32),
        compiler_params=_cparams(("parallel", "parallel")),
        name="modulation",
    )(c_all, w_mod, b_mod.reshape(depth, 1, width))


class _Rows:
    def __init__(self, b, s, c):
        self.b, self.s, self.c = b, s, c
        self.tm = 256 if c % 256 == 0 else 128
        self.n_lat = b * s
        self.r = b * s + b * c
        self.lat_blocks = self.n_lat // self.tm
        self.blocks = self.r // self.tm

    def group(self, i):
        return jnp.where(i < self.lat_blocks, i // (self.s // self.tm), self.b)


def _mod_spec(rows, d, which):
    return pl.BlockSpec((1, 1, 1, d), lambda i: (rows.group(i), which, 0, 0))


def _vec_spec(d):
    return pl.BlockSpec((1, d), lambda i: (0, 0))


def _layernorm(x):
    mu = jnp.mean(x, axis=-1, keepdims=True)
    xc = x - mu
    var = jnp.mean(xc * xc, axis=-1, keepdims=True)
    return xc * lax.rsqrt(var + LN_EPS)


def _entry_kernel(lat_ref, ctx_ref, sc_ref, sh_ref, x0_ref, h_ref, *, lat_blocks):
    def emit(src_ref):
        xn = _layernorm(src_ref[...])
        x0_ref[...] = xn
        h_ref[...] = (xn * (1.0 + sc_ref[0, 0]) + sh_ref[0, 0]).astype(BF16)

    is_lat = pl.program_id(0) < lat_blocks
    pl.when(is_lat)(functools.partial(emit, lat_ref))
    pl.when(jnp.logical_not(is_lat))(functools.partial(emit, ctx_ref))


def _entry(rows, x_lat, x_ctx, mod_l):
    d = x_lat.shape[1]
    tm = rows.tm
    lat_blocks = rows.lat_blocks
    row_spec = pl.BlockSpec((tm, d), lambda i: (i, 0))
    return pl.pallas_call(
        functools.partial(_entry_kernel, lat_blocks=lat_blocks),
        grid=(rows.r // tm,),
        in_specs=[pl.BlockSpec((tm, d), lambda i: (jnp.minimum(i, lat_blocks - 1), 0)),
                  pl.BlockSpec((tm, d), lambda i: (jnp.maximum(i - lat_blocks, 0), 0)),
                  _mod_spec(rows, d, 1), _mod_spec(rows, d, 0)],
        out_specs=[row_spec, row_spec],
        out_shape=[jax.ShapeDtypeStruct((rows.r, d), F32), jax.ShapeDtypeStruct((rows.r, d), BF16)],
        compiler_params=_cparams(("parallel",)),
        name="entry_norm",
    )(x_lat, x_ctx, mod_l, mod_l)


def _mm_kernel(a_ref, w_ref, o_ref):
    o_ref[...] = _dot(a_ref[...], w_ref[...]).astype(o_ref.dtype)


def _matmul(a, w, out_dtype):
    m, k = a.shape
    n = w.shape[1]
    tm = _pick(m, (1024, 512, 256, 128))
    tn = _pick(n, (1024, 512, 256, 128))
    return pl.pallas_call(
        _mm_kernel,
        grid=(n // tn, m // tm),
        in_specs=[pl.BlockSpec((tm, k), lambda j, i: (i, 0)),
                  pl.BlockSpec((k, tn), lambda j, i: (0, j))],
        out_specs=pl.BlockSpec((tm, tn), lambda j, i: (i, j)),
        out_shape=jax.ShapeDtypeStruct((m, n), out_dtype),
        compiler_params=_cparams(("parallel", "parallel")),
        name="in_proj",
    )(a, w)


def _rope(t, cos, sin_lo, sin_hi):
    return t * cos + pltpu.roll(t, HEAD_DIM - 32, 1) * sin_lo + pltpu.roll(t, 32, 1) * sin_hi


def _rms(t, g):
    return t * lax.rsqrt(jnp.mean(t * t, axis=-1, keepdims=True) + RMS_EPS) * g


def _attn_prep_kernel(x_ref, cos_ref, slo_ref, shi_ref, qn_ref, kn_ref, o_ref):
    cos, slo, shi = cos_ref[...], slo_ref[...], shi_ref[...]
    scale = HEAD_DIM ** -0.5 * LOG2_E

    def head(j):
        return x_ref[:, j * HEAD_DIM:(j + 1) * HEAD_DIM]

    def put(j, v):
        o_ref[:, j * HEAD_DIM:(j + 1) * HEAD_DIM] = v.astype(BF16)

    j = 0
    for _ in range(A_HEADS):
        put(j, _rope(_rms(head(j), qn_ref[...]), cos, slo, shi) * scale)
        j += 1
    for _ in range(A_KV_HEADS):
        put(j, _rope(_rms(head(j), kn_ref[...]), cos, slo, shi))
        j += 1
    for _ in range(A_KV_HEADS):
        put(j, head(j))
        j += 1
    for _ in range(B_HEADS):
        put(j, _rope(head(j), cos, slo, shi) * scale)
        j += 1
    for _ in range(B_KV_HEADS):
        put(j, _rope(head(j), cos, slo, shi))
        j += 1
    for _ in range(B_KV_HEADS):
        put(j, head(j))
        j += 1


def _attn_prep(rows, proj_main, tables, q_norm, k_norm):
    r = proj_main.shape[0]
    tm = rows.tm
    pos_blocks = rows.s // tm
    col_blk = (C_CONV + C_V) // ATT_W
    assert col_blk * ATT_W == C_CONV + C_V

    def tab_map(i):
        return (jnp.where(i < rows.lat_blocks, i % pos_blocks, pos_blocks), 0)

    tab_spec = pl.BlockSpec((tm, HEAD_DIM), tab_map)
    return pl.pallas_call(
        _attn_prep_kernel,
        grid=(r // tm,),
        in_specs=[pl.BlockSpec((tm, ATT_W), lambda i: (i, col_blk)), tab_spec, tab_spec, tab_spec,
                  _vec_spec(HEAD_DIM), _vec_spec(HEAD_DIM)],
        out_specs=pl.BlockSpec((tm, ATT_W), lambda i: (i, 0)),
        out_shape=jax.ShapeDtypeStruct((r, ATT_W), BF16),
        compiler_params=_cparams(("parallel",)),
        name="attn_prep",
    )(proj_main, *tables, q_norm, k_norm)


def _rope_tables(s, tm):
    row = jnp.repeat(jnp.arange(s // GRID_W, dtype=F32), GRID_W)
    col = jnp.tile(jnp.arange(GRID_W, dtype=F32), s // GRID_W)
    axis_dim = HEAD_DIM // 2
    inv_freq = ROPE_THETA ** (-jnp.arange(0, axis_dim, 2, dtype=F32) / axis_dim)
    ang_r = row[:, None] * inv_freq[None, :]
    ang_c = col[:, None] * inv_freq[None, :]
    ang = jnp.concatenate([ang_r, ang_r, ang_c, ang_c], axis=-1)
    cos, sin = jnp.cos(ang), jnp.sin(ang)
    lo = (jnp.arange(HEAD_DIM) % axis_dim) < (axis_dim // 2)
    sin_lo = jnp.where(lo, -sin, 0.0)
    sin_hi = jnp.where(lo, 0.0, sin)
    one, zero = jnp.ones((tm, HEAD_DIM), F32), jnp.zeros((tm, HEAD_DIM), F32)
    return (jnp.concatenate([cos, one]), jnp.concatenate([sin_lo, zero]), jnp.concatenate([sin_hi, zero]))


def _attn_full_kernel(*refs, n_src, has_sink, groups):
    if has_sink:
        sink_ref, refs = refs[0], refs[1:]
    q_ref, o_ref = refs[0], refs[1 + 2 * n_src]
    kv = refs[1:1 + 2 * n_src]
    kvh = pl.program_id(1)
    for g in range(groups):
        q = q_ref[:, g * HEAD_DIM:(g + 1) * HEAD_DIM]
        scores = [_dot_nt(q, kv[2 * j][...]) for j in range(n_src)]
        m = functools.reduce(jnp.maximum, [jnp.max(s, axis=-1, keepdims=True) for s in scores])
        if has_sink:
            sink = sink_ref[kvh * groups + g] * LOG2_E
            m = jnp.maximum(m, sink)
        es = [jnp.exp2(s - m) for s in scores]
        den = functools.reduce(jnp.add, [jnp.sum(e, axis=-1, keepdims=True) for e in es])
        if has_sink:
            den = den + jnp.exp2(sink - m)
        acc = functools.reduce(jnp.add, [_dot(es[j].astype(BF16), kv[2 * j + 1][...]) for j in range(n_src)])
        o_ref[:, g * HEAD_DIM:(g + 1) * HEAD_DIM] = (acc / den).astype(o_ref.dtype)


def _attn_full(qkv, *, batch, n_q, q_base, sources, q_col, k_col, v_col, heads, kv_heads, sink=None):
    groups = heads // kv_heads
    tq = _pick(n_q, (256, 128))
    qw = groups * HEAD_DIM
    nq = n_q // tq
    in_specs = []
    args = []
    if sink is not None:
        in_specs.append(pl.BlockSpec(memory_space=pltpu.SMEM))
        args.append(sink)
    in_specs.append(pl.BlockSpec((tq, qw), lambda b, h, i: (q_base // tq + b * nq + i, q_col // qw + h)))
    args.append(qkv)
    for base, n in sources:
        for col in (k_col, v_col):
            in_specs.append(pl.BlockSpec(
                (n, HEAD_DIM), lambda b, h, i, base=base, n=n, col=col: (base // n + b, col // HEAD_DIM + h)))
            args.append(qkv)
    kern = functools.partial(_attn_full_kernel, n_src=len(sources), has_sink=sink is not None, groups=groups)
    return pl.pallas_call(
        kern,
        grid=(batch, kv_heads, nq),
        in_specs=in_specs,
        out_specs=pl.BlockSpec((tq, qw), lambda b, h, i: (b * nq + i, h)),
        out_shape=jax.ShapeDtypeStruct((batch * n_q, heads * HEAD_DIM), BF16),
        compiler_params=_cparams(("parallel", "parallel", "arbitrary")),
        name="attn_full",
    )(*args)


def _attn_win_kernel(sink_ref, q_ref, k_ref, v_ref, kc_ref, vc_ref, o_ref, *, groups, seq):
    kvh = pl.program_id(1)
    n = pl.program_id(2)
    span = 3 * Q_BLOCK
    start = jnp.clip((n - 1) * Q_BLOCK, 0, seq - span)
    start = pl.multiple_of(start, Q_BLOCK)
    kb = k_ref[pl.ds(start, span), :]
    vb = v_ref[pl.ds(start, span), :]
    q_pos = n * Q_BLOCK + lax.broadcasted_iota(jnp.int32, (Q_BLOCK, span), 0)
    k_pos = start + lax.broadcasted_iota(jnp.int32, (Q_BLOCK, span), 1)
    valid = jnp.abs(k_pos - q_pos) <= WINDOW
    for g in range(groups):
        q = q_ref[:, g * HEAD_DIM:(g + 1) * HEAD_DIM]
        s_win = jnp.where(valid, _dot_nt(q, kb), -jnp.inf)
        s_ctx = _dot_nt(q, kc_ref[...])
        sink = sink_ref[kvh * groups + g] * LOG2_E
        m = jnp.maximum(jnp.maximum(jnp.max(s_win, axis=-1, keepdims=True),
                                    jnp.max(s_ctx, axis=-1, keepdims=True)), sink)
        e_win = jnp.exp2(s_win - m)
        e_ctx = jnp.exp2(s_ctx - m)
        den = (jnp.sum(e_win, axis=-1, keepdims=True) + jnp.sum(e_ctx, axis=-1, keepdims=True)
               + jnp.exp2(sink - m))
        acc = _dot(e_win.astype(BF16), vb) + _dot(e_ctx.astype(BF16), vc_ref[...])
        o_ref[:, g * HEAD_DIM:(g + 1) * HEAD_DIM] = (acc / den).astype(o_ref.dtype)


def _attn_window(qkv, sink, rows):
    b, s, c = rows.b, rows.s, rows.c
    groups = B_HEADS // B_KV_HEADS
    qw = groups * HEAD_DIM
    nq = s // Q_BLOCK
    q_col = A_Q + 2 * A_KV
    k_col = q_col + B_Q
    v_col = k_col + B_KV
    ctx_base = rows.n_lat // c

    def kv_spec(n, base, col):
        return pl.BlockSpec((n, HEAD_DIM), lambda bb, h, i: (base + bb, col // HEAD_DIM + h))

    return pl.pallas_call(
        functools.partial(_attn_win_kernel, groups=groups, seq=s),
        grid=(b, B_KV_HEADS, nq),
        in_specs=[pl.BlockSpec(memory_space=pltpu.SMEM),
                  pl.BlockSpec((Q_BLOCK, qw), lambda bb, h, i: (bb * nq + i, q_col // qw + h)),
                  kv_spec(s, 0, k_col), kv_spec(s, 0, v_col),
                  kv_spec(c, ctx_base, k_col), kv_spec(c, ctx_base, v_col)],
        out_specs=pl.BlockSpec((Q_BLOCK, qw), lambda bb, h, i: (bb * nq + i, h)),
        out_shape=jax.ShapeDtypeStruct((b * s, B_Q), BF16),
        compiler_params=_cparams(("parallel", "parallel", "arbitrary")),
        name="attn_window",
    )(sink, qkv, qkv, qkv, qkv, qkv)


def _delta_prep_kernel(prev_ref, cur_ref, next_ref, ab_ref, w_ref, alog_ref, dtb_ref, qkv_ref, gb_ref,
                       *, lat_blocks, seq_blocks, ctx_blocks):
    i = pl.program_id(0)
    tm = cur_ref.shape[0]
    is_lat = i < lat_blocks
    pos = jnp.where(is_lat, i % seq_blocks, (i - lat_blocks) % ctx_blocks)
    n_pos = jnp.where(is_lat, seq_blocks, ctx_blocks)
    first = pos == 0
    last = pos == n_pos - 1
    keep_prev = jnp.where(first, 0.0, 1.0)
    keep_next = jnp.where(last, 0.0, 1.0)
    ext = tm + 2 * HALO
    n_qk = 2 * C_QK_HEADS
    for j in range(C_CONV // HEAD_DIM):
        sl = slice(j * HEAD_DIM, (j + 1) * HEAD_DIM)
        xt = jnp.concatenate([prev_ref[:, sl] * keep_prev, cur_ref[:, sl], next_ref[:, sl] * keep_next], axis=0)
        acc = None
        for tap in range(CONV_K):
            shift = (CONV_K // 2 - tap) % ext
            xs = xt if shift == 0 else pltpu.roll(xt, shift, 0)
            term = xs[HALO:HALO + tm] * w_ref[tap:tap + 1, sl]
            acc = term if acc is None else acc + term
        y = _silu(acc)
        if j < n_qk:
            y = y * lax.rsqrt(jnp.sum(y * y, axis=-1, keepdims=True) + RMS_EPS)
            if j < C_QK_HEADS:
                y = y * (HEAD_DIM ** -0.5)
        qkv_ref[:, sl] = y
    ab = ab_ref[...]
    lane = lax.broadcasted_iota(jnp.int32, ab.shape, 1)
    n_g = 2 * C_V_HEADS
    z = ab + dtb_ref[...]
    softplus = jnp.maximum(z, 0.0) + jnp.log(1.0 + jnp.exp(-jnp.abs(z)))
    g = -jnp.exp(alog_ref[...]) * softplus
    gb_ref[...] = jnp.where(lane < n_g, g, jax.nn.sigmoid(ab))


def _delta_prep(rows, proj_main, proj_ab, conv_w, a_log, dt_bias):
    r = proj_main.shape[0]
    tm = rows.tm
    hb = tm // HALO
    last_halo = r // HALO - 1
    pad = AB_W - a_log.size
    alog = jnp.pad(a_log.reshape(1, -1), ((0, 0), (0, pad)))
    dtb = jnp.pad(dt_bias.reshape(1, -1), ((0, 0), (0, pad)))
    kern = functools.partial(_delta_prep_kernel, lat_blocks=rows.lat_blocks, seq_blocks=rows.s // tm,
                             ctx_blocks=rows.c // tm)
    return pl.pallas_call(
        kern,
        grid=(r // tm,),
        in_specs=[pl.BlockSpec((HALO, C_CONV), lambda i: (jnp.maximum(i * hb - 1, 0), 0)),
                  pl.BlockSpec((tm, C_CONV), lambda i: (i, 0)),
                  pl.BlockSpec((HALO, C_CONV), lambda i: (jnp.minimum((i + 1) * hb, last_halo), 0)),
                  pl.BlockSpec((tm, AB_W), lambda i: (i, 0)),
                  pl.BlockSpec((CONV_K, C_CONV), lambda i: (0, 0)),
                  _vec_spec(AB_W), _vec_spec(AB_W)],
        out_specs=[pl.BlockSpec((tm, C_CONV), lambda i: (i, 0)), pl.BlockSpec((tm, AB_W), lambda i: (i, 0))],
        out_shape=[jax.ShapeDtypeStruct((r, C_CONV), F32), jax.ShapeDtypeStruct((r, AB_W), F32)],
        compiler_params=_cparams(("parallel",)),
        name="delta_prep",
    )(proj_main, proj_main, proj_main, proj_ab, conv_w, alog, dtb)


def _delta_kernel(qkv_f_ref, gb_f_ref, qkv_r_ref, gb_r_ref, o_f_ref, o_r_ref, state_ref):
    @pl.when(pl.program_id(1) == 0)
    def _():
        state_ref[...] = jnp.zeros_like(state_ref)

    row = lax.broadcasted_iota(jnp.int32, (CHUNK, CHUNK), 0)
    col = lax.broadcasted_iota(jnp.int32, (CHUNK, CHUNK), 1)
    eye = jnp.where(row == col, 1.0, 0.0)
    rep = C_V_HEADS // C_QK_HEADS
    n_sq = int(math.log2(CHUNK)) - 1
    chains = []
    for direction, (qkv_ref, gb_ref, o_ref) in enumerate(((qkv_f_ref, gb_f_ref, o_f_ref),
                                                          (qkv_r_ref, gb_r_ref, o_r_ref))):
        if direction == 0:
            incl, strict, last = row >= col, row > col, CHUNK - 1
        else:
            incl, strict, last = row <= col, row < col, 0
        gb = gb_ref[...]
        gcum = _dot(jnp.where(incl, 1.0, 0.0), gb, precision=HIGHEST)
        gcum_t = gcum.T
        kq = []
        for qh in range(C_QK_HEADS):
            q = qkv_ref[:, qh * HEAD_DIM:(qh + 1) * HEAD_DIM]
            k = qkv_ref[:, C_QK + qh * HEAD_DIM:C_QK + (qh + 1) * HEAD_DIM]
            prod = _dot_nt(jnp.concatenate([k, q], axis=0).astype(BF16), k.astype(BF16))
            kq.append((q, k, prod[:CHUNK], prod[CHUNK:]))
        for h in range(C_V_HEADS):
            lane = direction * C_V_HEADS + h
            q, k, kk, qk = kq[h // rep]
            g_col = gcum[:, lane:lane + 1]
            beta = gb[:, 2 * C_V_HEADS + lane:2 * C_V_HEADS + lane + 1]
            decay = jnp.where(incl, jnp.exp(jnp.minimum(g_col - gcum_t[lane:lane + 1, :], 0.0)), 0.0)
            v = qkv_ref[:, 2 * C_QK + h * HEAD_DIM:2 * C_QK + (h + 1) * HEAD_DIM]
            p = jnp.where(strict, -(kk * beta) * decay, 0.0)
            chains.append(dict(
                h=h, slot=lane, o_ref=o_ref, q=q, k=k, g_col=g_col, e_g=jnp.exp(g_col),
                g_last=gcum[last:last + 1, lane:lane + 1], beta=beta, v=v, qk=qk * decay, p=p, t=eye + p))
    for _ in range(n_sq):
        for ch in chains:
            pb = ch["p"].astype(BF16)
            ch["p"] = _dot(pb, pb)
        for ch in chains:
            ch["t"] = ch["t"] + _dot(ch["t"].astype(BF16), ch["p"].astype(BF16))
    for ch in chains:
        rhs = jnp.concatenate([ch["v"] * ch["beta"], ch["k"] * (ch["beta"] * ch["e_g"])], axis=1)
        uw = _dot(ch["t"].astype(BF16), rhs.astype(BF16))
        ch["u"], ch["w"] = uw[:, :HEAD_DIM], uw[:, HEAD_DIM:]
    for ch in chains:
        ch["state"] = state_ref[ch["slot"]]
        lhs = jnp.concatenate([ch["w"], ch["q"] * ch["e_g"]], axis=0)
        ch["ws_qs"] = _dot(lhs.astype(BF16), ch["state"].astype(BF16))
    for ch in chains:
        ch["v_new"] = (ch["u"] - ch["ws_qs"][:CHUNK]).astype(BF16)
    for ch in chains:
        h = ch["h"]
        ch["o_ref"][:, h * HEAD_DIM:(h + 1) * HEAD_DIM] = ch["ws_qs"][CHUNK:] + _dot(ch["qk"].astype(BF16), ch["v_new"])
    for ch in chains:
        k_dec = ch["k"] * jnp.exp(ch["g_last"] - ch["g_col"])
        state_ref[ch["slot"]] = ch["state"] * jnp.exp(ch["g_last"]) + _dot_tn(k_dec.astype(BF16), ch["v_new"])


def _delta_scan(rows, dqkv, gb):
    b, s, c = rows.b, rows.s, rows.c
    ctx_chunks, lat_chunks = c // CHUNK, s // CHUNK
    ctx_base = rows.n_lat // CHUNK

    def fwd_map(bb, t):
        return (jnp.where(t < ctx_chunks, ctx_base + bb * ctx_chunks + t, bb * lat_chunks + t - ctx_chunks), 0)

    def rev_map(bb, t):
        ctx_i, lat_i = ctx_chunks - 1 - t, lat_chunks - 1 - (t - ctx_chunks)
        return (jnp.where(t < ctx_chunks, ctx_base + bb * ctx_chunks + ctx_i, bb * lat_chunks + lat_i), 0)

    out = jax.ShapeDtypeStruct((rows.r, C_V), F32)
    return pl.pallas_call(
        _delta_kernel,
        grid=(b, ctx_chunks + lat_chunks),
        in_specs=[pl.BlockSpec((CHUNK, C_CONV), fwd_map), pl.BlockSpec((CHUNK, AB_W), fwd_map),
                  pl.BlockSpec((CHUNK, C_CONV), rev_map), pl.BlockSpec((CHUNK, AB_W), rev_map)],
        out_specs=[pl.BlockSpec((CHUNK, C_V), fwd_map), pl.BlockSpec((CHUNK, C_V), rev_map)],
        out_shape=[out, out],
        scratch_shapes=[pltpu.VMEM((2 * C_V_HEADS, HEAD_DIM, HEAD_DIM), F32)],
        compiler_params=_cparams(("parallel", "arbitrary")),
        name="delta_scan",
    )(dqkv, gb, dqkv, gb)


def _merge_kernel(oa_ref, ob_ref, of_ref, or_ref, z_ref, nc_ref, ga_ref, gb_ref, gc_ref,
                  wa_ref, wb_ref, wc_ref, y_ref, oc_ref):
    for h in range(C_V_HEADS):
        sl = slice(h * HEAD_DIM, (h + 1) * HEAD_DIM)
        o = of_ref[:, sl] + or_ref[:, sl]
        oc_ref[:, sl] = (_rms(o, nc_ref[...]) * _silu(z_ref[:, sl])).astype(BF16)
    y = jax.nn.sigmoid(ga_ref[...].astype(F32)) * _dot(oa_ref[...], wa_ref[...])
    y = y + jax.nn.sigmoid(gb_ref[...].astype(F32)) * _dot(ob_ref[...], wb_ref[...])
    y = y + jax.nn.sigmoid(gc_ref[...].astype(F32)) * _dot(oc_ref[...], wc_ref[...])
    y_ref[...] = y.astype(BF16)


def _merge(n_rows, tm, o_a, o_b, o_f, o_r, proj_main, gates, norm_c, w_a, w_b, w_c):
    d = w_a.shape[1]
    z_blk = C_CONV // C_V

    def rspec(w, col=0):
        return pl.BlockSpec((tm, w), lambda i: (i, col))

    def wspec(kdim):
        return pl.BlockSpec((kdim, d), lambda i: (0, 0))

    return pl.pallas_call(
        _merge_kernel,
        grid=(n_rows // tm,),
        in_specs=[rspec(A_Q), rspec(B_Q), rspec(C_V), rspec(C_V), rspec(C_V, z_blk), _vec_spec(HEAD_DIM),
                  rspec(d, 0), rspec(d, 1), rspec(d, 2), wspec(A_Q), wspec(B_Q), wspec(C_V)],
        out_specs=rspec(d),
        out_shape=jax.ShapeDtypeStruct((n_rows, d), BF16),
        scratch_shapes=[pltpu.VMEM((tm, C_V), BF16)],
        compiler_params=_cparams(("parallel",)),
        name="branch_merge",
    )(o_a, o_b, o_f, o_r, proj_main, norm_c, gates, gates, gates, w_a, w_b, w_c)


def _outproj_kernel(y_ref, w_ref, x_ref, gt_ref, g_ref, b_ref, sc_ref, sh_ref, x1_ref, h_ref, *, alpha):
    y = _dot(y_ref[...], w_ref[...])
    x1 = _layernorm(alpha * x_ref[...] + gt_ref[0, 0] * y) * g_ref[...] + b_ref[...]
    x1_ref[...] = x1
    h_ref[...] = (x1 * (1.0 + sc_ref[0, 0]) + sh_ref[0, 0]).astype(BF16)


def _outproj(rows, n_rows, y, w_out, x, mod_l, ln_g, ln_b, alpha):
    d = x.shape[1]
    tm = rows.tm
    rspec = pl.BlockSpec((tm, d), lambda i: (i, 0))
    return pl.pallas_call(
        functools.partial(_outproj_kernel, alpha=alpha),
        grid=(n_rows // tm,),
        in_specs=[rspec, pl.BlockSpec((d, d), lambda i: (0, 0)), rspec, _mod_spec(rows, d, 2),
                  _vec_spec(d), _vec_spec(d), _mod_spec(rows, d, 4), _mod_spec(rows, d, 3)],
        out_specs=[rspec, rspec],
        out_shape=[jax.ShapeDtypeStruct((n_rows, d), F32), jax.ShapeDtypeStruct((n_rows, d), BF16)],
        compiler_params=_cparams(("parallel",)),
        name="out_proj_norm",
    )(y, w_out, x, mod_l, ln_g, ln_b, mod_l, mod_l)


def _router_kernel(x_ref, sc_ref, sh_ref, wr_ref, rb_ref, gate_ref, pos_ref, tpos_ref, cnt_ref, before_ref,
                   lstart_ref, tstart_ref, total_ref, carry_ref):
    @pl.when(pl.program_id(0) == 0)
    def _():
        carry_ref[...] = jnp.zeros_like(carry_ref)

    h = x_ref[...] * (1.0 + sc_ref[0, 0]) + sh_ref[0, 0]
    scores = jax.nn.sigmoid(_dot(h, wr_ref[...], precision=HIGHEST))
    tm, n_exp = scores.shape
    sel = scores + rb_ref[...]
    lane = lax.broadcasted_iota(jnp.int32, (tm, n_exp), 1)
    slot = lax.broadcasted_iota(jnp.int32, (tm, TOP_K), 1)
    gate = jnp.zeros((tm, TOP_K), F32)
    mask = jnp.zeros((tm, n_exp), F32)
    hits = []
    for k in range(TOP_K):
        best = jnp.max(sel, axis=-1, keepdims=True)
        choice = jnp.min(jnp.where(sel == best, lane, n_exp), axis=-1, keepdims=True)
        hit = lane == choice
        hits.append(hit)
        gate = jnp.where(slot == k, jnp.sum(jnp.where(hit, scores, 0.0), axis=-1, keepdims=True), gate)
        sel = jnp.where(hit, -jnp.inf, sel)
        mask = jnp.where(hit, 1.0, mask)
    gate_ref[...] = gate / jnp.sum(gate, axis=-1, keepdims=True) * ROUTED_SCALE
    r_i = lax.broadcasted_iota(jnp.int32, (tm, tm), 0)
    c_i = lax.broadcasted_iota(jnp.int32, (tm, tm), 1)
    earlier = jnp.where(c_i < r_i, 1.0, 0.0).astype(BF16)
    within = _dot(earlier, mask.astype(BF16))
    cnt = jnp.sum(mask, axis=0, keepdims=True)
    run_rows = jnp.floor((cnt + (RUN - 1)) * (1.0 / RUN)) * RUN
    e_r = lax.broadcasted_iota(jnp.int32, (n_exp, n_exp), 0)
    e_c = lax.broadcasted_iota(jnp.int32, (n_exp, n_exp), 1)
    lengths = jnp.concatenate([run_rows, cnt, jnp.zeros((6, n_exp), F32)], axis=0)
    starts = _dot(lengths, jnp.where(e_r < e_c, 1.0, 0.0), precision=HIGHEST)
    lstart, tstart = starts[0:1], starts[1:2]
    pos = jnp.zeros((tm, TOP_K), F32)
    tpos = jnp.zeros((tm, TOP_K), F32)
    for k in range(TOP_K):
        pos = jnp.where(slot == k, jnp.sum(jnp.where(hits[k], within + lstart, 0.0), axis=-1, keepdims=True), pos)
        tpos = jnp.where(slot == k, jnp.sum(jnp.where(hits[k], within + tstart, 0.0), axis=-1, keepdims=True), tpos)
    pos_ref[...] = pos.astype(jnp.int32)
    tpos_ref[...] = tpos.astype(jnp.int32)
    cnt_ref[0] = cnt.astype(jnp.int32)
    before_ref[0] = carry_ref[...].astype(jnp.int32)
    lstart_ref[0] = lstart.astype(jnp.int32)
    tstart_ref[0] = tstart.astype(jnp.int32)
    carry_ref[...] = carry_ref[...] + cnt
    total_ref[...] = carry_ref[...]


def _router(rows, n_rows, x1, mod_l, w_router, router_bias):
    d, n_exp = w_router.shape
    tm = rows.tm
    n_blk = n_rows // tm
    kspec = pl.BlockSpec((tm, TOP_K), lambda i: (i, 0))
    espec = pl.BlockSpec((1, 1, n_exp), lambda i: (i, 0, 0))
    per_block = jax.ShapeDtypeStruct((n_blk, 1, n_exp), jnp.int32)
    return pl.pallas_call(
        _router_kernel,
        grid=(n_blk,),
        in_specs=[pl.BlockSpec((tm, d), lambda i: (i, 0)), _mod_spec(rows, d, 4), _mod_spec(rows, d, 3),
                  pl.BlockSpec((d, n_exp), lambda i: (0, 0)), _vec_spec(n_exp)],
        out_specs=[kspec, kspec, kspec, espec, espec, espec, espec, _vec_spec(n_exp)],
        out_shape=[jax.ShapeDtypeStruct((n_rows, TOP_K), F32), jax.ShapeDtypeStruct((n_rows, TOP_K), jnp.int32),
                   jax.ShapeDtypeStruct((n_rows, TOP_K), jnp.int32),
                   per_block, per_block, per_block, per_block, jax.ShapeDtypeStruct((1, n_exp), F32)],
        scratch_shapes=[pltpu.VMEM((1, n_exp), F32)],
        compiler_params=_cparams(("arbitrary",)),
        name="router",
    )(x1, mod_l, mod_l, w_router, router_bias.reshape(1, n_exp))


def _local_rows(tm, n_exp):
    return tm * TOP_K + n_exp * RUN


def _token_rows(tokens, s, per_token):
    return pl.ds(s, tokens, stride=per_token) if per_token > 1 else pl.ds(s, tokens)


def _pack_words(lo, hi):
    return (lax.bitcast_convert_type(lo, jnp.uint32) >> 16) | (lax.bitcast_convert_type(hi, jnp.uint32)
                                                               & jnp.uint32(0xFFFF0000))


def _unpack_words(p):
    return (lax.bitcast_convert_type(p << 16, F32).astype(BF16),
            lax.bitcast_convert_type(p & jnp.uint32(0xFFFF0000), F32).astype(BF16))


def _run_dmas(n_exp, cnt_ref, src_start, dst_start, src_ref, dst_ref, sem, per_token):
    total = jnp.int32(0)
    rows = RUN * per_token
    for e in range(n_exp):
        n_run = (cnt_ref[0, 0, e] + (RUN - 1)) // RUN
        s0, d0 = src_start(e), dst_start(e)

        def body(c, carry, s0=s0, d0=d0):
            pltpu.make_async_copy(src_ref.at[pl.ds(pl.multiple_of((s0 + c * RUN) * per_token, per_token), rows)],
                                  dst_ref.at[pl.ds(pl.multiple_of((d0 + c * RUN) * per_token, per_token), rows)],
                                  sem).start()
            return carry

        lax.fori_loop(0, n_run, body, 0)
        total = total + n_run
    return total


def _wait_runs(n, src_ref, dst_ref, sem, per_token):
    rows = RUN * per_token

    def body(c, carry):
        pltpu.make_async_copy(src_ref.at[pl.ds(0, rows)], dst_ref.at[pl.ds(0, rows)], sem).wait()
        return carry

    lax.fori_loop(0, n, body, 0)


def _dispatch_kernel(gs_ref, count_ref, padded_ref, post_ref, cnt_ref, carry_ref, ls_ref, h_ref, xs_ref,
                     loc, zbuf, sem, pending_ref, *, per_token, n_exp):
    i = pl.program_id(0)
    tm, d = h_ref.shape
    half = d // 2
    n_local = loc.shape[1] // per_token
    chunk = n_local // DISPATCH_CHUNKS
    buf = i % 2
    for r0 in range(0, n_local, chunk):
        p_iota = r0 + lax.broadcasted_iota(jnp.int32, (chunk, tm), 0)
        onehot = jnp.zeros((chunk, tm), F32)
        for k in range(TOP_K):
            onehot = jnp.where(p_iota == post_ref[k:k + 1, :], 1.0, onehot)
        onehot = onehot.astype(BF16)
        words = _pack_words(_dot(onehot, h_ref[:, :half]), _dot(onehot, h_ref[:, half:]))
        for s in range(per_token):
            loc[buf, _token_rows(chunk, r0 * per_token + s, per_token), :] = words[:, s * LANES:(s + 1) * LANES]

    @pl.when(i > 0)
    def _():
        _wait_runs(pending_ref[0], loc.at[0], xs_ref, sem, per_token)

    pending_ref[0] = _run_dmas(n_exp, cnt_ref, lambda e: ls_ref[0, 0, e], lambda e: gs_ref[e] + carry_ref[0, 0, e],
                               loc.at[buf], xs_ref, sem, per_token)

    @pl.when(i == pl.num_programs(0) - 1)
    def _():
        _wait_runs(pending_ref[0], loc.at[0], xs_ref, sem, per_token)
        zbuf[...] = jnp.zeros_like(zbuf)
        rows = RUN * per_token
        n_zero = jnp.int32(0)
        for e in range(n_exp):
            first = gs_ref[e] + count_ref[e]
            n_piece = (padded_ref[e] - count_ref[e]) // RUN

            def body(c, carry, first=first):
                pltpu.make_async_copy(zbuf, xs_ref.at[pl.ds(pl.multiple_of((first + c * RUN) * per_token, per_token),
                                                            rows)], sem).start()
                return carry

            lax.fori_loop(0, n_piece, body, 0)
            n_zero = n_zero + n_piece
        used = gs_ref[n_exp - 1] + padded_ref[n_exp - 1]
        n_tail = (xs_ref.shape[0] // per_token - used) // RUN

        def tail_body(c, carry):
            pltpu.make_async_copy(zbuf, xs_ref.at[pl.ds(pl.multiple_of((used + c * RUN) * per_token, per_token),
                                                        rows)], sem).start()
            return carry

        lax.fori_loop(0, n_tail, tail_body, 0)
        _wait_runs(n_zero + n_tail, loc.at[0], xs_ref, sem, per_token)
        for e in range(n_exp):
            end = gs_ref[e] + padded_ref[e]
            pltpu.make_async_copy(zbuf, xs_ref.at[pl.ds(pl.multiple_of((end - RUN) * per_token, per_token), rows)],
                                  sem).start()
        _wait_runs(n_exp, loc.at[0], xs_ref, sem, per_token)


def _dispatch(rows, n_rows, h2, pos_t, cnt, carry, lstart, group_start, count, padded, n_slots):
    d = h2.shape[1]
    tm = rows.tm
    n_exp = count.shape[0]
    per_token = d // 2 // LANES
    n_local = tm * TOP_K + DISPATCH_TAIL
    assert DISPATCH_TAIL >= RUN and n_local % (8 * DISPATCH_CHUNKS) == 0
    smem_blk = pl.BlockSpec((1, 1, n_exp), lambda i, *_: (i, 0, 0), memory_space=pltpu.SMEM)
    grid_spec = pltpu.PrefetchScalarGridSpec(
        num_scalar_prefetch=3,
        grid=(n_rows // tm,),
        in_specs=[pl.BlockSpec((TOP_K, tm), lambda i, *_: (0, i)), smem_blk, smem_blk, smem_blk,
                  pl.BlockSpec((tm, d), lambda i, *_: (i, 0))],
        out_specs=pl.BlockSpec(memory_space=pl.ANY),
        scratch_shapes=[pltpu.VMEM((2, n_local * per_token, LANES), jnp.uint32),
                        pltpu.VMEM((RUN * per_token, LANES), jnp.uint32),
                        pltpu.SemaphoreType.DMA(()), pltpu.SMEM((1,), jnp.int32)],
    )
    return pl.pallas_call(
        functools.partial(_dispatch_kernel, per_token=per_token, n_exp=n_exp),
        grid_spec=grid_spec,
        out_shape=jax.ShapeDtypeStruct((n_slots * per_token, LANES), jnp.uint32),
        compiler_params=_cparams(("arbitrary",)),
        name="moe_dispatch",
    )(group_start, count, padded, pos_t, cnt, carry, lstart, h2)


def _expert_block_kernel(be_ref, nu_ref, x_ref, w1_ref, w3_ref, w2_ref, o_ref, xs_ref, w1b, w3b, w2b, prev_ref):
    blk = pl.program_id(0)
    d = xs_ref.shape[1]

    @pl.when(blk == 0)
    def _():
        prev_ref[0] = -1

    @pl.when(blk < nu_ref[0])
    def _():
        e = be_ref[blk]

        @pl.when(e != prev_ref[0])
        def _():
            w1b[...] = w1_ref[0, 0].astype(BF16)
            w3b[...] = w3_ref[0, 0].astype(BF16)
            w2b[...] = w2_ref[0, 0].astype(BF16)
            prev_ref[0] = e

        half = d // 2
        per_token = half // LANES
        for s in range(per_token):
            lo, hi = _unpack_words(x_ref[_token_rows(MOE_ROWS, s, per_token), :])
            xs_ref[:, s * LANES:(s + 1) * LANES] = lo
            xs_ref[:, half + s * LANES:half + (s + 1) * LANES] = hi
        hidden = _silu(_dot(xs_ref[...], w1b[...])) * _dot(xs_ref[...], w3b[...])
        y = _dot(hidden.astype(BF16), w2b[...])
        for s in range(per_token):
            o_ref[_token_rows(MOE_ROWS, s, per_token), :] = _pack_words(
                y[:, s * LANES:(s + 1) * LANES].astype(BF16).astype(F32),
                y[:, half + s * LANES:half + (s + 1) * LANES].astype(BF16).astype(F32))

    @pl.when(blk >= nu_ref[0])
    def _():
        o_ref[...] = jnp.zeros_like(o_ref)


def _expert_blocks(xs, block_expert, n_used, w1, w3, w2, layer):
    d, ff = w1.shape[-2:]
    per_token = d // 2 // LANES
    n_blocks = block_expert.shape[0]
    blk_spec = pl.BlockSpec((MOE_ROWS * per_token, LANES), lambda i, be, nu: (i, 0))
    in_spec = pl.BlockSpec((MOE_ROWS * per_token, LANES), lambda i, be, nu: (jnp.minimum(i, nu[0] - 1), 0))

    def wmap(i, be, nu):
        return (layer, be[i], 0, 0)

    grid_spec = pltpu.PrefetchScalarGridSpec(
        num_scalar_prefetch=2,
        grid=(n_blocks,),
        in_specs=[in_spec, pl.BlockSpec((1, 1, d, ff), wmap), pl.BlockSpec((1, 1, d, ff), wmap),
                  pl.BlockSpec((1, 1, ff, d), wmap)],
        out_specs=blk_spec,
        scratch_shapes=[pltpu.VMEM((MOE_ROWS, d), BF16),
                        pltpu.VMEM((d, ff), BF16), pltpu.VMEM((d, ff), BF16), pltpu.VMEM((ff, d), BF16),
                        pltpu.SMEM((1,), jnp.int32)],
    )
    return pl.pallas_call(
        _expert_block_kernel,
        grid_spec=grid_spec,
        out_shape=jax.ShapeDtypeStruct(xs.shape, jnp.uint32),
        compiler_params=_cparams(("arbitrary",)),
        name="experts",
    )(block_expert, n_used, xs, w1, w3, w2)


def _gather_combine_kernel(*refs, alpha, emit_h, per_token, n_exp):
    gs_ref, pos_ref, gate_ref, cnt_ref, carry_ref, ls_ref, ys_ref, h_ref, ws1_ref, ws3_ref, ws2_ref = refs[:11]
    x_ref, gt_ref, g_ref, b_ref = refs[11:15]
    n_out = 2 if emit_h else 1
    out_refs = refs[-(n_out + 3):-3]
    loc, wts_ref, sem = refs[-3:]
    tm, d = h_ref.shape
    n_local = loc.shape[0] // per_token

    @pl.when(pl.program_id(0) == 0)
    def _():
        loc[...] = jnp.zeros_like(loc)

    n_dma = _run_dmas(n_exp, cnt_ref, lambda e: gs_ref[e] + carry_ref[0, 0, e], lambda e: ls_ref[0, 0, e],
                      ys_ref, loc, sem, per_token)
    hidden = _silu(_dot(h_ref[...], ws1_ref[...])) * _dot(h_ref[...], ws3_ref[...])
    f = _dot(hidden.astype(BF16), ws2_ref[...])
    p_iota = lax.broadcasted_iota(jnp.int32, (tm, n_local), 1)
    weights = jnp.zeros((tm, n_local), F32)
    for k in range(TOP_K):
        weights = jnp.where(p_iota == pos_ref[:, k:k + 1], gate_ref[:, k:k + 1], weights)
    wts_ref[...] = weights.astype(BF16)
    _wait_runs(n_dma, ys_ref, loc, sem, per_token)
    group = 2 if per_token % 2 == 0 else 1
    lo_cols, hi_cols = [], []
    for s0 in range(0, per_token, group):
        planes = [_unpack_words(loc[_token_rows(n_local, s, per_token), :]) for s in range(s0, s0 + group)]
        lo_cols.append(_dot(wts_ref[...], jnp.concatenate([lo for lo, _ in planes], axis=1)))
        hi_cols.append(_dot(wts_ref[...], jnp.concatenate([hi for _, hi in planes], axis=1)))
    f = f + jnp.concatenate(lo_cols + hi_cols, axis=1)
    x2 = _layernorm(alpha * x_ref[...] + gt_ref[0, 0] * f) * g_ref[...] + b_ref[...]
    out_refs[0][...] = x2
    if emit_h:
        sc_ref, sh_ref = refs[15:17]
        out_refs[1][...] = (x2 * (1.0 + sc_ref[0, 0]) + sh_ref[0, 0]).astype(BF16)


def _gather_combine(rows, n_rows, ys, pos, gate, cnt, carry, lstart, group_start, h2, ws1, ws3, ws2, x1, mod_l,
                    ln_g, ln_b, alpha, mod_next):
    d = x1.shape[1]
    sf = ws1.shape[1]
    tm = rows.tm
    n_exp = group_start.shape[0]
    per_token = d // 2 // LANES
    n_local = _local_rows(tm, n_exp)

    def rmap(i, *_):
        return (i, 0)

    def const(i, *_):
        return (0, 0)

    def mod_spec(which):
        return pl.BlockSpec((1, 1, 1, d), lambda i, *_: (rows.group(i), which, 0, 0))

    rspec = pl.BlockSpec((tm, d), rmap)
    kspec = pl.BlockSpec((tm, TOP_K), rmap)
    smem_blk = pl.BlockSpec((1, 1, n_exp), lambda i, *_: (i, 0, 0), memory_space=pltpu.SMEM)
    in_specs = [kspec, kspec, smem_blk, smem_blk, smem_blk, pl.BlockSpec(memory_space=pl.ANY), rspec,
                pl.BlockSpec((d, sf), const), pl.BlockSpec((d, sf), const), pl.BlockSpec((sf, d), const),
                rspec, mod_spec(5), pl.BlockSpec((1, d), const), pl.BlockSpec((1, d), const)]
    args = [pos, gate, cnt, carry, lstart, ys, h2, ws1, ws3, ws2, x1, mod_l, ln_g, ln_b]
    emit_h = mod_next is not None
    if emit_h:
        in_specs += [mod_spec(1), mod_spec(0)]
        args += [mod_next, mod_next]
        out_specs = [rspec, rspec]
        out_shape = [jax.ShapeDtypeStruct((n_rows, d), F32), jax.ShapeDtypeStruct((n_rows, d), BF16)]
    else:
        out_specs = rspec
        out_shape = jax.ShapeDtypeStruct((n_rows, d), F32)
    grid_spec = pltpu.PrefetchScalarGridSpec(
        num_scalar_prefetch=1,
        grid=(n_rows // tm,),
        in_specs=in_specs,
        out_specs=out_specs,
        scratch_shapes=[pltpu.VMEM((n_local * per_token, LANES), jnp.uint32), pltpu.VMEM((tm, n_local), BF16),
                        pltpu.SemaphoreType.DMA(())],
    )
    return pl.pallas_call(
        functools.partial(_gather_combine_kernel, alpha=alpha, emit_h=emit_h, per_token=per_token, n_exp=n_exp),
        grid_spec=grid_spec,
        out_shape=out_shape,
        compiler_params=_cparams(("arbitrary",)),
        name="moe_combine_norm",
    )(group_start, *args)


def _permute_w_in(w):
    a_end = A_Q + 2 * A_KV
    b_end = a_end + B_Q + 2 * B_KV
    c_end = b_end + C_CONV + C_V
    ab_end = c_end + 4 * C_V_HEADS
    main = jnp.concatenate([w[:, b_end:c_end], w[:, :b_end]], axis=1).astype(BF16)
    ab = jnp.pad(w[:, c_end:ab_end], ((0, 0), (0, AB_W - 4 * C_V_HEADS))).astype(BF16)
    gates = w[:, ab_end:].astype(BF16)
    return main, gates, ab


def kernel(x, c, ctx, c_ctx, w_mod, b_mod, w_in, q_norm_a, k_norm_a, sink_b, conv_c, a_log_c, dt_bias_c, norm_c,
           w_br_a, w_br_b, w_br_c, w_out, ln1_g, ln1_b, w_router, router_bias, w1, w3, w2, ws1, ws3, ws2,
           ln2_g, ln2_b):
    b, s, d = x.shape
    n_ctx = ctx.shape[1]
    depth = w_mod.shape[0]
    n_exp = w_router.shape[-1]
    alpha = (2.0 * depth) ** 0.25
    rows = _Rows(b, s, n_ctx)
    tm = rows.tm

    c_all = jnp.concatenate([c, c_ctx[None, :], jnp.zeros((8 - b - 1, d), F32)], axis=0)
    mod = _modulation(c_all, w_mod, b_mod).reshape(depth, 8, N_MOD, 1, d)
    tables = _rope_tables(s, tm)

    x_cur, h = _entry(rows, x.reshape(b * s, d), ctx.reshape(b * n_ctx, d), mod[0])

    for l in range(depth):
        need_ctx = l < depth - 1
        n_rows = rows.r if need_ctx else rows.n_lat
        mod_l = mod[l]
        w_main, w_gates, w_ab = _permute_w_in(w_in[l])
        proj_main = _matmul(h, w_main, F32)
        gates = _matmul(h, w_gates, BF16)
        proj_ab = _matmul(h, w_ab, F32)

        qkv = _attn_prep(rows, proj_main, tables, q_norm_a[l].reshape(1, -1), k_norm_a[l].reshape(1, -1))
        a_k, a_v = A_Q, A_Q + A_KV
        o_a = _attn_full(qkv, batch=b, n_q=s, q_base=0, sources=[(rows.n_lat, n_ctx), (0, s)],
                         q_col=0, k_col=a_k, v_col=a_v, heads=A_HEADS, kv_heads=A_KV_HEADS)
        o_b = _attn_window(qkv, sink_b[l], rows)
        if need_ctx:
            b_q = A_Q + 2 * A_KV
            o_a_c = _attn_full(qkv, batch=b, n_q=n_ctx, q_base=rows.n_lat, sources=[(rows.n_lat, n_ctx)],
                               q_col=0, k_col=a_k, v_col=a_v, heads=A_HEADS, kv_heads=A_KV_HEADS)
            o_b_c = _attn_full(qkv, batch=b, n_q=n_ctx, q_base=rows.n_lat, sources=[(rows.n_lat, n_ctx)],
                               q_col=b_q, k_col=b_q + B_Q, v_col=b_q + B_Q + B_KV, heads=B_HEADS,
                               kv_heads=B_KV_HEADS, sink=sink_b[l])
            o_a = jnp.concatenate([o_a, o_a_c], axis=0)
            o_b = jnp.concatenate([o_b, o_b_c], axis=0)

        dqkv, gbeta = _delta_prep(rows, proj_main, proj_ab, conv_c[l], a_log_c[l], dt_bias_c[l])
        o_f, o_r = _delta_scan(rows, dqkv, gbeta)

        y = _merge(n_rows, tm, o_a, o_b, o_f, o_r, proj_main, gates, norm_c[l].reshape(1, -1),
                   w_br_a[l].astype(BF16), w_br_b[l].astype(BF16), w_br_c[l].astype(BF16))
        x1, h2 = _outproj(rows, n_rows, y, w_out[l].astype(BF16), x_cur, mod_l,
                          ln1_g[l].reshape(1, -1), ln1_b[l].reshape(1, -1), alpha)

        gate, pos, tpos, cnt, before, lstart, tstart, counts = _router(rows, n_rows, x1, mod_l, w_router[l],
                                                                       router_bias[l])
        counts = counts[0].astype(jnp.int32)
        padded = (counts + RUN + MOE_ROWS - 1) // MOE_ROWS * MOE_ROWS
        padded_end = jnp.cumsum(padded)
        group_start = padded_end - padded
        n_blocks = -(-(n_rows * TOP_K + n_exp * RUN) // MOE_ROWS) + n_exp
        block_start = jnp.arange(n_blocks, dtype=jnp.int32)[:, None] * MOE_ROWS
        block_expert = jnp.minimum(jnp.sum((padded_end[None, :] <= block_start).astype(jnp.int32), axis=-1), n_exp - 1)
        n_used = (padded_end[-1:] // MOE_ROWS).astype(jnp.int32)
        xs = _dispatch(rows, n_rows, h2, tpos.T, cnt, before, tstart, group_start, counts, padded,
                       n_blocks * MOE_ROWS)
        ys = _expert_blocks(xs, block_expert, n_used, w1, w3, w2, l)

        mod_next = mod[l + 1] if need_ctx else None
        out = _gather_combine(rows, n_rows, ys, pos, gate, cnt, before, lstart, group_start, h2,
                              ws1[l].astype(BF16), ws3[l].astype(BF16), ws2[l].astype(BF16),
                              x1, mod_l, ln2_g[l].reshape(1, -1), ln2_b[l].reshape(1, -1), alpha, mod_next)
        if need_ctx:
            x_cur, h = out
        else:
            x_cur = out
    return x_cur[:rows.n_lat].reshape(b, s, d)
```

```python
import functools
import math

import jax
import jax.numpy as jnp
from jax import lax
from jax.experimental import pallas as pl
from jax.experimental.pallas import tpu as pltpu

F32 = jnp.float32
BF16 = jnp.bfloat16
HIGHEST = lax.Precision.HIGHEST

HEAD_DIM = 128
GRID_W = 64
ROPE_THETA = 10000.0
A_HEADS, A_KV_HEADS = 8, 2
B_HEADS, B_KV_HEADS = 8, 2
WINDOW = 128
Q_BLOCK = 128
C_QK_HEADS, C_V_HEADS = 4, 8
CONV_K = 5
CHUNK = 64
TOP_K = 8
ROUTED_SCALE = 2.5
LN_EPS = 1e-5
RMS_EPS = 1e-6
LOG2_E = 1.4426950408889634
MOE_ROWS = 256
EXPERT_LOOKAHEAD = 2
RUN = 16
DISPATCH_TAIL = 64
DISPATCH_CHUNKS = 4
LANES = 128
N_MOD = 6
HALO = 8
VMEM_LIMIT = 56 * 2**20

A_Q, A_KV = A_HEADS * HEAD_DIM, A_KV_HEADS * HEAD_DIM
B_Q, B_KV = B_HEADS * HEAD_DIM, B_KV_HEADS * HEAD_DIM
C_QK, C_V = C_QK_HEADS * HEAD_DIM, C_V_HEADS * HEAD_DIM
C_CONV = 2 * C_QK + C_V
ATT_W = A_Q + 2 * A_KV + B_Q + 2 * B_KV
MAIN_W = C_CONV + C_V + ATT_W
AB_W = 128


def _cparams(sem):
    return pltpu.CompilerParams(dimension_semantics=sem, vmem_limit_bytes=VMEM_LIMIT)


def _pick(n, prefs):
    for p in prefs:
        if n % p == 0:
            return p
    return n


def _dot(a, b, **kw):
    return jnp.dot(a, b, preferred_element_type=F32, **kw)


def _dot_nt(a, b):
    return lax.dot_general(a, b, (((1,), (1,)), ((), ())), preferred_element_type=F32)


def _dot_tn(a, b):
    return lax.dot_general(a, b, (((0,), (0,)), ((), ())), preferred_element_type=F32)


def _silu(x):
    return x * jax.nn.sigmoid(x)


def _mod_kernel(c_ref, w_ref, b_ref, o_ref):
    o_ref[0] = _dot(_silu(c_ref[...]), w_ref[0], precision=HIGHEST) + b_ref[0]


def _modulation(c_all, w_mod, b_mod):
    depth, d, width = w_mod.shape
    tn = _pick(width, (1024, 512, 256, 128))
    return pl.pallas_call(
        _mod_kernel,
        grid=(depth, width // tn),
        in_specs=[pl.BlockSpec((8, d), lambda l, j: (0, 0)),
                  pl.BlockSpec((1, d, tn), lambda l, j: (l, 0, j)),
                  pl.BlockSpec((1, 1, tn), lambda l, j: (l, 0, j))],
        out_specs=pl.BlockSpec((1, 8, tn), lambda l, j: (l, 0, j)),
        out_shape=jax.ShapeDtypeStruct((depth, 8, width), F32),
        compiler_params=_cparams(("parallel", "parallel")),
        name="modulation",
    )(c_all, w_mod, b_mod.reshape(depth, 1, width))


class _Rows:
    def __init__(self, b, s, c):
        self.b, self.s, self.c = b, s, c
        self.tm = 256 if c % 256 == 0 else 128
        self.n_lat = b * s
        self.r = b * s + b * c
        self.lat_blocks = self.n_lat // self.tm
        self.blocks = self.r // self.tm

    def group(self, i):
        return jnp.where(i < self.lat_blocks, i // (self.s // self.tm), self.b)


def _mod_spec(rows, d, which):
    return pl.BlockSpec((1, 1, 1, d), lambda i: (rows.group(i), which, 0, 0))


def _vec_spec(d):
    return pl.BlockSpec((1, d), lambda i: (0, 0))


def _layernorm(x):
    mu = jnp.mean(x, axis=-1, keepdims=True)
    xc = x - mu
    var = jnp.mean(xc * xc, axis=-1, keepdims=True)
    return xc * lax.rsqrt(var + LN_EPS)


def _entry_kernel(lat_ref, ctx_ref, sc_ref, sh_ref, x0_ref, h_ref, *, lat_blocks):
    def emit(src_ref):
        xn = _layernorm(src_ref[...])
        x0_ref[...] = xn
        h_ref[...] = (xn * (1.0 + sc_ref[0, 0]) + sh_ref[0, 0]).astype(BF16)

    is_lat = pl.program_id(0) < lat_blocks
    pl.when(is_lat)(functools.partial(emit, lat_ref))
    pl.when(jnp.logical_not(is_lat))(functools.partial(emit, ctx_ref))


def _entry(rows, x_lat, x_ctx, mod_l):
    d = x_lat.shape[1]
    tm = rows.tm
    lat_blocks = rows.lat_blocks
    row_spec = pl.BlockSpec((tm, d), lambda i: (i, 0))
    return pl.pallas_call(
        functools.partial(_entry_kernel, lat_blocks=lat_blocks),
        grid=(rows.r // tm,),
        in_specs=[pl.BlockSpec((tm, d), lambda i: (jnp.minimum(i, lat_blocks - 1), 0)),
                  pl.BlockSpec((tm, d), lambda i: (jnp.maximum(i - lat_blocks, 0), 0)),
                  _mod_spec(rows, d, 1), _mod_spec(rows, d, 0)],
        out_specs=[row_spec, row_spec],
        out_shape=[jax.ShapeDtypeStruct((rows.r, d), F32), jax.ShapeDtypeStruct((rows.r, d), BF16)],
        compiler_params=_cparams(("parallel",)),
        name="entry_norm",
    )(x_lat, x_ctx, mod_l, mod_l)


def _mm_kernel(a_ref, w_ref, o_ref):
    o_ref[...] = _dot(a_ref[...], w_ref[...]).astype(o_ref.dtype)


def _matmul(a, w, out_dtype):
    m, k = a.shape
    n = w.shape[1]
    tm = _pick(m, (1024, 512, 256, 128))
    tn = _pick(n, (1024, 512, 256, 128))
    return pl.pallas_call(
        _mm_kernel,
        grid=(n // tn, m // tm),
        in_specs=[pl.BlockSpec((tm, k), lambda j, i: (i, 0)),
                  pl.BlockSpec((k, tn), lambda j, i: (0, j))],
        out_specs=pl.BlockSpec((tm, tn), lambda j, i: (i, j)),
        out_shape=jax.ShapeDtypeStruct((m, n), out_dtype),
        compiler_params=_cparams(("parallel", "parallel")),
        name="in_proj",
    )(a, w)


def _rope(t, cos, sin_lo, sin_hi):
    return t * cos + pltpu.roll(t, HEAD_DIM - 32, 1) * sin_lo + pltpu.roll(t, 32, 1) * sin_hi


def _rms(t, g):
    return t * lax.rsqrt(jnp.mean(t * t, axis=-1, keepdims=True) + RMS_EPS) * g


def _attn_prep_kernel(x_ref, cos_ref, slo_ref, shi_ref, qn_ref, kn_ref, o_ref):
    cos, slo, shi = cos_ref[...], slo_ref[...], shi_ref[...]
    scale = HEAD_DIM ** -0.5 * LOG2_E

    def head(j):
        return x_ref[:, j * HEAD_DIM:(j + 1) * HEAD_DIM]

    def put(j, v):
        o_ref[:, j * HEAD_DIM:(j + 1) * HEAD_DIM] = v.astype(BF16)

    j = 0
    for _ in range(A_HEADS):
        put(j, _rope(_rms(head(j), qn_ref[...]), cos, slo, shi) * scale)
        j += 1
    for _ in range(A_KV_HEADS):
        put(j, _rope(_rms(head(j), kn_ref[...]), cos, slo, shi))
        j += 1
    for _ in range(A_KV_HEADS):
        put(j, head(j))
        j += 1
    for _ in range(B_HEADS):
        put(j, _rope(head(j), cos, slo, shi) * scale)
        j += 1
    for _ in range(B_KV_HEADS):
        put(j, _rope(head(j), cos, slo, shi))
        j += 1
    for _ in range(B_KV_HEADS):
        put(j, head(j))
        j += 1


def _attn_prep(rows, proj_main, tables, q_norm, k_norm):
    r = proj_main.shape[0]
    tm = rows.tm
    pos_blocks = rows.s // tm
    col_blk = (C_CONV + C_V) // ATT_W
    assert col_blk * ATT_W == C_CONV + C_V

    def tab_map(i):
        return (jnp.where(i < rows.lat_blocks, i % pos_blocks, pos_blocks), 0)

    tab_spec = pl.BlockSpec((tm, HEAD_DIM), tab_map)
    return pl.pallas_call(
        _attn_prep_kernel,
        grid=(r // tm,),
        in_specs=[pl.BlockSpec((tm, ATT_W), lambda i: (i, col_blk)), tab_spec, tab_spec, tab_spec,
                  _vec_spec(HEAD_DIM), _vec_spec(HEAD_DIM)],
        out_specs=pl.BlockSpec((tm, ATT_W), lambda i: (i, 0)),
        out_shape=jax.ShapeDtypeStruct((r, ATT_W), BF16),
        compiler_params=_cparams(("parallel",)),
        name="attn_prep",
    )(proj_main, *tables, q_norm, k_norm)


def _rope_tables(s, tm):
    row = jnp.repeat(jnp.arange(s // GRID_W, dtype=F32), GRID_W)
    col = jnp.tile(jnp.arange(GRID_W, dtype=F32), s // GRID_W)
    axis_dim = HEAD_DIM // 2
    inv_freq = ROPE_THETA ** (-jnp.arange(0, axis_dim, 2, dtype=F32) / axis_dim)
    ang_r = row[:, None] * inv_freq[None, :]
    ang_c = col[:, None] * inv_freq[None, :]
    ang = jnp.concatenate([ang_r, ang_r, ang_c, ang_c], axis=-1)
    cos, sin = jnp.cos(ang), jnp.sin(ang)
    lo = (jnp.arange(HEAD_DIM) % axis_dim) < (axis_dim // 2)
    sin_lo = jnp.where(lo, -sin, 0.0)
    sin_hi = jnp.where(lo, 0.0, sin)
    one, zero = jnp.ones((tm, HEAD_DIM), F32), jnp.zeros((tm, HEAD_DIM), F32)
    return (jnp.concatenate([cos, one]), jnp.concatenate([sin_lo, zero]), jnp.concatenate([sin_hi, zero]))


def _attn_full_kernel(*refs, n_src, has_sink, groups):
    if has_sink:
        sink_ref, refs = refs[0], refs[1:]
    q_ref, o_ref = refs[0], refs[1 + 2 * n_src]
    kv = refs[1:1 + 2 * n_src]
    kvh = pl.program_id(1)
    for g in range(groups):
        q = q_ref[:, g * HEAD_DIM:(g + 1) * HEAD_DIM]
        scores = [_dot_nt(q, kv[2 * j][...]) for j in range(n_src)]
        m = functools.reduce(jnp.maximum, [jnp.max(s, axis=-1, keepdims=True) for s in scores])
        if has_sink:
            sink = sink_ref[kvh * groups + g] * LOG2_E
            m = jnp.maximum(m, sink)
        es = [jnp.exp2(s - m) for s in scores]
        den = functools.reduce(jnp.add, [jnp.sum(e, axis=-1, keepdims=True) for e in es])
        if has_sink:
            den = den + jnp.exp2(sink - m)
        acc = functools.reduce(jnp.add, [_dot(es[j].astype(BF16), kv[2 * j + 1][...]) for j in range(n_src)])
        o_ref[:, g * HEAD_DIM:(g + 1) * HEAD_DIM] = (acc / den).astype(o_ref.dtype)


def _attn_full(qkv, *, batch, n_q, q_base, sources, q_col, k_col, v_col, heads, kv_heads, sink=None):
    groups = heads // kv_heads
    tq = _pick(n_q, (256, 128))
    qw = groups * HEAD_DIM
    nq = n_q // tq
    in_specs = []
    args = []
    if sink is not None:
        in_specs.append(pl.BlockSpec(memory_space=pltpu.SMEM))
        args.append(sink)
    in_specs.append(pl.BlockSpec((tq, qw), lambda b, h, i: (q_base // tq + b * nq + i, q_col // qw + h)))
    args.append(qkv)
    for base, n in sources:
        for col in (k_col, v_col):
            in_specs.append(pl.BlockSpec(
                (n, HEAD_DIM), lambda b, h, i, base=base, n=n, col=col: (base // n + b, col // HEAD_DIM + h)))
            args.append(qkv)
    kern = functools.partial(_attn_full_kernel, n_src=len(sources), has_sink=sink is not None, groups=groups)
    return pl.pallas_call(
        kern,
        grid=(batch, kv_heads, nq),
        in_specs=in_specs,
        out_specs=pl.BlockSpec((tq, qw), lambda b, h, i: (b * nq + i, h)),
        out_shape=jax.ShapeDtypeStruct((batch * n_q, heads * HEAD_DIM), BF16),
        compiler_params=_cparams(("parallel", "parallel", "arbitrary")),
        name="attn_full",
    )(*args)


def _attn_win_kernel(sink_ref, q_ref, k_ref, v_ref, kc_ref, vc_ref, o_ref, *, groups, seq):
    kvh = pl.program_id(1)
    n = pl.program_id(2)
    span = 3 * Q_BLOCK
    start = jnp.clip((n - 1) * Q_BLOCK, 0, seq - span)
    start = pl.multiple_of(start, Q_BLOCK)
    kb = k_ref[pl.ds(start, span), :]
    vb = v_ref[pl.ds(start, span), :]
    q_pos = n * Q_BLOCK + lax.broadcasted_iota(jnp.int32, (Q_BLOCK, span), 0)
    k_pos = start + lax.broadcasted_iota(jnp.int32, (Q_BLOCK, span), 1)
    valid = jnp.abs(k_pos - q_pos) <= WINDOW
    for g in range(groups):
        q = q_ref[:, g * HEAD_DIM:(g + 1) * HEAD_DIM]
        s_win = jnp.where(valid, _dot_nt(q, kb), -jnp.inf)
        s_ctx = _dot_nt(q, kc_ref[...])
        sink = sink_ref[kvh * groups + g] * LOG2_E
        m = jnp.maximum(jnp.maximum(jnp.max(s_win, axis=-1, keepdims=True),
                                    jnp.max(s_ctx, axis=-1, keepdims=True)), sink)
        e_win = jnp.exp2(s_win - m)
        e_ctx = jnp.exp2(s_ctx - m)
        den = (jnp.sum(e_win, axis=-1, keepdims=True) + jnp.sum(e_ctx, axis=-1, keepdims=True)
               + jnp.exp2(sink - m))
        acc = _dot(e_win.astype(BF16), vb) + _dot(e_ctx.astype(BF16), vc_ref[...])
        o_ref[:, g * HEAD_DIM:(g + 1) * HEAD_DIM] = (acc / den).astype(o_ref.dtype)


def _attn_window(qkv, sink, rows):
    b, s, c = rows.b, rows.s, rows.c
    groups = B_HEADS // B_KV_HEADS
    qw = groups * HEAD_DIM
    nq = s // Q_BLOCK
    q_col = A_Q + 2 * A_KV
    k_col = q_col + B_Q
    v_col = k_col + B_KV
    ctx_base = rows.n_lat // c

    def kv_spec(n, base, col):
        return pl.BlockSpec((n, HEAD_DIM), lambda bb, h, i: (base + bb, col // HEAD_DIM + h))

    return pl.pallas_call(
        functools.partial(_attn_win_kernel, groups=groups, seq=s),
        grid=(b, B_KV_HEADS, nq),
        in_specs=[pl.BlockSpec(memory_space=pltpu.SMEM),
                  pl.BlockSpec((Q_BLOCK, qw), lambda bb, h, i: (bb * nq + i, q_col // qw + h)),
                  kv_spec(s, 0, k_col), kv_spec(s, 0, v_col),
                  kv_spec(c, ctx_base, k_col), kv_spec(c, ctx_base, v_col)],
        out_specs=pl.BlockSpec((Q_BLOCK, qw), lambda bb, h, i: (bb * nq + i, h)),
        out_shape=jax.ShapeDtypeStruct((b * s, B_Q), BF16),
        compiler_params=_cparams(("parallel", "parallel", "arbitrary")),
        name="attn_window",
    )(sink, qkv, qkv, qkv, qkv, qkv)


def _delta_prep_kernel(prev_ref, cur_ref, next_ref, ab_ref, w_ref, alog_ref, dtb_ref, qkv_ref, gb_ref,
                       *, lat_blocks, seq_blocks, ctx_blocks):
    i = pl.program_id(0)
    tm = cur_ref.shape[0]
    is_lat = i < lat_blocks
    pos = jnp.where(is_lat, i % seq_blocks, (i - lat_blocks) % ctx_blocks)
    n_pos = jnp.where(is_lat, seq_blocks, ctx_blocks)
    first = pos == 0
    last = pos == n_pos - 1
    keep_prev = jnp.where(first, 0.0, 1.0)
    keep_next = jnp.where(last, 0.0, 1.0)
    ext = tm + 2 * HALO
    n_qk = 2 * C_QK_HEADS
    for j in range(C_CONV // HEAD_DIM):
        sl = slice(j * HEAD_DIM, (j + 1) * HEAD_DIM)
        xt = jnp.concatenate([prev_ref[:, sl] * keep_prev, cur_ref[:, sl], next_ref[:, sl] * keep_next], axis=0)
        acc = None
        for tap in range(CONV_K):
            shift = (CONV_K // 2 - tap) % ext
            xs = xt if shift == 0 else pltpu.roll(xt, shift, 0)
            term = xs[HALO:HALO + tm] * w_ref[tap:tap + 1, sl]
            acc = term if acc is None else acc + term
        y = _silu(acc)
        if j < n_qk:
            y = y * lax.rsqrt(jnp.sum(y * y, axis=-1, keepdims=True) + RMS_EPS)
            if j < C_QK_HEADS:
                y = y * (HEAD_DIM ** -0.5)
        qkv_ref[:, sl] = y
    ab = ab_ref[...]
    lane = lax.broadcasted_iota(jnp.int32, ab.shape, 1)
    n_g = 2 * C_V_HEADS
    z = ab + dtb_ref[...]
    softplus = jnp.maximum(z, 0.0) + jnp.log(1.0 + jnp.exp(-jnp.abs(z)))
    g = -jnp.exp(alog_ref[...]) * softplus
    gb_ref[...] = jnp.where(lane < n_g, g, jax.nn.sigmoid(ab))


def _delta_prep(rows, proj_main, proj_ab, conv_w, a_log, dt_bias):
    r = proj_main.shape[0]
    tm = rows.tm
    hb = tm // HALO
    last_halo = r // HALO - 1
    pad = AB_W - a_log.size
    alog = jnp.pad(a_log.reshape(1, -1), ((0, 0), (0, pad)))
    dtb = jnp.pad(dt_bias.reshape(1, -1), ((0, 0), (0, pad)))
    kern = functools.partial(_delta_prep_kernel, lat_blocks=rows.lat_blocks, seq_blocks=rows.s // tm,
                             ctx_blocks=rows.c // tm)
    return pl.pallas_call(
        kern,
        grid=(r // tm,),
        in_specs=[pl.BlockSpec((HALO, C_CONV), lambda i: (jnp.maximum(i * hb - 1, 0), 0)),
                  pl.BlockSpec((tm, C_CONV), lambda i: (i, 0)),
                  pl.BlockSpec((HALO, C_CONV), lambda i: (jnp.minimum((i + 1) * hb, last_halo), 0)),
                  pl.BlockSpec((tm, AB_W), lambda i: (i, 0)),
                  pl.BlockSpec((CONV_K, C_CONV), lambda i: (0, 0)),
                  _vec_spec(AB_W), _vec_spec(AB_W)],
        out_specs=[pl.BlockSpec((tm, C_CONV), lambda i: (i, 0)), pl.BlockSpec((tm, AB_W), lambda i: (i, 0))],
        out_shape=[jax.ShapeDtypeStruct((r, C_CONV), F32), jax.ShapeDtypeStruct((r, AB_W), F32)],
        compiler_params=_cparams(("parallel",)),
        name="delta_prep",
    )(proj_main, proj_main, proj_main, proj_ab, conv_w, alog, dtb)


def _delta_kernel(qkv_f_ref, gb_f_ref, qkv_r_ref, gb_r_ref, o_f_ref, o_r_ref, state_ref):
    @pl.when(pl.program_id(1) == 0)
    def _():
        state_ref[...] = jnp.zeros_like(state_ref)

    row = lax.broadcasted_iota(jnp.int32, (CHUNK, CHUNK), 0)
    col = lax.broadcasted_iota(jnp.int32, (CHUNK, CHUNK), 1)
    eye = jnp.where(row == col, 1.0, 0.0)
    rep = C_V_HEADS // C_QK_HEADS
    n_sq = int(math.log2(CHUNK)) - 1
    chains = []
    for direction, (qkv_ref, gb_ref, o_ref) in enumerate(((qkv_f_ref, gb_f_ref, o_f_ref),
                                                          (qkv_r_ref, gb_r_ref, o_r_ref))):
        if direction == 0:
            incl, strict, last = row >= col, row > col, CHUNK - 1
        else:
            incl, strict, last = row <= col, row < col, 0
        gb = gb_ref[...]
        gcum = _dot(jnp.where(incl, 1.0, 0.0), gb, precision=HIGHEST)
        gcum_t = gcum.T
        kq = []
        for qh in range(C_QK_HEADS):
            q = qkv_ref[:, qh * HEAD_DIM:(qh + 1) * HEAD_DIM]
            k = qkv_ref[:, C_QK + qh * HEAD_DIM:C_QK + (qh + 1) * HEAD_DIM]
            prod = _dot_nt(jnp.concatenate([k, q], axis=0).astype(BF16), k.astype(BF16))
            kq.append((q, k, prod[:CHUNK], prod[CHUNK:]))
        for h in range(C_V_HEADS):
            lane = direction * C_V_HEADS + h
            q, k, kk, qk = kq[h // rep]
            g_col = gcum[:, lane:lane + 1]
            beta = gb[:, 2 * C_V_HEADS + lane:2 * C_V_HEADS + lane + 1]
            decay = jnp.where(incl, jnp.exp(jnp.minimum(g_col - gcum_t[lane:lane + 1, :], 0.0)), 0.0)
            v = qkv_ref[:, 2 * C_QK + h * HEAD_DIM:2 * C_QK + (h + 1) * HEAD_DIM]
            p = jnp.where(strict, -(kk * beta) * decay, 0.0)
            chains.append(dict(
                h=h, slot=lane, o_ref=o_ref, q=q, k=k, g_col=g_col, e_g=jnp.exp(g_col),
                g_last=gcum[last:last + 1, lane:lane + 1], beta=beta, v=v, qk=qk * decay, p=p, t=eye + p))
    for _ in range(n_sq):
        for ch in chains:
            pb = ch["p"].astype(BF16)
            ch["p"] = _dot(pb, pb)
        for ch in chains:
            ch["t"] = ch["t"] + _dot(ch["t"].astype(BF16), ch["p"].astype(BF16))
    for ch in chains:
        rhs = jnp.concatenate([ch["v"] * ch["beta"], ch["k"] * (ch["beta"] * ch["e_g"])], axis=1)
        uw = _dot(ch["t"].astype(BF16), rhs.astype(BF16))
        ch["u"], ch["w"] = uw[:, :HEAD_DIM], uw[:, HEAD_DIM:]
    for ch in chains:
        ch["state"] = state_ref[ch["slot"]]
        lhs = jnp.concatenate([ch["w"], ch["q"] * ch["e_g"]], axis=0)
        ch["ws_qs"] = _dot(lhs.astype(BF16), ch["state"].astype(BF16))
    for ch in chains:
        ch["v_new"] = (ch["u"] - ch["ws_qs"][:CHUNK]).astype(BF16)
    for ch in chains:
        h = ch["h"]
        ch["o_ref"][:, h * HEAD_DIM:(h + 1) * HEAD_DIM] = ch["ws_qs"][CHUNK:] + _dot(ch["qk"].astype(BF16), ch["v_new"])
    for ch in chains:
        k_dec = ch["k"] * jnp.exp(ch["g_last"] - ch["g_col"])
        state_ref[ch["slot"]] = ch["state"] * jnp.exp(ch["g_last"]) + _dot_tn(k_dec.astype(BF16), ch["v_new"])


def _delta_scan(rows, dqkv, gb):
    b, s, c = rows.b, rows.s, rows.c
    ctx_chunks, lat_chunks = c // CHUNK, s // CHUNK
    ctx_base = rows.n_lat // CHUNK

    def fwd_map(bb, t):
        return (jnp.where(t < ctx_chunks, ctx_base + bb * ctx_chunks + t, bb * lat_chunks + t - ctx_chunks), 0)

    def rev_map(bb, t):
        ctx_i, lat_i = ctx_chunks - 1 - t, lat_chunks - 1 - (t - ctx_chunks)
        return (jnp.where(t < ctx_chunks, ctx_base + bb * ctx_chunks + ctx_i, bb * lat_chunks + lat_i), 0)

    out = jax.ShapeDtypeStruct((rows.r, C_V), F32)
    return pl.pallas_call(
        _delta_kernel,
        grid=(b, ctx_chunks + lat_chunks),
        in_specs=[pl.BlockSpec((CHUNK, C_CONV), fwd_map), pl.BlockSpec((CHUNK, AB_W), fwd_map),
                  pl.BlockSpec((CHUNK, C_CONV), rev_map), pl.BlockSpec((CHUNK, AB_W), rev_map)],
        out_specs=[pl.BlockSpec((CHUNK, C_V), fwd_map), pl.BlockSpec((CHUNK, C_V), rev_map)],
        out_shape=[out, out],
        scratch_shapes=[pltpu.VMEM((2 * C_V_HEADS, HEAD_DIM, HEAD_DIM), F32)],
        compiler_params=_cparams(("parallel", "arbitrary")),
        name="delta_scan",
    )(dqkv, gb, dqkv, gb)


def _merge_kernel(oa_ref, ob_ref, of_ref, or_ref, z_ref, nc_ref, ga_ref, gb_ref, gc_ref,
                  wa_ref, wb_ref, wc_ref, y_ref, oc_ref):
    for h in range(C_V_HEADS):
        sl = slice(h * HEAD_DIM, (h + 1) * HEAD_DIM)
        o = of_ref[:, sl] + or_ref[:, sl]
        oc_ref[:, sl] = (_rms(o, nc_ref[...]) * _silu(z_ref[:, sl])).astype(BF16)
    y = jax.nn.sigmoid(ga_ref[...].astype(F32)) * _dot(oa_ref[...], wa_ref[...])
    y = y + jax.nn.sigmoid(gb_ref[...].astype(F32)) * _dot(ob_ref[...], wb_ref[...])
    y = y + jax.nn.sigmoid(gc_ref[...].astype(F32)) * _dot(oc_ref[...], wc_ref[...])
    y_ref[...] = y.astype(BF16)


def _merge(n_rows, tm, o_a, o_b, o_f, o_r, proj_main, gates, norm_c, w_a, w_b, w_c):
    d = w_a.shape[1]
    z_blk = C_CONV // C_V

    def rspec(w, col=0):
        return pl.BlockSpec((tm, w), lambda i: (i, col))

    def wspec(kdim):
        return pl.BlockSpec((kdim, d), lambda i: (0, 0))

    return pl.pallas_call(
        _merge_kernel,
        grid=(n_rows // tm,),
        in_specs=[rspec(A_Q), rspec(B_Q), rspec(C_V), rspec(C_V), rspec(C_V, z_blk), _vec_spec(HEAD_DIM),
                  rspec(d, 0), rspec(d, 1), rspec(d, 2), wspec(A_Q), wspec(B_Q), wspec(C_V)],
        out_specs=rspec(d),
        out_shape=jax.ShapeDtypeStruct((n_rows, d), BF16),
        scratch_shapes=[pltpu.VMEM((tm, C_V), BF16)],
        compiler_params=_cparams(("parallel",)),
        name="branch_merge",
    )(o_a, o_b, o_f, o_r, proj_main, norm_c, gates, gates, gates, w_a, w_b, w_c)


def _outproj_kernel(y_ref, w_ref, x_ref, gt_ref, g_ref, b_ref, sc_ref, sh_ref, x1_ref, h_ref, *, alpha):
    y = _dot(y_ref[...], w_ref[...])
    x1 = _layernorm(alpha * x_ref[...] + gt_ref[0, 0] * y) * g_ref[...] + b_ref[...]
    x1_ref[...] = x1
    h_ref[...] = (x1 * (1.0 + sc_ref[0, 0]) + sh_ref[0, 0]).astype(BF16)


def _outproj(rows, n_rows, y, w_out, x, mod_l, ln_g, ln_b, alpha):
    d = x.shape[1]
    tm = rows.tm
    rspec = pl.BlockSpec((tm, d), lambda i: (i, 0))
    return pl.pallas_call(
        functools.partial(_outproj_kernel, alpha=alpha),
        grid=(n_rows // tm,),
        in_specs=[rspec, pl.BlockSpec((d, d), lambda i: (0, 0)), rspec, _mod_spec(rows, d, 2),
                  _vec_spec(d), _vec_spec(d), _mod_spec(rows, d, 4), _mod_spec(rows, d, 3)],
        out_specs=[rspec, rspec],
        out_shape=[jax.ShapeDtypeStruct((n_rows, d), F32), jax.ShapeDtypeStruct((n_rows, d), BF16)],
        compiler_params=_cparams(("parallel",)),
        name="out_proj_norm",
    )(y, w_out, x, mod_l, ln_g, ln_b, mod_l, mod_l)


def _router_kernel(x_ref, sc_ref, sh_ref, wr_ref, rb_ref, gate_ref, pos_ref, tpos_ref, cnt_ref, before_ref,
                   lstart_ref, tstart_ref, total_ref, carry_ref):
    @pl.when(pl.program_id(0) == 0)
    def _():
        carry_ref[...] = jnp.zeros_like(carry_ref)

    h = x_ref[...] * (1.0 + sc_ref[0, 0]) + sh_ref[0, 0]
    scores = jax.nn.sigmoid(_dot(h, wr_ref[...], precision=HIGHEST))
    tm, n_exp = scores.shape
    sel = scores + rb_ref[...]
    lane = lax.broadcasted_iota(jnp.int32, (tm, n_exp), 1)
    slot = lax.broadcasted_iota(jnp.int32, (tm, TOP_K), 1)
    gate = jnp.zeros((tm, TOP_K), F32)
    mask = jnp.zeros((tm, n_exp), F32)
    hits = []
    for k in range(TOP_K):
        best = jnp.max(sel, axis=-1, keepdims=True)
        choice = jnp.min(jnp.where(sel == best, lane, n_exp), axis=-1, keepdims=True)
        hit = lane == choice
        hits.append(hit)
        gate = jnp.where(slot == k, jnp.sum(jnp.where(hit, scores, 0.0), axis=-1, keepdims=True), gate)
        sel = jnp.where(hit, -jnp.inf, sel)
        mask = jnp.where(hit, 1.0, mask)
    gate_ref[...] = gate / jnp.sum(gate, axis=-1, keepdims=True) * ROUTED_SCALE
    r_i = lax.broadcasted_iota(jnp.int32, (tm, tm), 0)
    c_i = lax.broadcasted_iota(jnp.int32, (tm, tm), 1)
    earlier = jnp.where(c_i < r_i, 1.0, 0.0).astype(BF16)
    within = _dot(earlier, mask.astype(BF16))
    cnt = jnp.sum(mask, axis=0, keepdims=True)
    run_rows = jnp.floor((cnt + (RUN - 1)) * (1.0 / RUN)) * RUN
    e_r = lax.broadcasted_iota(jnp.int32, (n_exp, n_exp), 0)
    e_c = lax.broadcasted_iota(jnp.int32, (n_exp, n_exp), 1)
    lengths = jnp.concatenate([run_rows, cnt, jnp.zeros((6, n_exp), F32)], axis=0)
    starts = _dot(lengths, jnp.where(e_r < e_c, 1.0, 0.0), precision=HIGHEST)
    lstart, tstart = starts[0:1], starts[1:2]
    pos = jnp.zeros((tm, TOP_K), F32)
    tpos = jnp.zeros((tm, TOP_K), F32)
    for k in range(TOP_K):
        pos = jnp.where(slot == k, jnp.sum(jnp.where(hits[k], within + lstart, 0.0), axis=-1, keepdims=True), pos)
        tpos = jnp.where(slot == k, jnp.sum(jnp.where(hits[k], within + tstart, 0.0), axis=-1, keepdims=True), tpos)
    pos_ref[...] = pos.astype(jnp.int32)
    tpos_ref[...] = tpos.astype(jnp.int32)
    cnt_ref[0] = cnt.astype(jnp.int32)
    before_ref[0] = carry_ref[...].astype(jnp.int32)
    lstart_ref[0] = lstart.astype(jnp.int32)
    tstart_ref[0] = tstart.astype(jnp.int32)
    carry_ref[...] = carry_ref[...] + cnt
    total_ref[...] = carry_ref[...]


def _router(rows, n_rows, x1, mod_l, w_router, router_bias):
    d, n_exp = w_router.shape
    tm = rows.tm
    n_blk = n_rows // tm
    kspec = pl.BlockSpec((tm, TOP_K), lambda i: (i, 0))
    espec = pl.BlockSpec((1, 1, n_exp), lambda i: (i, 0, 0))
    per_block = jax.ShapeDtypeStruct((n_blk, 1, n_exp), jnp.int32)
    return pl.pallas_call(
        _router_kernel,
        grid=(n_blk,),
        in_specs=[pl.BlockSpec((tm, d), lambda i: (i, 0)), _mod_spec(rows, d, 4), _mod_spec(rows, d, 3),
                  pl.BlockSpec((d, n_exp), lambda i: (0, 0)), _vec_spec(n_exp)],
        out_specs=[kspec, kspec, kspec, espec, espec, espec, espec, _vec_spec(n_exp)],
        out_shape=[jax.ShapeDtypeStruct((n_rows, TOP_K), F32), jax.ShapeDtypeStruct((n_rows, TOP_K), jnp.int32),
                   jax.ShapeDtypeStruct((n_rows, TOP_K), jnp.int32),
                   per_block, per_block, per_block, per_block, jax.ShapeDtypeStruct((1, n_exp), F32)],
        scratch_shapes=[pltpu.VMEM((1, n_exp), F32)],
        compiler_params=_cparams(("arbitrary",)),
        name="router",
    )(x1, mod_l, mod_l, w_router, router_bias.reshape(1, n_exp))


def _local_rows(tm, n_exp):
    return tm * TOP_K + n_exp * RUN


def _token_rows(tokens, s, per_token):
    return pl.ds(s, tokens, stride=per_token) if per_token > 1 else pl.ds(s, tokens)


def _pack_words(lo, hi):
    return (lax.bitcast_convert_type(lo, jnp.uint32) >> 16) | (lax.bitcast_convert_type(hi, jnp.uint32)
                                                               & jnp.uint32(0xFFFF0000))


def _unpack_words(p):
    return (lax.bitcast_convert_type(p << 16, F32).astype(BF16),
            lax.bitcast_convert_type(p & jnp.uint32(0xFFFF0000), F32).astype(BF16))


def _run_dmas(n_exp, cnt_ref, src_start, dst_start, src_ref, dst_ref, sem, per_token):
    total = jnp.int32(0)
    rows = RUN * per_token
    for e in range(n_exp):
        n_run = (cnt_ref[0, 0, e] + (RUN - 1)) // RUN
        s0, d0 = src_start(e), dst_start(e)

        def body(c, carry, s0=s0, d0=d0):
            pltpu.make_async_copy(src_ref.at[pl.ds(pl.multiple_of((s0 + c * RUN) * per_token, per_token), rows)],
                                  dst_ref.at[pl.ds(pl.multiple_of((d0 + c * RUN) * per_token, per_token), rows)],
                                  sem).start()
            return carry

        lax.fori_loop(0, n_run, body, 0)
        total = total + n_run
    return total


def _wait_runs(n, src_ref, dst_ref, sem, per_token):
    rows = RUN * per_token

    def body(c, carry):
        pltpu.make_async_copy(src_ref.at[pl.ds(0, rows)], dst_ref.at[pl.ds(0, rows)], sem).wait()
        return carry

    lax.fori_loop(0, n, body, 0)


def _dispatch_kernel(gs_ref, count_ref, padded_ref, post_ref, cnt_ref, carry_ref, ls_ref, h_ref, xs_ref,
                     loc, zbuf, sem, pending_ref, *, per_token, n_exp):
    i = pl.program_id(0)
    tm, d = h_ref.shape
    half = d // 2
    n_local = loc.shape[1] // per_token
    chunk = n_local // DISPATCH_CHUNKS
    buf = i % 2
    for r0 in range(0, n_local, chunk):
        p_iota = r0 + lax.broadcasted_iota(jnp.int32, (chunk, tm), 0)
        onehot = jnp.zeros((chunk, tm), F32)
        for k in range(TOP_K):
            onehot = jnp.where(p_iota == post_ref[k:k + 1, :], 1.0, onehot)
        onehot = onehot.astype(BF16)
        words = _pack_words(_dot(onehot, h_ref[:, :half]), _dot(onehot, h_ref[:, half:]))
        for s in range(per_token):
            loc[buf, _token_rows(chunk, r0 * per_token + s, per_token), :] = words[:, s * LANES:(s + 1) * LANES]

    @pl.when(i > 0)
    def _():
        _wait_runs(pending_ref[0], loc.at[0], xs_ref, sem, per_token)

    pending_ref[0] = _run_dmas(n_exp, cnt_ref, lambda e: ls_ref[0, 0, e], lambda e: gs_ref[e] + carry_ref[0, 0, e],
                               loc.at[buf], xs_ref, sem, per_token)

    @pl.when(i == pl.num_programs(0) - 1)
    def _():
        _wait_runs(pending_ref[0], loc.at[0], xs_ref, sem, per_token)
        zbuf[...] = jnp.zeros_like(zbuf)
        rows = RUN * per_token
        n_zero = jnp.int32(0)
        for e in range(n_exp):
            first = gs_ref[e] + count_ref[e]
            n_piece = (padded_ref[e] - count_ref[e]) // RUN

            def body(c, carry, first=first):
                pltpu.make_async_copy(zbuf, xs_ref.at[pl.ds(pl.multiple_of((first + c * RUN) * per_token, per_token),
                                                            rows)], sem).start()
                return carry

            lax.fori_loop(0, n_piece, body, 0)
            n_zero = n_zero + n_piece
        used = gs_ref[n_exp - 1] + padded_ref[n_exp - 1]
        n_tail = (xs_ref.shape[0] // per_token - used) // RUN

        def tail_body(c, carry):
            pltpu.make_async_copy(zbuf, xs_ref.at[pl.ds(pl.multiple_of((used + c * RUN) * per_token, per_token),
                                                        rows)], sem).start()
            return carry

        lax.fori_loop(0, n_tail, tail_body, 0)
        _wait_runs(n_zero + n_tail, loc.at[0], xs_ref, sem, per_token)
        for e in range(n_exp):
            end = gs_ref[e] + padded_ref[e]
            pltpu.make_async_copy(zbuf, xs_ref.at[pl.ds(pl.multiple_of((end - RUN) * per_token, per_token), rows)],
                                  sem).start()
        _wait_runs(n_exp, loc.at[0], xs_ref, sem, per_token)


def _dispatch(rows, n_rows, h2, pos_t, cnt, carry, lstart, group_start, count, padded, n_slots):
    d = h2.shape[1]
    tm = rows.tm
    n_exp = count.shape[0]
    per_token = d // 2 // LANES
    n_local = tm * TOP_K + DISPATCH_TAIL
    assert DISPATCH_TAIL >= RUN and n_local % (8 * DISPATCH_CHUNKS) == 0
    smem_blk = pl.BlockSpec((1, 1, n_exp), lambda i, *_: (i, 0, 0), memory_space=pltpu.SMEM)
    grid_spec = pltpu.PrefetchScalarGridSpec(
        num_scalar_prefetch=3,
        grid=(n_rows // tm,),
        in_specs=[pl.BlockSpec((TOP_K, tm), lambda i, *_: (0, i)), smem_blk, smem_blk, smem_blk,
                  pl.BlockSpec((tm, d), lambda i, *_: (i, 0))],
        out_specs=pl.BlockSpec(memory_space=pl.ANY),
        scratch_shapes=[pltpu.VMEM((2, n_local * per_token, LANES), jnp.uint32),
                        pltpu.VMEM((RUN * per_token, LANES), jnp.uint32),
                        pltpu.SemaphoreType.DMA(()), pltpu.SMEM((1,), jnp.int32)],
    )
    return pl.pallas_call(
        functools.partial(_dispatch_kernel, per_token=per_token, n_exp=n_exp),
        grid_spec=grid_spec,
        out_shape=jax.ShapeDtypeStruct((n_slots * per_token, LANES), jnp.uint32),
        compiler_params=_cparams(("arbitrary",)),
        name="moe_dispatch",
    )(group_start, count, padded, pos_t, cnt, carry, lstart, h2)


def _expert_block_kernel(be_ref, nu_ref, x_hbm, w1_ref, w3_ref, w2_ref, o_ref, xbuf, xsem, xs_ref, w1b, w3b, w2b,
                         prev_ref):
    blk = pl.program_id(0)
    n_used = nu_ref[0]
    d = xs_ref.shape[1]
    n_buf, block_rows = xbuf.shape[0], xbuf.shape[1]

    def fetch(b):
        slot = b % n_buf
        start = b * block_rows if isinstance(b, int) else pl.multiple_of(b * block_rows, block_rows)
        return pltpu.make_async_copy(x_hbm.at[pl.ds(start, block_rows)], xbuf.at[slot], xsem.at[slot])

    @pl.when(blk == 0)
    def _():
        prev_ref[0] = -1
        for b in range(n_buf - 1):
            pl.when(b < n_used)(lambda b=b: fetch(b).start())

    @pl.when(blk + (n_buf - 1) < n_used)
    def _():
        fetch(blk + (n_buf - 1)).start()

    @pl.when(blk < n_used)
    def _():
        fetch(blk).wait()
        x_ref = xbuf.at[blk % n_buf]
        e = be_ref[blk]

        @pl.when(e != prev_ref[0])
        def _():
            w1b[...] = w1_ref[0, 0].astype(BF16)
            w3b[...] = w3_ref[0, 0].astype(BF16)
            w2b[...] = w2_ref[0, 0].astype(BF16)
            prev_ref[0] = e

        half = d // 2
        per_token = half // LANES
        for s in range(per_token):
            lo, hi = _unpack_words(x_ref[_token_rows(MOE_ROWS, s, per_token), :])
            xs_ref[:, s * LANES:(s + 1) * LANES] = lo
            xs_ref[:, half + s * LANES:half + (s + 1) * LANES] = hi
        hidden = _silu(_dot(xs_ref[...], w1b[...])) * _dot(xs_ref[...], w3b[...])
        y = _dot(hidden.astype(BF16), w2b[...])
        for s in range(per_token):
            o_ref[_token_rows(MOE_ROWS, s, per_token), :] = _pack_words(
                y[:, s * LANES:(s + 1) * LANES].astype(BF16).astype(F32),
                y[:, half + s * LANES:half + (s + 1) * LANES].astype(BF16).astype(F32))

    @pl.when(blk >= nu_ref[0])
    def _():
        o_ref[...] = jnp.zeros_like(o_ref)


def _expert_blocks(xs, block_expert, n_used, w1, w3, w2, layer):
    d, ff = w1.shape[-2:]
    per_token = d // 2 // LANES
    n_blocks = block_expert.shape[0]
    blk_spec = pl.BlockSpec((MOE_ROWS * per_token, LANES), lambda i, be, nu: (i, 0))

    def wmap(i, be, nu):
        return (layer, be[i], 0, 0)

    grid_spec = pltpu.PrefetchScalarGridSpec(
        num_scalar_prefetch=2,
        grid=(n_blocks,),
        in_specs=[pl.BlockSpec(memory_space=pl.ANY), pl.BlockSpec((1, 1, d, ff), wmap),
                  pl.BlockSpec((1, 1, d, ff), wmap), pl.BlockSpec((1, 1, ff, d), wmap)],
        out_specs=blk_spec,
        scratch_shapes=[pltpu.VMEM((EXPERT_LOOKAHEAD + 1, MOE_ROWS * per_token, LANES), jnp.uint32),
                        pltpu.SemaphoreType.DMA((EXPERT_LOOKAHEAD + 1,)),
                        pltpu.VMEM((MOE_ROWS, d), BF16),
                        pltpu.VMEM((d, ff), BF16), pltpu.VMEM((d, ff), BF16), pltpu.VMEM((ff, d), BF16),
                        pltpu.SMEM((1,), jnp.int32)],
    )
    return pl.pallas_call(
        _expert_block_kernel,
        grid_spec=grid_spec,
        out_shape=jax.ShapeDtypeStruct(xs.shape, jnp.uint32),
        compiler_params=_cparams(("arbitrary",)),
        name="experts",
    )(block_expert, n_used, xs, w1, w3, w2)


def _gather_combine_kernel(*refs, alpha, emit_h, per_token, n_exp):
    gs_ref, pos_ref, gate_ref, cnt_ref, carry_ref, ls_ref, ys_ref, h_ref, ws1_ref, ws3_ref, ws2_ref = refs[:11]
    x_ref, gt_ref, g_ref, b_ref = refs[11:15]
    n_out = 2 if emit_h else 1
    out_refs = refs[-(n_out + 3):-3]
    loc, wts_ref, sem = refs[-3:]
    tm, d = h_ref.shape
    n_local = loc.shape[0] // per_token

    @pl.when(pl.program_id(0) == 0)
    def _():
        loc[...] = jnp.zeros_like(loc)

    n_dma = _run_dmas(n_exp, cnt_ref, lambda e: gs_ref[e] + carry_ref[0, 0, e], lambda e: ls_ref[0, 0, e],
                      ys_ref, loc, sem, per_token)
    hidden = _silu(_dot(h_ref[...], ws1_ref[...])) * _dot(h_ref[...], ws3_ref[...])
    f = _dot(hidden.astype(BF16), ws2_ref[...])
    p_iota = lax.broadcasted_iota(jnp.int32, (tm, n_local), 1)
    weights = jnp.zeros((tm, n_local), F32)
    for k in range(TOP_K):
        weights = jnp.where(p_iota == pos_ref[:, k:k + 1], gate_ref[:, k:k + 1], weights)
    wts_ref[...] = weights.astype(BF16)
    _wait_runs(n_dma, ys_ref, loc, sem, per_token)
    group = 2 if per_token % 2 == 0 else 1
    lo_cols, hi_cols = [], []
    for s0 in range(0, per_token, group):
        planes = [_unpack_words(loc[_token_rows(n_local, s, per_token), :]) for s in range(s0, s0 + group)]
        lo_cols.append(_dot(wts_ref[...], jnp.concatenate([lo for lo, _ in planes], axis=1)))
        hi_cols.append(_dot(wts_ref[...], jnp.concatenate([hi for _, hi in planes], axis=1)))
    f = f + jnp.concatenate(lo_cols + hi_cols, axis=1)
    x2 = _layernorm(alpha * x_ref[...] + gt_ref[0, 0] * f) * g_ref[...] + b_ref[...]
    out_refs[0][...] = x2
    if emit_h:
        sc_ref, sh_ref = refs[15:17]
        out_refs[1][...] = (x2 * (1.0 + sc_ref[0, 0]) + sh_ref[0, 0]).astype(BF16)


def _gather_combine(rows, n_rows, ys, pos, gate, cnt, carry, lstart, group_start, h2, ws1, ws3, ws2, x1, mod_l,
                    ln_g, ln_b, alpha, mod_next):
    d = x1.shape[1]
    sf = ws1.shape[1]
    tm = rows.tm
    n_exp = group_start.shape[0]
    per_token = d // 2 // LANES
    n_local = _local_rows(tm, n_exp)

    def rmap(i, *_):
        return (i, 0)

    def const(i, *_):
        return (0, 0)

    def mod_spec(which):
        return pl.BlockSpec((1, 1, 1, d), lambda i, *_: (rows.group(i), which, 0, 0))

    rspec = pl.BlockSpec((tm, d), rmap)
    kspec = pl.BlockSpec((tm, TOP_K), rmap)
    smem_blk = pl.BlockSpec((1, 1, n_exp), lambda i, *_: (i, 0, 0), memory_space=pltpu.SMEM)
    in_specs = [kspec, kspec, smem_blk, smem_blk, smem_blk, pl.BlockSpec(memory_space=pl.ANY), rspec,
                pl.BlockSpec((d, sf), const), pl.BlockSpec((d, sf), const), pl.BlockSpec((sf, d), const),
                rspec, mod_spec(5), pl.BlockSpec((1, d), const), pl.BlockSpec((1, d), const)]
    args = [pos, gate, cnt, carry, lstart, ys, h2, ws1, ws3, ws2, x1, mod_l, ln_g, ln_b]
    emit_h = mod_next is not None
    if emit_h:
        in_specs += [mod_spec(1), mod_spec(0)]
        args += [mod_next, mod_next]
        out_specs = [rspec, rspec]
        out_shape = [jax.ShapeDtypeStruct((n_rows, d), F32), jax.ShapeDtypeStruct((n_rows, d), BF16)]
    else:
        out_specs = rspec
        out_shape = jax.ShapeDtypeStruct((n_rows, d), F32)
    grid_spec = pltpu.PrefetchScalarGridSpec(
        num_scalar_prefetch=1,
        grid=(n_rows // tm,),
        in_specs=in_specs,
        out_specs=out_specs,
        scratch_shapes=[pltpu.VMEM((n_local * per_token, LANES), jnp.uint32), pltpu.VMEM((tm, n_local), BF16),
                        pltpu.SemaphoreType.DMA(())],
    )
    return pl.pallas_call(
        functools.partial(_gather_combine_kernel, alpha=alpha, emit_h=emit_h, per_token=per_token, n_exp=n_exp),
        grid_spec=grid_spec,
        out_shape=out_shape,
        compiler_params=_cparams(("arbitrary",)),
        name="moe_combine_norm",
    )(group_start, *args)


def _permute_w_in(w):
    a_end = A_Q + 2 * A_KV
    b_end = a_end + B_Q + 2 * B_KV
    c_end = b_end + C_CONV + C_V
    ab_end = c_end + 4 * C_V_HEADS
    main = jnp.concatenate([w[:, b_end:c_end], w[:, :b_end]], axis=1).astype(BF16)
    ab = jnp.pad(w[:, c_end:ab_end], ((0, 0), (0, AB_W - 4 * C_V_HEADS))).astype(BF16)
    gates = w[:, ab_end:].astype(BF16)
    return main, gates, ab


def kernel(x, c, ctx, c_ctx, w_mod, b_mod, w_in, q_norm_a, k_norm_a, sink_b, conv_c, a_log_c, dt_bias_c, norm_c,
           w_br_a, w_br_b, w_br_c, w_out, ln1_g, ln1_b, w_router, router_bias, w1, w3, w2, ws1, ws3, ws2,
           ln2_g, ln2_b):
    b, s, d = x.shape
    n_ctx = ctx.shape[1]
    depth = w_mod.shape[0]
    n_exp = w_router.shape[-1]
    alpha = (2.0 * depth) ** 0.25
    rows = _Rows(b, s, n_ctx)
    tm = rows.tm

    c_all = jnp.concatenate([c, c_ctx[None, :], jnp.zeros((8 - b - 1, d), F32)], axis=0)
    mod = _modulation(c_all, w_mod, b_mod).reshape(depth, 8, N_MOD, 1, d)
    tables = _rope_tables(s, tm)

    x_cur, h = _entry(rows, x.reshape(b * s, d), ctx.reshape(b * n_ctx, d), mod[0])

    for l in range(depth):
        need_ctx = l < depth - 1
        n_rows = rows.r if need_ctx else rows.n_lat
        mod_l = mod[l]
        w_main, w_gates, w_ab = _permute_w_in(w_in[l])
        proj_main = _matmul(h, w_main, F32)
        gates = _matmul(h, w_gates, BF16)
        proj_ab = _matmul(h, w_ab, F32)

        qkv = _attn_prep(rows, proj_main, tables, q_norm_a[l].reshape(1, -1), k_norm_a[l].reshape(1, -1))
        a_k, a_v = A_Q, A_Q + A_KV
        o_a = _attn_full(qkv, batch=b, n_q=s, q_base=0, sources=[(rows.n_lat, n_ctx), (0, s)],
                         q_col=0, k_col=a_k, v_col=a_v, heads=A_HEADS, kv_heads=A_KV_HEADS)
        o_b = _attn_window(qkv, sink_b[l], rows)
        if need_ctx:
            b_q = A_Q + 2 * A_KV
            o_a_c = _attn_full(qkv, batch=b, n_q=n_ctx, q_base=rows.n_lat, sources=[(rows.n_lat, n_ctx)],
                               q_col=0, k_col=a_k, v_col=a_v, heads=A_HEADS, kv_heads=A_KV_HEADS)
            o_b_c = _attn_full(qkv, batch=b, n_q=n_ctx, q_base=rows.n_lat, sources=[(rows.n_lat, n_ctx)],
                               q_col=b_q, k_col=b_q + B_Q, v_col=b_q + B_Q + B_KV, heads=B_HEADS,
                               kv_heads=B_KV_HEADS, sink=sink_b[l])
            o_a = jnp.concatenate([o_a, o_a_c], axis=0)
            o_b = jnp.concatenate([o_b, o_b_c], axis=0)

        dqkv, gbeta = _delta_prep(rows, proj_main, proj_ab, conv_c[l], a_log_c[l], dt_bias_c[l])
        o_f, o_r = _delta_scan(rows, dqkv, gbeta)

        y = _merge(n_rows, tm, o_a, o_b, o_f, o_r, proj_main, gates, norm_c[l].reshape(1, -1),
                   w_br_a[l].astype(BF16), w_br_b[l].astype(BF16), w_br_c[l].astype(BF16))
        x1, h2 = _outproj(rows, n_rows, y, w_out[l].astype(BF16), x_cur, mod_l,
                          ln1_g[l].reshape(1, -1), ln1_b[l].reshape(1, -1), alpha)

        gate, pos, tpos, cnt, before, lstart, tstart, counts = _router(rows, n_rows, x1, mod_l, w_router[l],
                                                                       router_bias[l])
        counts = counts[0].astype(jnp.int32)
        padded = (counts + RUN + MOE_ROWS - 1) // MOE_ROWS * MOE_ROWS
        padded_end = jnp.cumsum(padded)
        group_start = padded_end - padded
        n_blocks = -(-(n_rows * TOP_K + n_exp * RUN) // MOE_ROWS) + n_exp
        block_start = jnp.arange(n_blocks, dtype=jnp.int32)[:, None] * MOE_ROWS
        block_expert = jnp.minimum(jnp.sum((padded_end[None, :] <= block_start).astype(jnp.int32), axis=-1), n_exp - 1)
        n_used = (padded_end[-1:] // MOE_ROWS).astype(jnp.int32)
        xs = _dispatch(rows, n_rows, h2, tpos.T, cnt, before, tstart, group_start, counts, padded,
                       n_blocks * MOE_ROWS)
        ys = _expert_blocks(xs, block_expert, n_used, w1, w3, w2, l)

        mod_next = mod[l + 1] if need_ctx else None
        out = _gather_combine(rows, n_rows, ys, pos, gate, cnt, before, lstart, group_start, h2,
                              ws1[l].astype(BF16), ws3[l].astype(BF16), ws2[l].astype(BF16),
                              x1, mod_l, ln2_g[l].reshape(1, -1), ln2_b[l].reshape(1, -1), alpha, mod_next)
        if need_ctx:
            x_cur, h = out
        else:
            x_cur = out
    return x_cur[:rows.n_lat].reshape(b, s, d)
```
